```python
import math
import jax, jax.numpy as jnp
from jax import lax
import numpy as np

D_MODEL = 1024
BATCH = 2
SEQ = 8192
DEPTH = 1
DEC_BATCH = 128
DEC_SEQ = 4
PAST_LEN = 8192
PAGE_SIZE = 128

MLA_HEADS = 8
MLA_NOPE = 64
MLA_ROPE = 32
MLA_V = 64
MLA_Q_LORA = 256
MLA_KV_LORA = 128
MLA_SCALE = (MLA_NOPE + MLA_ROPE) ** -0.5
Q_BLOCK = 128
RET_HEADS = 4
RET_DK = 128
RET_DV = 128
RET_CHUNK = 128
N_EXPERTS = 64
N_GROUPS = 8
TOPK_GROUPS = 4
TOP_K = 6
EXPERT_FF = 256
SHARED_FF = 256
ROUTED_SCALE = 2.5
MOE_BLOCK = 128
ROPE_BASE = 10000.0
LN_EPS = 1e-5
RMS_EPS = 1e-6
DEEPNORM_ALPHA = (2 * DEPTH) ** 0.25
DEEPNORM_BETA = (8 * DEPTH) ** -0.25
IN_SPLITS = (MLA_Q_LORA, MLA_KV_LORA, MLA_ROPE, RET_HEADS * RET_DK, RET_HEADS * RET_DK,
             RET_HEADS * RET_DV, RET_HEADS * RET_DV, D_MODEL, D_MODEL)
IN_WIDTH = sum(IN_SPLITS)

kernel_name = 'hybrid_mla_retention_moe_deepnorm_step'


def layer_norm(x, g, b):
    xf = x.astype(jnp.float32)
    mu = jnp.mean(xf, axis=-1, keepdims=True)
    var = jnp.mean(jnp.square(xf - mu), axis=-1, keepdims=True)
    return ((xf - mu) * lax.rsqrt(var + LN_EPS)).astype(x.dtype) * g + b


def rms_norm(x, g):
    xf = x.astype(jnp.float32)
    inv = lax.rsqrt(jnp.mean(jnp.square(xf), axis=-1, keepdims=True) + RMS_EPS)
    return (xf * inv).astype(x.dtype) * g


def rope(x, pos):
    half = x.shape[-1] // 2
    inv = ROPE_BASE ** (-jnp.arange(half, dtype=jnp.float32) / half)
    ang = pos.astype(jnp.float32)[:, None] * inv[None, :]
    cos, sin = jnp.cos(ang), jnp.sin(ang)
    if x.ndim == 4:
        cos, sin = cos[:, None, :], sin[:, None, :]
    xf = x.astype(jnp.float32)
    x1, x2 = xf[..., :half], xf[..., half:]
    return jnp.concatenate([x1 * cos - x2 * sin, x2 * cos + x1 * sin], axis=-1).astype(x.dtype)


def mixer_inputs(x, pos, w_in, q_norm_g, w_q_up, kv_norm_g):
    B, L, _ = x.shape
    cuts = [int(c) for c in np.cumsum(IN_SPLITS)[:-1]]
    cq, ckv, kr, rq, rk, rv, rg, ga, gb = jnp.split(x @ w_in, cuts, axis=-1)
    q = (rms_norm(cq, q_norm_g) @ w_q_up).reshape(B, L, MLA_HEADS, MLA_NOPE + MLA_ROPE)
    q_nope = q[..., :MLA_NOPE]
    q_rope = rope(q[..., MLA_NOPE:], pos)
    c_kv = rms_norm(ckv, kv_norm_g)
    k_rope = rope(kr, pos)
    rq = rope(rq.reshape(B, L, RET_HEADS, RET_DK), pos)
    rk = rope(rk.reshape(B, L, RET_HEADS, RET_DK), pos) * (RET_DK ** -0.5)
    rv = rv.reshape(B, L, RET_HEADS, RET_DV)
    return q_nope, q_rope, c_kv, k_rope, rq, rk, rv, rg, ga, gb


def mla_prompt(q_nope, q_rope, k_nope, k_rope, v):
    B, S, H, _ = q_nope.shape
    kpos = jnp.arange(S)

    def one_block(i):
        qn = lax.dynamic_slice_in_dim(q_nope, i * Q_BLOCK, Q_BLOCK, axis=1)
        qr = lax.dynamic_slice_in_dim(q_rope, i * Q_BLOCK, Q_BLOCK, axis=1)
        s = (jnp.einsum('bqhd,bkhd->bhqk', qn, k_nope)
             + jnp.einsum('bqhr,bkr->bhqk', qr, k_rope)).astype(jnp.float32) * MLA_SCALE
        qpos = i * Q_BLOCK + jnp.arange(Q_BLOCK)
        s = jnp.where(kpos[None, :] <= qpos[:, None], s, -jnp.inf)
        p = jax.nn.softmax(s, axis=-1).astype(v.dtype)
        return jnp.einsum('bhqk,bkhd->bqhd', p, v)

    o = lax.map(one_block, jnp.arange(S // Q_BLOCK))
    return o.transpose(1, 0, 2, 3, 4).reshape(B, S, H * MLA_V)


def mla_sample(q_nope, q_rope, c_kv, k_rope, past_ckv, past_kr, w_uk, w_uv):
    B, Q, H, _ = q_nope.shape
    P = past_ckv.shape[1]
    q_lat = jnp.einsum('bqhn,chn->bqhc', q_nope, w_uk.reshape(MLA_KV_LORA, MLA_HEADS, MLA_NOPE))
    s_past = (jnp.einsum('bqhc,bkc->bhqk', q_lat, past_ckv)
              + jnp.einsum('bqhr,bkr->bhqk', q_rope, past_kr))
    s_new = (jnp.einsum('bqhc,bkc->bhqk', q_lat, c_kv)
             + jnp.einsum('bqhr,bkr->bhqk', q_rope, k_rope))
    causal = jnp.arange(Q)[None, :] <= jnp.arange(Q)[:, None]
    s_new = jnp.where(causal, s_new.astype(jnp.float32), -jnp.inf)
    s = jnp.concatenate([s_past.astype(jnp.float32), s_new], axis=-1) * MLA_SCALE
    p = jax.nn.softmax(s, axis=-1).astype(past_ckv.dtype)
    o_lat = (jnp.einsum('bhqk,bkc->bqhc', p[..., :P], past_ckv)
             + jnp.einsum('bhqk,bkc->bqhc', p[..., P:], c_kv))
    o = jnp.einsum('bqhc,chv->bqhv', o_lat, w_uv.reshape(MLA_KV_LORA, MLA_HEADS, MLA_V))
    return o.reshape(B, Q, H * MLA_V)


def retention_chunked(q, k, v, s0):
    B, L, H, _ = q.shape
    C = math.gcd(L, RET_CHUNK)
    n = L // C
    lg = jnp.log(1.0 - 2.0 ** (-5.0 - jnp.arange(H, dtype=jnp.float32)))
    idx = jnp.arange(C, dtype=jnp.float32)
    diff = idx[:, None] - idx[None, :]
    dmask = jnp.where(diff >= 0, jnp.exp(jnp.maximum(diff, 0.0)[None] * lg[:, None, None]), 0.0)
    q_dec = jnp.exp((idx[:, None] + 1.0) * lg[None, :])
    k_dec = jnp.exp((C - 1.0 - idx)[:, None] * lg[None, :])
    s_dec = jnp.exp(C * lg)

    def to_chunks(t):
        return t.astype(jnp.float32).reshape(B, n, C, H, t.shape[-1]).transpose(1, 0, 2, 3, 4)

    def step(S, xs):
        qc, kc, vc = xs
        att = jnp.einsum('bihd,bjhd->bhij', qc, kc) * dmask[None]
        o = (jnp.einsum('bhij,bjhv->bihv', att, vc)
             + jnp.einsum('bihd,bhdv->bihv', qc * q_dec[None, :, :, None], S))
        S = S * s_dec[None, :, None, None] + jnp.einsum('bjhd,bjhv->bhdv', kc * k_dec[None, :, :, None], vc)
        return S, o

    S, o = lax.scan(step, s0.astype(jnp.float32), (to_chunks(q), to_chunks(k), to_chunks(v)))
    o = o.transpose(1, 0, 2, 3, 4).reshape(B, L, H, v.shape[-1])
    return o, S.astype(s0.dtype)


def head_norm(o, g, dtype):
    mu = jnp.mean(o, axis=-1, keepdims=True)
    var = jnp.mean(jnp.square(o - mu), axis=-1, keepdims=True)
    n = (o - mu) * lax.rsqrt(var + LN_EPS)
    B, L = o.shape[:2]
    return n.reshape(B, L, -1).astype(dtype) * g


def swiglu(x, wg, wu, wd):
    return (jax.nn.silu(x @ wg) * (x @ wu)) @ wd


def route(h2d, w_router, router_bias):
    T = h2d.shape[0]
    scores = jax.nn.sigmoid(h2d.astype(jnp.float32) @ w_router.astype(jnp.float32))
    sel = scores + router_bias.astype(jnp.float32)[None, :]
    grp = sel.reshape(T, N_GROUPS, N_EXPERTS // N_GROUPS)
    grp_score = jnp.sum(lax.top_k(grp, 2)[0], axis=-1)
    _, gidx = lax.top_k(grp_score, TOPK_GROUPS)
    gmask = jnp.sum(jax.nn.one_hot(gidx, N_GROUPS, dtype=jnp.float32), axis=-2)
    emask = jnp.repeat(gmask, N_EXPERTS // N_GROUPS, axis=-1)
    _, eidx = lax.top_k(jnp.where(emask > 0, sel, -jnp.inf), TOP_K)
    w = jnp.take_along_axis(scores, eidx, axis=-1)
    w = w / jnp.sum(w, axis=-1, keepdims=True) * ROUTED_SCALE
    return eidx, w


def routed_experts(x2d, eidx, ew, w_gate, w_up, w_down):
    T, D = x2d.shape
    A = T * TOP_K
    flat_e = eidx.reshape(A)
    flat_w = ew.reshape(A).astype(x2d.dtype)
    flat_t = jnp.arange(A, dtype=jnp.int32) // TOP_K
    order = jnp.argsort(flat_e)
    se, st, sw = flat_e[order], flat_t[order], flat_w[order]
    counts = jnp.bincount(flat_e, length=N_EXPERTS)
    padded = (counts + MOE_BLOCK - 1) // MOE_BLOCK * MOE_BLOCK
    pad_end = jnp.cumsum(padded)
    pad_start = pad_end - padded
    start = jnp.cumsum(counts) - counts
    dest = pad_start[se] + jnp.arange(A, dtype=jnp.int32) - start[se]
    n_blocks = (A + N_EXPERTS * (MOE_BLOCK - 1) + MOE_BLOCK - 1) // MOE_BLOCK
    P = n_blocks * MOE_BLOCK
    row_tok = jnp.zeros((P,), jnp.int32).at[dest].set(st)
    row_w = jnp.zeros((P,), x2d.dtype).at[dest].set(sw)
    blk_e = jnp.minimum(jnp.searchsorted(pad_end, jnp.arange(n_blocks) * MOE_BLOCK, side='right'),
                        N_EXPERTS - 1)
    xb = x2d[row_tok].reshape(n_blocks, MOE_BLOCK, D)

    def expert_block(args):
        xblk, e = args
        return swiglu(xblk, w_gate[e], w_up[e], w_down[e])

    yb = lax.map(expert_block, (xb, blk_e)).reshape(P, D)
    return jax.ops.segment_sum(yb * row_w[:, None], row_tok, num_segments=T)


def merge_and_channel_mix(x, o_mla, o_ret, rg, ga, gb, ret_gn_g, w_mla_o, w_ret_o, w_out,
                          ln1_g, ln1_b, w_router, router_bias, w_exp_gate, w_exp_up, w_exp_down,
                          w_sh_gate, w_sh_up, w_sh_down, ln2_g, ln2_b):
    B, L, D = x.shape
    ret = jax.nn.silu(rg) * head_norm(o_ret, ret_gn_g, x.dtype)
    y_a = o_mla @ w_mla_o
    y_b = ret @ w_ret_o
    mixed = (jax.nn.sigmoid(ga) * y_a + jax.nn.sigmoid(gb) * y_b) @ w_out
    h = layer_norm(DEEPNORM_ALPHA * x + mixed, ln1_g, ln1_b)
    h2d = h.reshape(B * L, D)
    eidx, ew = route(h2d, w_router, router_bias)
    ffn = routed_experts(h2d, eidx, ew, w_exp_gate, w_exp_up, w_exp_down) + swiglu(h2d, w_sh_gate, w_sh_up, w_sh_down)
    return layer_norm(DEEPNORM_ALPHA * h + ffn.reshape(B, L, D), ln2_g, ln2_b)


def setup_inputs(seed: int = 0) -> dict:
    key = jax.random.key(seed)
    ks = jax.random.split(key, 28)
    f32 = jnp.float32

    def nrm(k, shape, scale):
        return jax.random.normal(k, shape, f32) * scale

    n_pages = PAST_LEN // PAGE_SIZE
    n_used = DEC_BATCH * n_pages
    n_phys = n_used + max(1, n_used // 4)
    page_table = jax.random.permutation(ks[0], n_phys)[:n_used].reshape(DEC_BATCH, n_pages).astype(jnp.int32)
    col_scale = jnp.concatenate([jnp.ones((sum(IN_SPLITS[:5]),), f32),
                                 jnp.full((IN_SPLITS[5],), DEEPNORM_BETA, f32),
                                 jnp.ones((sum(IN_SPLITS[6:]),), f32)])
    hv = MLA_HEADS * MLA_V
    rvw = RET_HEADS * RET_DV
    b = DEEPNORM_BETA
    return {
        'x_prompt': nrm(ks[1], (BATCH, SEQ, D_MODEL), 1.0),
        'x_sample': nrm(ks[2], (DEC_BATCH, DEC_SEQ, D_MODEL), 1.0),
        'cache_ckv': nrm(ks[3], (n_phys, PAGE_SIZE, MLA_KV_LORA), 1.0),
        'cache_krope': nrm(ks[4], (n_phys, PAGE_SIZE, MLA_ROPE), 1.0),
        'state_ret': nrm(ks[5], (DEC_BATCH, RET_HEADS, RET_DK, RET_DV), 0.5),
        'page_table': page_table,
        'w_in': nrm(ks[6], (D_MODEL, IN_WIDTH), D_MODEL ** -0.5) * col_scale[None, :],
        'q_norm_g': 1.0 + nrm(ks[7], (MLA_Q_LORA,), 0.02),
        'w_q_up': nrm(ks[8], (MLA_Q_LORA, MLA_HEADS * (MLA_NOPE + MLA_ROPE)), MLA_Q_LORA ** -0.5),
        'kv_norm_g': 1.0 + nrm(ks[9], (MLA_KV_LORA,), 0.02),
        'w_uk': nrm(ks[10], (MLA_KV_LORA, MLA_HEADS * MLA_NOPE), MLA_KV_LORA ** -0.5),
        'w_uv': nrm(ks[11], (MLA_KV_LORA, hv), MLA_KV_LORA ** -0.5 * b),
        'ret_gn_g': 1.0 + nrm(ks[12], (rvw,), 0.02),
        'w_mla_o': nrm(ks[13], (hv, D_MODEL), hv ** -0.5 * b),
        'w_ret_o': nrm(ks[14], (rvw, D_MODEL), rvw ** -0.5 * b),
        'w_out': nrm(ks[15], (D_MODEL, D_MODEL), D_MODEL ** -0.5 * b),
        'ln1_g': 1.0 + nrm(ks[16], (D_MODEL,), 0.02),
        'ln1_b': nrm(ks[17], (D_MODEL,), 0.02),
        'w_router': nrm(ks[18], (D_MODEL, N_EXPERTS), D_MODEL ** -0.5),
        'router_bias': nrm(ks[19], (N_EXPERTS,), 0.01),
        'w_exp_gate': nrm(ks[20], (N_EXPERTS, D_MODEL, EXPERT_FF), D_MODEL ** -0.5 * b),
        'w_exp_up': nrm(ks[21], (N_EXPERTS, D_MODEL, EXPERT_FF), D_MODEL ** -0.5 * b),
        'w_exp_down': nrm(ks[22], (N_EXPERTS, EXPERT_FF, D_MODEL), EXPERT_FF ** -0.5 * b),
        'w_sh_gate': nrm(ks[23], (D_MODEL, SHARED_FF), D_MODEL ** -0.5 * b),
        'w_sh_up': nrm(ks[24], (D_MODEL, SHARED_FF), D_MODEL ** -0.5 * b),
        'w_sh_down': nrm(ks[25], (SHARED_FF, D_MODEL), SHARED_FF ** -0.5 * b),
        'ln2_g': 1.0 + nrm(ks[26], (D_MODEL,), 0.02),
        'ln2_b': nrm(ks[27], (D_MODEL,), 0.02),
    }


def reference(x_prompt, x_sample, cache_ckv, cache_krope, state_ret, page_table,
              w_in, q_norm_g, w_q_up, kv_norm_g, w_uk, w_uv, ret_gn_g, w_mla_o, w_ret_o, w_out,
              ln1_g, ln1_b, w_router, router_bias, w_exp_gate, w_exp_up, w_exp_down,
              w_sh_gate, w_sh_up, w_sh_down, ln2_g, ln2_b):
    y_p, y_s = x_prompt, x_sample
    for _ in range(DEPTH):
        B, S = y_p.shape[:2]
        pos_p = jnp.arange(S)
        qn, qr, ckv_p, kr_p, rq, rk, rv, rg, ga, gb = mixer_inputs(y_p, pos_p, w_in, q_norm_g, w_q_up, kv_norm_g)
        k_nope = (ckv_p @ w_uk).reshape(B, S, MLA_HEADS, MLA_NOPE)
        v = (ckv_p @ w_uv).reshape(B, S, MLA_HEADS, MLA_V)
        o_mla = mla_prompt(qn, qr, k_nope, kr_p, v)
        s0 = jnp.zeros((B, RET_HEADS, RET_DK, RET_DV), y_p.dtype)
        o_ret, ret_p = retention_chunked(rq, rk, rv, s0)
        y_p_new = merge_and_channel_mix(y_p, o_mla, o_ret, rg, ga, gb, ret_gn_g, w_mla_o, w_ret_o, w_out,
                                        ln1_g, ln1_b, w_router, router_bias, w_exp_gate, w_exp_up, w_exp_down,
                                        w_sh_gate, w_sh_up, w_sh_down, ln2_g, ln2_b)
        DB, Q = y_s.shape[:2]
        pos_s = PAST_LEN + jnp.arange(Q)
        qn_s, qr_s, ckv_s, kr_s, rq_s, rk_s, rv_s, rg_s, ga_s, gb_s = mixer_inputs(y_s, pos_s, w_in, q_norm_g, w_q_up, kv_norm_g)
        past_ckv = cache_ckv[page_table].reshape(DB, -1, MLA_KV_LORA)
        past_kr = cache_krope[page_table].reshape(DB, -1, MLA_ROPE)
        o_mla_s = mla_sample(qn_s, qr_s, ckv_s, kr_s, past_ckv, past_kr, w_uk, w_uv)
        o_ret_s, ret_s = retention_chunked(rq_s, rk_s, rv_s, state_ret)
        y_s_new = merge_and_channel_mix(y_s, o_mla_s, o_ret_s, rg_s, ga_s, gb_s, ret_gn_g, w_mla_o, w_ret_o, w_out,
                                        ln1_g, ln1_b, w_router, router_bias, w_exp_gate, w_exp_up, w_exp_down,
                                        w_sh_gate, w_sh_up, w_sh_down, ln2_g, ln2_b)
        y_p, y_s = y_p_new, y_s_new
    return (y_p, y_s, ckv_p, kr_p, ret_p, ckv_s, kr_s, ret_s)
```

```python
import functools
import math

import numpy as np
import jax
import jax.numpy as jnp
from jax import lax
from jax.experimental import pallas as pl
from jax.experimental.pallas import tpu as pltpu

F32 = jnp.float32
BF16 = jnp.bfloat16

MLA_HEADS = 8
MLA_NOPE = 64
MLA_ROPE = 32
MLA_V = 64
MLA_Q_LORA = 256
MLA_KV_LORA = 128
MLA_SCALE = (MLA_NOPE + MLA_ROPE) ** -0.5
RET_HEADS = 4
RET_DK = 128
RET_DV = 128
RET_CHUNK = 128
N_EXPERTS = 64
N_GROUPS = 8
TOPK_GROUPS = 4
TOP_K = 6
EXPERT_FF = 256
SHARED_FF = 256
ROUTED_SCALE = 2.5
MOE_BLOCK = 128
ROPE_BASE = 10000.0
LN_EPS = 1e-5
RMS_EPS = 1e-6
DEPTH = 1
DEEPNORM_ALPHA = (2 * DEPTH) ** 0.25

LANES = 128
HALF_ROPE = MLA_ROPE // 2
VMEM_LIMIT = 56 * 1024 * 1024
NEG_INF = float("-inf")


def _cparams(sem):
    return pltpu.CompilerParams(dimension_semantics=sem, vmem_limit_bytes=VMEM_LIMIT)


def _dot(a, b):
    return jnp.dot(a, b, preferred_element_type=F32)


def _dot_nt(a, b):
    return lax.dot_general(a, b, (((1,), (1,)), ((), ())), preferred_element_type=F32)


def _dot_tn(a, b):
    return lax.dot_general(a, b, (((0,), (0,)), ((), ())), preferred_element_type=F32)


N_TAB = 8


def _rope_tables(pos):
    L = pos.shape[0]
    posf = pos.astype(F32)[:, None]
    half_r = RET_DK // 2
    ang_r = posf * (ROPE_BASE ** (-jnp.arange(half_r, dtype=F32) / half_r))[None, :]
    cos_r = jnp.concatenate([jnp.cos(ang_r), jnp.cos(ang_r)], axis=1)
    sin_r = jnp.concatenate([-jnp.sin(ang_r), jnp.sin(ang_r)], axis=1)
    ang_m = posf * (ROPE_BASE ** (-jnp.arange(HALF_ROPE, dtype=F32) / HALF_ROPE))[None, :]
    c, s = jnp.cos(ang_m), jnp.sin(ang_m)
    z16 = jnp.zeros((L, HALF_ROPE), F32)

    def place(parts, offset):
        body = jnp.concatenate(parts, axis=1)
        return jnp.concatenate([jnp.zeros((L, offset), F32), body,
                                jnp.zeros((L, LANES - offset - body.shape[1]), F32)], axis=1)

    cos_k = place([c, c], 0)
    sinp_k = place([z16, s], 0)
    sinm_k = place([-s, z16], 0)
    ones = jnp.ones((L, MLA_NOPE), F32)
    cos_q = jnp.concatenate([ones, c, c, jnp.zeros((L, LANES - MLA_NOPE - MLA_ROPE), F32)], axis=1) * MLA_SCALE
    sinp_q = place([z16, s], MLA_NOPE) * MLA_SCALE
    sinm_q = place([-s, z16], MLA_NOPE) * MLA_SCALE
    return jnp.concatenate([cos_r, sin_r, cos_k, sinp_k, sinm_k, cos_q, sinp_q, sinm_q], axis=1)


def _prep_weights(w_in, w_q_up, w_uk, w_uv, w_mla_o):
    d = w_in.shape[0]
    c_q, c_kv, c_kr = MLA_Q_LORA, MLA_KV_LORA, MLA_ROPE
    o_ret = c_q + c_kv + c_kr
    n_ret = 2 * RET_HEADS * RET_DK + 2 * RET_HEADS * RET_DV
    w_small = jnp.concatenate([w_in[:, :o_ret], jnp.zeros((d, LANES - c_kr), F32)], axis=1).astype(BF16)
    w_ret = w_in[:, o_ret:o_ret + n_ret].astype(BF16)
    w_gate = w_in[:, o_ret + n_ret:].astype(BF16)
    hd = MLA_NOPE + MLA_ROPE
    w_q = jnp.pad(w_q_up.reshape(c_q, MLA_HEADS, hd), ((0, 0), (0, 0), (0, LANES - hd)))
    w_q = w_q.reshape(c_q, MLA_HEADS * LANES).astype(BF16)
    wk_top = jnp.pad(w_uk.reshape(c_kv, MLA_HEADS, MLA_NOPE), ((0, 0), (0, 0), (0, LANES - MLA_NOPE)))
    place = jnp.zeros((LANES, MLA_HEADS, LANES), F32)
    idx = jnp.arange(MLA_ROPE)
    place = place.at[idx, :, MLA_NOPE + idx].set(1.0)
    w_k = jnp.concatenate([wk_top.reshape(c_kv, -1), place.reshape(LANES, -1)], axis=0).astype(BF16)
    w_v = jnp.pad(w_uv.reshape(c_kv, MLA_HEADS, MLA_V), ((0, 0), (0, 0), (0, LANES - MLA_V)))
    w_v = w_v.reshape(c_kv, MLA_HEADS * LANES).astype(BF16)
    w_o = jnp.pad(w_mla_o.reshape(MLA_HEADS, MLA_V, -1), ((0, 0), (0, LANES - MLA_V), (0, 0)))
    w_o = w_o.reshape(MLA_HEADS * LANES, -1).astype(BF16)
    return w_small, w_ret, w_gate, w_q, w_k, w_v, w_o


def _rms_norm(x, g):
    inv = lax.rsqrt(jnp.mean(x * x, axis=-1, keepdims=True) + RMS_EPS)
    return x * inv * g


def _layer_norm(x, g, b):
    mu = jnp.mean(x, axis=-1, keepdims=True)
    xc = x - mu
    var = jnp.mean(xc * xc, axis=-1, keepdims=True)
    return xc * lax.rsqrt(var + LN_EPS) * g + b


def _sigmoid(x):
    return 1.0 / (1.0 + jnp.exp(-x))


def _proj_kernel(x_ref, tab_ref, wsm_ref, wret_ref, wg_ref, qg_ref, wq_ref, kvg_ref, wk_ref, wv_ref,
                 qh_ref, ckv_ref, kr_ref, krp_ref, rq_ref, rk_ref, rv_ref, rg_ref, ga_ref, gb_ref,
                 kh_ref, vh_ref):
    xb = x_ref[...].astype(BF16)

    def tab(i):
        return tab_ref[:, i * LANES:(i + 1) * LANES]

    small = _dot(xb, wsm_ref[...])
    cq = small[:, :MLA_Q_LORA]
    ckv = small[:, MLA_Q_LORA:MLA_Q_LORA + MLA_KV_LORA]
    krb = small[:, MLA_Q_LORA + MLA_KV_LORA:]
    q = _dot(_rms_norm(cq, qg_ref[...]).astype(BF16), wq_ref[...])
    cos_q, sinp_q, sinm_q = tab(5), tab(6), tab(7)
    for h in range(MLA_HEADS):
        blk = q[:, h * LANES:(h + 1) * LANES]
        rot = (blk * cos_q + pltpu.roll(blk, HALF_ROPE, 1) * sinp_q
               + pltpu.roll(blk, LANES - HALF_ROPE, 1) * sinm_q)
        qh_ref[:, h * LANES:(h + 1) * LANES] = rot.astype(BF16)
    ckvn = _rms_norm(ckv, kvg_ref[...])
    ckv_ref[...] = ckvn
    krr = (krb * tab(2) + pltpu.roll(krb, HALF_ROPE, 1) * tab(3)
           + pltpu.roll(krb, LANES - HALF_ROPE, 1) * tab(4))
    kr_ref[...] = krr[:, :MLA_ROPE]
    krp_ref[...] = krr
    kcat = jnp.concatenate([ckvn, krr], axis=1).astype(BF16)
    kh_ref[...] = _dot(kcat, wk_ref[...]).astype(BF16)
    vh_ref[...] = _dot(kcat[:, :MLA_KV_LORA], wv_ref[...]).astype(BF16)

    r = _dot(xb, wret_ref[...])
    cos_r, sin_r = tab(0), tab(1)
    nq = RET_HEADS * RET_DK
    for h in range(RET_HEADS):
        sl = slice(h * RET_DK, (h + 1) * RET_DK)
        a = r[:, sl]
        rq_ref[:, sl] = (a * cos_r + pltpu.roll(a, RET_DK // 2, 1) * sin_r).astype(rq_ref.dtype)
        b = r[:, nq + h * RET_DK:nq + (h + 1) * RET_DK]
        rk_ref[:, sl] = ((b * cos_r + pltpu.roll(b, RET_DK // 2, 1) * sin_r)
                         * (RET_DK ** -0.5)).astype(rk_ref.dtype)
    rv_ref[...] = r[:, 2 * nq:2 * nq + RET_HEADS * RET_DV].astype(rv_ref.dtype)
    rg = r[:, 2 * nq + RET_HEADS * RET_DV:]
    rg_ref[...] = (rg * _sigmoid(rg)).astype(rg_ref.dtype)

    g = _dot(xb, wg_ref[...])
    d = ga_ref.shape[1]
    ga_ref[...] = _sigmoid(g[:, :d]).astype(ga_ref.dtype)
    gb_ref[...] = _sigmoid(g[:, d:]).astype(gb_ref.dtype)


def _proj(x2d, tab, weights, q_norm_g, kv_norm_g, *, tm, ret_dtype):
    T, D = x2d.shape
    w_small, w_ret, w_gate, w_q, w_k, w_v = weights
    n_tab = tab.shape[0] // tm
    nr = RET_HEADS * RET_DK
    hp = MLA_HEADS * LANES

    def row(i):
        return (i, 0)

    def const(i):
        return (0, 0)

    def full(a):
        return pl.BlockSpec(a.shape, const)

    out_shapes = [
        jax.ShapeDtypeStruct((T, hp), BF16),
        jax.ShapeDtypeStruct((T, MLA_KV_LORA), F32),
        jax.ShapeDtypeStruct((T, MLA_ROPE), F32),
        jax.ShapeDtypeStruct((T, LANES), F32),
        jax.ShapeDtypeStruct((T, nr), ret_dtype),
        jax.ShapeDtypeStruct((T, nr), ret_dtype),
        jax.ShapeDtypeStruct((T, nr), ret_dtype),
        jax.ShapeDtypeStruct((T, nr), BF16),
        jax.ShapeDtypeStruct((T, D), BF16),
        jax.ShapeDtypeStruct((T, D), BF16),
        jax.ShapeDtypeStruct((T, hp), BF16),
        jax.ShapeDtypeStruct((T, hp), BF16),
    ]
    out_specs = [pl.BlockSpec((tm, s.shape[1]), row) for s in out_shapes]
    qg = q_norm_g.reshape(1, -1)
    kvg = kv_norm_g.reshape(1, -1)
    in_specs = [pl.BlockSpec((tm, D), row),
                pl.BlockSpec((tm, N_TAB * LANES), lambda i: (i % n_tab, 0)),
                full(w_small), full(w_ret), full(w_gate), full(qg), full(w_q), full(kvg), full(w_k), full(w_v)]
    return pl.pallas_call(
        _proj_kernel,
        grid=(T // tm,),
        in_specs=in_specs,
        out_specs=out_specs,
        out_shape=out_shapes,
        compiler_params=_cparams(("parallel",)),
        name="proj",
    )(x2d, tab, w_small, w_ret, w_gate, qg, w_q, kvg, w_k, w_v)


def _attn_p_kernel(q_ref, k_ref, v_ref, o_ref, m_ref, l_ref, acc_ref, *, tq, tk):
    i = pl.program_id(2)
    q = q_ref[...]
    m_ref[...] = jnp.full(m_ref.shape, NEG_INF, F32)
    l_ref[...] = jnp.zeros(l_ref.shape, F32)
    acc_ref[...] = jnp.zeros(acc_ref.shape, F32)

    def step(j, masked):
        k = k_ref[pl.ds(j * tk, tk), :]
        v = v_ref[pl.ds(j * tk, tk), :]
        s = _dot_nt(q, k)
        if masked:
            row = lax.broadcasted_iota(jnp.int32, (tq, tk), 0) + i * tq
            col = lax.broadcasted_iota(jnp.int32, (tq, tk), 1) + j * tk
            s = jnp.where(col <= row, s, NEG_INF)
        m_prev = m_ref[...]
        m_new = jnp.maximum(m_prev, jnp.max(s, axis=1, keepdims=True))
        alpha = jnp.exp(m_prev - m_new)
        p = jnp.exp(s - m_new)
        l_ref[...] = alpha * l_ref[...] + jnp.sum(p, axis=1, keepdims=True)
        acc_ref[...] = alpha * acc_ref[...] + _dot(p.astype(BF16), v)
        m_ref[...] = m_new

    n_full = (i * tq) // tk

    def body(j, c):
        step(j, False)
        return c

    lax.fori_loop(0, n_full, body, 0)
    for jj in range(tq // tk):
        step(n_full + jj, True)
    o_ref[...] = (acc_ref[...] / l_ref[...]).astype(o_ref.dtype)


def _attn_p(qh, kh, vh, B, S, *, tq, tk):
    nq = S // tq
    hp = MLA_HEADS * LANES
    kh3 = kh.reshape(B, S, hp)
    vh3 = vh.reshape(B, S, hp)
    return pl.pallas_call(
        functools.partial(_attn_p_kernel, tq=tq, tk=tk),
        grid=(B, MLA_HEADS, nq),
        in_specs=[pl.BlockSpec((tq, LANES), lambda b, h, i: (b * nq + i, h)),
                  pl.BlockSpec((None, S, LANES), lambda b, h, i: (b, 0, h)),
                  pl.BlockSpec((None, S, LANES), lambda b, h, i: (b, 0, h))],
        out_specs=pl.BlockSpec((tq, LANES), lambda b, h, i: (b * nq + i, h)),
        out_shape=jax.ShapeDtypeStruct((B * S, hp), BF16),
        scratch_shapes=[pltpu.VMEM((tq, 1), F32), pltpu.VMEM((tq, 1), F32), pltpu.VMEM((tq, LANES), F32)],
        compiler_params=_cparams(("parallel", "parallel", "arbitrary")),
        name="attn_p",
    )(qh, kh3, vh3)


def _ret_consts(C):
    lg = jnp.log(1.0 - 2.0 ** (-5.0 - jnp.arange(RET_HEADS, dtype=F32)))
    idx = jnp.arange(C, dtype=F32)
    diff = idx[:, None] - idx[None, :]
    dmask = jnp.where(diff >= 0, jnp.exp(jnp.maximum(diff, 0.0)[None] * lg[:, None, None]), 0.0)
    q_dec = jnp.exp((idx[None, :] + 1.0) * lg[:, None])[:, :, None]
    k_dec = jnp.exp((C - 1.0 - idx)[None, :] * lg[:, None])[:, :, None]
    s_dec = jnp.exp(C * lg)
    return dmask, q_dec, k_dec, s_dec


def _head_norm_gate(o, gate, gn):
    mu = jnp.mean(o, axis=-1, keepdims=True)
    oc = o - mu
    var = jnp.mean(oc * oc, axis=-1, keepdims=True)
    return gate * (oc * lax.rsqrt(var + LN_EPS) * gn)


def _ret_p_kernel(sdec_ref, q_ref, k_ref, v_ref, g_ref, dm_ref, qd_ref, kd_ref, gn_ref,
                  o_ref, s_ref, *, nb, cb):
    C = RET_CHUNK

    @pl.when(pl.program_id(0) == 0)
    def _():
        s_ref[...] = jnp.zeros(s_ref.shape, F32)

    for c in range(cb):
        rows = slice(c * C, (c + 1) * C)
        for b in range(nb):
            for h in range(RET_HEADS):
                cols = slice(h * RET_DK, (h + 1) * RET_DK)
                q = q_ref[b, rows, cols]
                k = k_ref[b, rows, cols]
                v = v_ref[b, rows, cols]
                state = s_ref[b, h]
                att = _dot_nt(q, k) * dm_ref[h]
                o = _dot(att.astype(BF16), v) + _dot(q, state.astype(BF16)) * qd_ref[h]
                kd = (k.astype(F32) * kd_ref[h]).astype(BF16)
                s_ref[b, h] = state * sdec_ref[h] + _dot_tn(kd, v)
                gate = g_ref[b, rows, cols].astype(F32)
                o_ref[b, rows, cols] = _head_norm_gate(o, gate, gn_ref[:, cols]).astype(o_ref.dtype)


def _ret_p(rq, rk, rv, rg, ret_gn_g, B, S, *, cb):
    C = RET_CHUNK
    nr = RET_HEADS * RET_DK
    dmask, q_dec, k_dec, s_dec = _ret_consts(C)
    blk = pl.BlockSpec((B, cb * C, nr), lambda g: (0, g, 0))

    def full(a):
        return pl.BlockSpec(a.shape, lambda g: (0,) * a.ndim)

    gn = ret_gn_g.reshape(1, nr)
    args = [a.reshape(B, S, nr) for a in (rq, rk, rv, rg)]
    ret, state = pl.pallas_call(
        functools.partial(_ret_p_kernel, nb=B, cb=cb),
        grid=(S // (cb * C),),
        in_specs=[pl.BlockSpec(memory_space=pltpu.SMEM), blk, blk, blk, blk,
                  full(dmask), full(q_dec), full(k_dec), full(gn)],
        out_specs=[blk, pl.BlockSpec((B, RET_HEADS, RET_DK, RET_DV), lambda g: (0, 0, 0, 0))],
        out_shape=[jax.ShapeDtypeStruct((B, S, nr), BF16),
                   jax.ShapeDtypeStruct((B, RET_HEADS, RET_DK, RET_DV), F32)],
        compiler_params=_cparams(("arbitrary",)),
        name="ret_p",
    )(s_dec, *args, dmask, q_dec, k_dec, gn)
    return ret.reshape(B * S, nr), state


def _ret_s_kernel(sdec_ref, q_ref, k_ref, v_ref, g_ref, s0_ref, dm_ref, qd_ref, kd_ref, gn_ref,
                  o_ref, s_ref, *, gb, q_len):
    rows = gb * q_len
    row_b = lax.broadcasted_iota(jnp.int32, (rows, RET_DV), 0) // q_len
    for h in range(RET_HEADS):
        cols = slice(h * RET_DK, (h + 1) * RET_DK)
        q = q_ref[:, cols].astype(BF16)
        k = k_ref[:, cols]
        v = v_ref[:, cols].astype(BF16)
        att = _dot_nt(q, k.astype(BF16)) * dm_ref[h]
        o = _dot(att.astype(BF16), v)
        kd = k * kd_ref[h]
        inter = jnp.zeros((rows, RET_DV), F32)
        for b in range(gb):
            state = s0_ref[b, h]
            inter = jnp.where(row_b == b, _dot(q, state.astype(BF16)), inter)
            kd_b = jnp.where(row_b == b, kd, 0.0).astype(BF16)
            s_ref[b, h] = state * sdec_ref[h] + _dot_tn(kd_b, v)
        o = o + inter * qd_ref[h]
        gate = g_ref[:, cols].astype(F32)
        o_ref[:, cols] = _head_norm_gate(o, gate, gn_ref[:, cols]).astype(o_ref.dtype)


def _ret_s(rq, rk, rv, rg, state, ret_gn_g, *, gb):
    DB = state.shape[0]
    q_len = rq.shape[0] // DB
    nr = RET_HEADS * RET_DK
    rows = gb * q_len
    dmask, q_dec, k_dec, s_dec = _ret_consts(q_len)
    same = (jnp.arange(rows)[:, None] // q_len) == (jnp.arange(rows)[None, :] // q_len)
    dm = jnp.where(same[None], jnp.tile(dmask, (1, gb, gb)), 0.0)
    qd = jnp.tile(q_dec, (1, gb, 1))
    kd = jnp.tile(k_dec, (1, gb, 1))
    gn = ret_gn_g.reshape(1, nr)
    blk = pl.BlockSpec((rows, nr), lambda g: (g, 0))
    sblk = pl.BlockSpec((gb, RET_HEADS, RET_DK, RET_DV), lambda g: (g, 0, 0, 0))

    def full(a):
        return pl.BlockSpec(a.shape, lambda g: (0,) * a.ndim)

    return pl.pallas_call(
        functools.partial(_ret_s_kernel, gb=gb, q_len=q_len),
        grid=(DB // gb,),
        in_specs=[pl.BlockSpec(memory_space=pltpu.SMEM), blk, blk, blk, blk, sblk,
                  full(dm), full(qd), full(kd), full(gn)],
        out_specs=[blk, sblk],
        out_shape=[jax.ShapeDtypeStruct((DB * q_len, nr), BF16),
                   jax.ShapeDtypeStruct(state.shape, F32)],
        compiler_params=_cparams(("parallel",)),
        name="ret_s",
    )(s_dec, rq, rk, rv, rg, state, dm, qd, kd, gn)


SLAB = 16


def _attn_s_kernel(pt_ref, q_ref, cn_ref, kn_ref, wabs_ref, wsel_ref, wuv_ref, ckv_hbm, kr_hbm,
                   o_ref, ckv_buf, kr_buf, sem, o_scr, *, n_pages, page, q_len, tk):
    b = pl.program_id(0)
    nb = pl.num_programs(0)
    slot = b % 2
    per_slab = SLAB // q_len
    P = n_pages * page

    def page_copies(bi, s):
        out = []
        for p in range(n_pages):
            pg = pt_ref[bi, p]
            out.append(pltpu.make_async_copy(ckv_hbm.at[pg], ckv_buf.at[s, pl.ds(p * page, page), :], sem.at[0, s]))
            out.append(pltpu.make_async_copy(kr_hbm.at[pg], kr_buf.at[s, pl.ds(p * page, page), :], sem.at[1, s]))
        return out

    @pl.when(b == 0)
    def _():
        for cp in page_copies(0, 0):
            cp.start()

    @pl.when(b + 1 < nb)
    def _():
        for cp in page_copies(b + 1, 1 - slot):
            cp.start()

    ql, qr = [], []
    for h in range(MLA_HEADS):
        qh = q_ref[:, h * LANES:(h + 1) * LANES]
        ql.append(_dot(qh, wabs_ref[h]))
        qr.append(_dot(qh, wsel_ref[...]))
    ql = jnp.concatenate(ql, axis=0).astype(BF16)
    qr = jnp.concatenate(qr, axis=0).astype(BF16)
    R = MLA_HEADS * SLAB

    for cp in page_copies(b, slot):
        cp.wait()

    def chunk(j, carry):
        m, l, acc = carry
        kc = ckv_buf[slot, pl.ds(j * tk, tk), :].astype(BF16)
        kr = kr_buf[slot, pl.ds(j * tk, tk), :].astype(BF16)
        s = _dot_nt(ql, kc) + _dot_nt(qr[:, :MLA_ROPE], kr)
        m_new = jnp.maximum(m, jnp.max(s, axis=1, keepdims=True))
        alpha = jnp.exp(m - m_new)
        p = jnp.exp(s - m_new)
        l = alpha * l + jnp.sum(p, axis=1, keepdims=True)
        acc = alpha * acc + _dot(p.astype(BF16), kc)
        return m_new, l, acc

    init = (jnp.full((R, 1), NEG_INF, F32), jnp.zeros((R, 1), F32), jnp.zeros((R, LANES), F32))
    m, l, acc = lax.fori_loop(0, P // tk, chunk, init)

    kn = cn_ref[...].astype(BF16)
    s = _dot_nt(ql, kn) + _dot_nt(qr, kn_ref[...].astype(BF16))
    row_t = lax.broadcasted_iota(jnp.int32, (R, SLAB), 0) % SLAB
    col_t = lax.broadcasted_iota(jnp.int32, (R, SLAB), 1)
    mine = b % per_slab
    ok = (col_t // q_len == mine) & (col_t % q_len <= row_t % q_len)
    s = jnp.where(ok, s, NEG_INF)
    m_new = jnp.maximum(m, jnp.max(s, axis=1, keepdims=True))
    alpha = jnp.exp(m - m_new)
    p = jnp.exp(s - m_new)
    l = alpha * l + jnp.sum(p, axis=1, keepdims=True)
    acc = alpha * acc + _dot(p.astype(BF16), kn)
    o_lat = (acc / l).astype(BF16)

    @pl.when(mine == 0)
    def _():
        o_scr[...] = jnp.zeros(o_scr.shape, F32)

    sel = lax.broadcasted_iota(jnp.int32, (SLAB, LANES), 0) // q_len == mine
    for h in range(MLA_HEADS):
        cols = slice(h * LANES, (h + 1) * LANES)
        o_h = _dot(o_lat[h * SLAB:(h + 1) * SLAB], wuv_ref[h])
        o_scr[:, cols] = jnp.where(sel, o_h, o_scr[:, cols])
    o_ref[...] = o_scr[...].astype(o_ref.dtype)


def _attn_s(page_table, qh, ckv_new, krp_new, cache_ckv, cache_krope, w_uk, w_uv, *, tk):
    DB, n_pages = page_table.shape
    page = cache_ckv.shape[1]
    q_len = qh.shape[0] // DB
    hp = MLA_HEADS * LANES
    per_slab = SLAB // q_len
    wabs = jnp.transpose(w_uk.reshape(MLA_KV_LORA, MLA_HEADS, MLA_NOPE), (1, 2, 0))
    wabs = jnp.pad(wabs, ((0, 0), (0, LANES - MLA_NOPE), (0, 0))).astype(BF16)
    idx = jnp.arange(MLA_ROPE)
    wsel = jnp.zeros((LANES, LANES), F32).at[MLA_NOPE + idx, idx].set(1.0).astype(BF16)
    wuv = jnp.pad(jnp.transpose(w_uv.reshape(MLA_KV_LORA, MLA_HEADS, MLA_V), (1, 0, 2)),
                  ((0, 0), (0, 0), (0, LANES - MLA_V))).astype(BF16)
    P = n_pages * page

    def slab(b, pt):
        return (b // per_slab, 0)

    grid_spec = pltpu.PrefetchScalarGridSpec(
        num_scalar_prefetch=1,
        grid=(DB,),
        in_specs=[pl.BlockSpec((SLAB, hp), slab),
                  pl.BlockSpec((SLAB, MLA_KV_LORA), slab),
                  pl.BlockSpec((SLAB, LANES), slab),
                  pl.BlockSpec(wabs.shape, lambda b, pt: (0, 0, 0)),
                  pl.BlockSpec(wsel.shape, lambda b, pt: (0, 0)),
                  pl.BlockSpec(wuv.shape, lambda b, pt: (0, 0, 0)),
                  pl.BlockSpec(memory_space=pl.ANY),
                  pl.BlockSpec(memory_space=pl.ANY)],
        out_specs=pl.BlockSpec((SLAB, hp), slab),
        scratch_shapes=[pltpu.VMEM((2, P, MLA_KV_LORA), F32),
                        pltpu.VMEM((2, P, MLA_ROPE), F32),
                        pltpu.SemaphoreType.DMA((2, 2)),
                        pltpu.VMEM((SLAB, hp), F32)],
    )
    return pl.pallas_call(
        functools.partial(_attn_s_kernel, n_pages=n_pages, page=page, q_len=q_len, tk=tk),
        grid_spec=grid_spec,
        out_shape=jax.ShapeDtypeStruct((DB * q_len, hp), BF16),
        compiler_params=_cparams(("arbitrary",)),
        name="attn_s",
    )(page_table, qh, ckv_new, krp_new, wabs, wsel, wuv, cache_ckv, cache_krope)


ROUTE_ROWS = 8


def _first_index(hit, idx, big):
    return jnp.min(jnp.where(hit, idx, big), axis=0, keepdims=True)


def _route(scores, sel):
    tm = scores.shape[1]
    gsz = N_EXPERTS // N_GROUPS
    sub = lax.broadcasted_iota(jnp.int32, (gsz, tm), 0)
    grp_rows = lax.broadcasted_iota(jnp.int32, (N_GROUPS, tm), 0)
    groups = [sel[g * gsz:(g + 1) * gsz, :] for g in range(N_GROUPS)]
    gscore = jnp.zeros((N_GROUPS, tm), F32)
    for g, x in enumerate(groups):
        m1 = jnp.max(x, axis=0, keepdims=True)
        first = _first_index(x == m1, sub, gsz)
        m2 = jnp.max(jnp.where(sub == first, NEG_INF, x), axis=0, keepdims=True)
        gscore = jnp.where(grp_rows == g, m1 + m2, gscore)
    chosen = jnp.zeros((N_GROUPS, tm), jnp.bool_)
    y = gscore
    for _ in range(TOPK_GROUPS):
        m = jnp.max(y, axis=0, keepdims=True)
        hit = grp_rows == _first_index(y == m, grp_rows, N_GROUPS)
        chosen = chosen | hit
        y = jnp.where(hit, NEG_INF, y)
    cand = [jnp.where(chosen[g:g + 1, :], x, NEG_INF) for g, x in enumerate(groups)]
    eids = [sub + g * gsz for g in range(N_GROUPS)]
    out_rows = lax.broadcasted_iota(jnp.int32, (ROUTE_ROWS, tm), 0)
    eidx = jnp.zeros((ROUTE_ROWS, tm), jnp.int32)
    wsel = jnp.zeros((ROUTE_ROWS, tm), F32)
    for k in range(TOP_K):
        m = functools.reduce(jnp.maximum, [jnp.max(c, axis=0, keepdims=True) for c in cand])
        first = functools.reduce(jnp.minimum, [_first_index(c == m, e, N_EXPERTS) for c, e in zip(cand, eids)])
        wk = jnp.zeros((1, tm), F32)
        for g in range(N_GROUPS):
            hit = eids[g] == first
            wk = wk + jnp.sum(jnp.where(hit, scores[g * gsz:(g + 1) * gsz, :], 0.0), axis=0, keepdims=True)
            cand[g] = jnp.where(hit, NEG_INF, cand[g])
        eidx = jnp.where(out_rows == k, first, eidx)
        wsel = jnp.where(out_rows == k, wk, wsel)
    total = jnp.sum(wsel, axis=0, keepdims=True)
    return eidx, wsel / total * ROUTED_SCALE


def _split_hi_lo(a):
    hi = a.astype(BF16)
    lo = (a - hi.astype(F32)).astype(BF16)
    return hi, lo


def _mix_kernel(x_ref, om_ref, ret_ref, ga_ref, gb_ref, wo_ref, wr_ref, wout_ref, g1_ref, b1_ref,
                wrt_hi_ref, wrt_lo_ref, rb_ref, h_ref, eidx_ref, gw_ref):
    y_a = _dot(om_ref[...], wo_ref[...])
    y_b = _dot(ret_ref[...], wr_ref[...])
    mixed_in = ga_ref[...].astype(F32) * y_a + gb_ref[...].astype(F32) * y_b
    mixed = _dot(mixed_in.astype(BF16), wout_ref[...])
    h = _layer_norm(DEEPNORM_ALPHA * x_ref[...] + mixed, g1_ref[...], b1_ref[...])
    h_ref[...] = h
    h_hi, h_lo = _split_hi_lo(h)
    logits = _dot_nt(wrt_hi_ref[...], h_hi) + (_dot_nt(wrt_hi_ref[...], h_lo) + _dot_nt(wrt_lo_ref[...], h_hi))
    scores = _sigmoid(logits)
    eidx, gw = _route(scores, scores + rb_ref[...])
    eidx_ref[...] = eidx
    gw_ref[...] = gw


def _mix(x2d, o_mla, ret, ga, gb, w_o, w_ret_o, w_out, ln1_g, ln1_b, w_router, router_bias, *, tm):
    T, D = x2d.shape
    wrt = w_router.T
    wrt_hi, wrt_lo = _split_hi_lo(wrt)
    rb = router_bias.reshape(N_EXPERTS, 1).astype(F32)
    g1, b1 = ln1_g.reshape(1, D), ln1_b.reshape(1, D)
    wr = w_ret_o.astype(BF16)
    wout = w_out.astype(BF16)

    def row(i):
        return (i, 0)

    def full(a):
        return pl.BlockSpec(a.shape, lambda i: (0,) * a.ndim)

    return pl.pallas_call(
        _mix_kernel,
        grid=(T // tm,),
        in_specs=[pl.BlockSpec((tm, D), row), pl.BlockSpec((tm, o_mla.shape[1]), row),
                  pl.BlockSpec((tm, ret.shape[1]), row), pl.BlockSpec((tm, D), row), pl.BlockSpec((tm, D), row),
                  full(w_o), full(wr), full(wout), full(g1), full(b1), full(wrt_hi), full(wrt_lo), full(rb)],
        out_specs=[pl.BlockSpec((tm, D), row),
                   pl.BlockSpec((ROUTE_ROWS, tm), lambda i: (0, i)),
                   pl.BlockSpec((ROUTE_ROWS, tm), lambda i: (0, i))],
        out_shape=[jax.ShapeDtypeStruct((T, D), F32),
                   jax.ShapeDtypeStruct((ROUTE_ROWS, T), jnp.int32),
                   jax.ShapeDtypeStruct((ROUTE_ROWS, T), F32)],
        compiler_params=_cparams(("parallel",)),
        name="mix",
    )(x2d, o_mla, ret, ga, gb, w_o, wr, wout, g1, b1, wrt_hi, wrt_lo, rb)


def _dispatch_plan(eidx, n_tokens):
    A = n_tokens * TOP_K
    flat_e = eidx.reshape(A)
    order = jnp.argsort(flat_e)
    counts = jnp.bincount(flat_e, length=N_EXPERTS)
    padded = (counts + MOE_BLOCK - 1) // MOE_BLOCK * MOE_BLOCK
    pad_end = jnp.cumsum(padded)
    pad_start = pad_end - padded
    start = jnp.cumsum(counts) - counts
    rank = jnp.zeros((A,), jnp.int32).at[order].set(jnp.arange(A, dtype=jnp.int32))
    dest = (pad_start[flat_e] + rank - start[flat_e]).astype(jnp.int32)
    n_blocks = (A + N_EXPERTS * (MOE_BLOCK - 1) + MOE_BLOCK - 1) // MOE_BLOCK
    row_tok = jnp.zeros((n_blocks * MOE_BLOCK,), jnp.int32).at[dest].set(jnp.arange(A, dtype=jnp.int32) // TOP_K)
    blk_e = jnp.minimum(jnp.searchsorted(pad_end, jnp.arange(n_blocks) * MOE_BLOCK, side="right"),
                        N_EXPERTS - 1).astype(jnp.int32)
    return dest.reshape(n_tokens, TOP_K), row_tok, blk_e


def _row_copy(src_hbm, row, dst, r, sem):
    return pltpu.make_async_copy(src_hbm.at[pl.ds(row, 1), :], dst.at[pl.ds(r, 1), :], sem)


def _experts_kernel(be_ref, tok_ref, h_hbm, wg_ref, wu_ref, wd_ref, y_ref, xbuf, sem):
    del be_ref

    def start(r, c):
        _row_copy(h_hbm, tok_ref[0, 0, r], xbuf, r, sem).start()
        return c

    def wait(r, c):
        _row_copy(h_hbm, 0, xbuf, r, sem).wait()
        return c

    lax.fori_loop(0, MOE_BLOCK, start, 0)
    lax.fori_loop(0, MOE_BLOCK, wait, 0)
    x = xbuf[...].astype(BF16)
    gate = _dot(x, wg_ref[...].astype(BF16))
    up = _dot(x, wu_ref[...].astype(BF16))
    hid = (gate * _sigmoid(gate) * up).astype(BF16)
    y_ref[...] = _dot(hid, wd_ref[...].astype(BF16))


def _experts(h_all, row_tok, blk_e, w_gate, w_up, w_down):
    T, D = h_all.shape
    n_blocks = blk_e.shape[0]
    tok3 = row_tok.reshape(n_blocks, 1, MOE_BLOCK)
    grid_spec = pltpu.PrefetchScalarGridSpec(
        num_scalar_prefetch=1,
        grid=(n_blocks,),
        in_specs=[pl.BlockSpec((1, 1, MOE_BLOCK), lambda j, be: (j, 0, 0), memory_space=pltpu.SMEM),
                  pl.BlockSpec(memory_space=pl.ANY),
                  pl.BlockSpec((None, D, EXPERT_FF), lambda j, be: (be[j], 0, 0)),
                  pl.BlockSpec((None, D, EXPERT_FF), lambda j, be: (be[j], 0, 0)),
                  pl.BlockSpec((None, EXPERT_FF, D), lambda j, be: (be[j], 0, 0))],
        out_specs=pl.BlockSpec((MOE_BLOCK, D), lambda j, be: (j, 0)),
        scratch_shapes=[pltpu.VMEM((MOE_BLOCK, D), F32), pltpu.SemaphoreType.DMA(())],
    )
    return pl.pallas_call(
        _experts_kernel,
        grid_spec=grid_spec,
        out_shape=jax.ShapeDtypeStruct((n_blocks * MOE_BLOCK, D), F32),
        compiler_params=_cparams(("arbitrary",)),
        name="experts",
    )(blk_e, tok3, h_all, w_gate, w_up, w_down)


def _combine_kernel(dest_ref, h_ref, gw_ref, y_hbm, wsg_ref, wsu_ref, wsd_ref, g2_ref, b2_ref,
                    o_ref, ybuf, sem, *, tm):
    n = TOP_K * tm

    def start(a, c):
        _row_copy(y_hbm, dest_ref[0, 0, a], ybuf, a, sem).start()
        return c

    def wait(a, c):
        _row_copy(y_hbm, 0, ybuf, a, sem).wait()
        return c

    lax.fori_loop(0, n, start, 0)
    h = h_ref[...]
    hb = h.astype(BF16)
    gate = _dot(hb, wsg_ref[...])
    up = _dot(hb, wsu_ref[...])
    ffn = _dot((gate * _sigmoid(gate) * up).astype(BF16), wsd_ref[...])
    lax.fori_loop(0, n, wait, 0)
    gw = gw_ref[...]
    for k in range(TOP_K):
        ffn = ffn + ybuf[pl.ds(k * tm, tm), :] * gw[:, k:k + 1]
    o_ref[...] = _layer_norm(DEEPNORM_ALPHA * h + ffn, g2_ref[...], b2_ref[...])


def _combine(h_all, dest, gw, ys, w_sh_gate, w_sh_up, w_sh_down, ln2_g, ln2_b, *, tm, tile0, n_tiles):
    T, D = h_all.shape
    dest3 = jnp.transpose(dest.reshape(T // tm, tm, TOP_K), (0, 2, 1)).reshape(T // tm, 1, TOP_K * tm)
    gwp = jnp.pad(gw, ((0, 0), (0, ROUTE_ROWS - TOP_K)))
    wsg, wsu, wsd = w_sh_gate.astype(BF16), w_sh_up.astype(BF16), w_sh_down.astype(BF16)
    g2, b2 = ln2_g.reshape(1, D), ln2_b.reshape(1, D)

    def full(a):
        return pl.BlockSpec(a.shape, lambda i: (0,) * a.ndim)

    return pl.pallas_call(
        functools.partial(_combine_kernel, tm=tm),
        grid=(n_tiles,),
        in_specs=[pl.BlockSpec((1, 1, TOP_K * tm), lambda i: (tile0 + i, 0, 0), memory_space=pltpu.SMEM),
                  pl.BlockSpec((tm, D), lambda i: (tile0 + i, 0)),
                  pl.BlockSpec((tm, ROUTE_ROWS), lambda i: (tile0 + i, 0)),
                  pl.BlockSpec(memory_space=pl.ANY),
                  full(wsg), full(wsu), full(wsd), full(g2), full(b2)],
        out_specs=pl.BlockSpec((tm, D), lambda i: (i, 0)),
        out_shape=jax.ShapeDtypeStruct((n_tiles * tm, D), F32),
        scratch_shapes=[pltpu.VMEM((TOP_K * tm, D), F32), pltpu.SemaphoreType.DMA(())],
        compiler_params=_cparams(("arbitrary",)),
        name="combine",
    )(dest3, h_all, gwp, ys, wsg, wsu, wsd, g2, b2)


def kernel(x_prompt, x_sample, cache_ckv, cache_krope, state_ret, page_table, w_in, q_norm_g, w_q_up, kv_norm_g,
           w_uk, w_uv, ret_gn_g, w_mla_o, w_ret_o, w_out, ln1_g, ln1_b, w_router, router_bias,
           w_exp_gate, w_exp_up, w_exp_down, w_sh_gate, w_sh_up, w_sh_down, ln2_g, ln2_b):
    B, S, D = x_prompt.shape
    DB, Q, _ = x_sample.shape
    Tp, Ts = B * S, DB * Q
    past_len = page_table.shape[1] * cache_ckv.shape[1]
    w_small, w_ret, w_gate, w_q, w_k, w_v, w_o = _prep_weights(w_in, w_q_up, w_uk, w_uv, w_mla_o)
    pw = (w_small, w_ret, w_gate, w_q, w_k, w_v)

    tab_p = _rope_tables(jnp.arange(S))
    (qh, ckv_p, kr_p, _, rq, rk, rv, rg, ga, gb, kh, vh) = _proj(
        x_prompt.reshape(Tp, D), tab_p, pw, q_norm_g, kv_norm_g, tm=512, ret_dtype=BF16)
    o_mla = _attn_p(qh, kh, vh, B, S, tq=512, tk=512)
    ret, ret_state_p = _ret_p(rq, rk, rv, rg, ret_gn_g, B, S, cb=4)
    h_p, eidx_p, gw_p = _mix(x_prompt.reshape(Tp, D), o_mla, ret, ga, gb, w_o, w_ret_o, w_out,
                             ln1_g, ln1_b, w_router, router_bias, tm=256)

    tab_s = _rope_tables(jnp.tile(past_len + jnp.arange(Q), DB))
    (qh_s, ckv_s, kr_s, krp_s, rq_s, rk_s, rv_s, rg_s, ga_s, gb_s, _, _) = _proj(
        x_sample.reshape(Ts, D), tab_s, pw, q_norm_g, kv_norm_g, tm=Ts, ret_dtype=F32)
    o_mla_s = _attn_s(page_table, qh_s, ckv_s, krp_s, cache_ckv, cache_krope, w_uk, w_uv, tk=min(1024, past_len))
    ret_s, ret_state_s = _ret_s(rq_s, rk_s, rv_s, rg_s, state_ret, ret_gn_g, gb=min(16, DB))
    h_s, eidx_s, gw_s = _mix(x_sample.reshape(Ts, D), o_mla_s, ret_s, ga_s, gb_s, w_o, w_ret_o, w_out,
                             ln1_g, ln1_b, w_router, router_bias, tm=min(256, Ts))

    T = Tp + Ts
    h_all = jnp.concatenate([h_p, h_s], axis=0)
    eidx = jnp.concatenate([eidx_p[:TOP_K], eidx_s[:TOP_K]], axis=1).T
    gw = jnp.concatenate([gw_p[:TOP_K], gw_s[:TOP_K]], axis=1).T
    dest, row_tok, blk_e = _dispatch_plan(eidx, T)
    ys = _experts(h_all, row_tok, blk_e, w_exp_gate, w_exp_up, w_exp_down)
    tmc = 128
    shared = (w_sh_gate, w_sh_up, w_sh_down, ln2_g, ln2_b)
    y_p = _combine(h_all, dest, gw, ys, *shared, tm=tmc, tile0=0, n_tiles=Tp // tmc)
    y_s = _combine(h_all, dest, gw, ys, *shared, tm=tmc, tile0=Tp // tmc, n_tiles=Ts // tmc)

    return (y_p.reshape(B, S, D), y_s.reshape(DB, Q, D),
            ckv_p.reshape(B, S, -1), kr_p.reshape(B, S, -1), ret_state_p,
            ckv_s.reshape(DB, Q, -1), kr_s.reshape(DB, Q, -1), ret_state_s)
```

```python
import functools
import math

import numpy as np
import jax
import jax.numpy as jnp
from jax import lax
from jax.experimental import pallas as pl
from jax.experimental.pallas import tpu as pltpu

F32 = jnp.float32
BF16 = jnp.bfloat16

MLA_HEADS = 8
MLA_NOPE = 64
MLA_ROPE = 32
MLA_V = 64
MLA_Q_LORA = 256
MLA_KV_LORA = 128
MLA_SCALE = (MLA_NOPE + MLA_ROPE) ** -0.5
Q_SCALE = MLA_SCALE * math.log2(math.e)
RET_HEADS = 4
RET_DK = 128
RET_DV = 128
RET_CHUNK = 128
N_EXPERTS = 64
N_GROUPS = 8
TOPK_GROUPS = 4
TOP_K = 6
EXPERT_FF = 256
SHARED_FF = 256
ROUTED_SCALE = 2.5
MOE_BLOCK = 128
ROPE_BASE = 10000.0
LN_EPS = 1e-5
RMS_EPS = 1e-6
DEPTH = 1
DEEPNORM_ALPHA = (2 * DEPTH) ** 0.25

LANES = 128
ROW_TILE = 8
HALF_ROPE = MLA_ROPE // 2
VMEM_LIMIT = 56 * 1024 * 1024
NEG_INF = float("-inf")


def _cparams(sem):
    return pltpu.CompilerParams(dimension_semantics=sem, vmem_limit_bytes=VMEM_LIMIT)


def _dot(a, b):
    return jnp.dot(a, b, preferred_element_type=F32)


def _dot_nt(a, b):
    return lax.dot_general(a, b, (((1,), (1,)), ((), ())), preferred_element_type=F32)


def _dot_tn(a, b):
    return lax.dot_general(a, b, (((0,), (0,)), ((), ())), preferred_element_type=F32)


N_TAB = 8


def _rope_tables(pos):
    L = pos.shape[0]
    posf = pos.astype(F32)[:, None]
    half_r = RET_DK // 2
    ang_r = posf * (ROPE_BASE ** (-jnp.arange(half_r, dtype=F32) / half_r))[None, :]
    cos_r = jnp.concatenate([jnp.cos(ang_r), jnp.cos(ang_r)], axis=1)
    sin_r = jnp.concatenate([-jnp.sin(ang_r), jnp.sin(ang_r)], axis=1)
    ang_m = posf * (ROPE_BASE ** (-jnp.arange(HALF_ROPE, dtype=F32) / HALF_ROPE))[None, :]
    c, s = jnp.cos(ang_m), jnp.sin(ang_m)
    z16 = jnp.zeros((L, HALF_ROPE), F32)

    def place(parts, offset):
        body = jnp.concatenate(parts, axis=1)
        return jnp.concatenate([jnp.zeros((L, offset), F32), body,
                                jnp.zeros((L, LANES - offset - body.shape[1]), F32)], axis=1)

    cos_k = place([c, c], 0)
    sinp_k = place([z16, s], 0)
    sinm_k = place([-s, z16], 0)
    ones = jnp.ones((L, MLA_NOPE), F32)
    cos_q = jnp.concatenate([ones, c, c, jnp.zeros((L, LANES - MLA_NOPE - MLA_ROPE), F32)], axis=1) * Q_SCALE
    sinp_q = place([z16, s], MLA_NOPE) * Q_SCALE
    sinm_q = place([-s, z16], MLA_NOPE) * Q_SCALE
    return jnp.concatenate([cos_r, sin_r, cos_k, sinp_k, sinm_k, cos_q, sinp_q, sinm_q], axis=1)


def _prep_weights(w_in, w_q_up, w_uk, w_uv, w_mla_o):
    d = w_in.shape[0]
    c_q, c_kv, c_kr = MLA_Q_LORA, MLA_KV_LORA, MLA_ROPE
    o_ret = c_q + c_kv + c_kr
    n_ret = 2 * RET_HEADS * RET_DK + 2 * RET_HEADS * RET_DV
    w_small = jnp.concatenate([w_in[:, :o_ret], jnp.zeros((d, LANES - c_kr), F32)], axis=1).astype(BF16)
    w_ret = w_in[:, o_ret:o_ret + n_ret].astype(BF16)
    w_gate = w_in[:, o_ret + n_ret:].astype(BF16)
    hd = MLA_NOPE + MLA_ROPE
    w_q = jnp.pad(w_q_up.reshape(c_q, MLA_HEADS, hd), ((0, 0), (0, 0), (0, LANES - hd)))
    w_q = w_q.reshape(c_q, MLA_HEADS * LANES).astype(BF16)
    wk_top = jnp.pad(w_uk.reshape(c_kv, MLA_HEADS, MLA_NOPE), ((0, 0), (0, 0), (0, LANES - MLA_NOPE)))
    place = jnp.zeros((LANES, MLA_HEADS, LANES), F32)
    idx = jnp.arange(MLA_ROPE)
    place = place.at[idx, :, MLA_NOPE + idx].set(1.0)
    w_k = jnp.concatenate([wk_top.reshape(c_kv, -1), place.reshape(LANES, -1)], axis=0).astype(BF16)
    w_v = jnp.pad(w_uv.reshape(c_kv, MLA_HEADS, MLA_V), ((0, 0), (0, 0), (0, LANES - MLA_V)))
    w_v = w_v.reshape(c_kv, MLA_HEADS * LANES).astype(BF16)
    w_o = jnp.pad(w_mla_o.reshape(MLA_HEADS, MLA_V, -1), ((0, 0), (0, LANES - MLA_V), (0, 0)))
    w_o = w_o.reshape(MLA_HEADS * LANES, -1).astype(BF16)
    return w_small, w_ret, w_gate, w_q, w_k, w_v, w_o


def _rms_norm(x, g):
    inv = lax.rsqrt(jnp.mean(x * x, axis=-1, keepdims=True) + RMS_EPS)
    return x * inv * g


def _layer_norm(x, g, b):
    mu = jnp.mean(x, axis=-1, keepdims=True)
    xc = x - mu
    var = jnp.mean(xc * xc, axis=-1, keepdims=True)
    return xc * lax.rsqrt(var + LN_EPS) * g + b


def _sigmoid(x):
    return 1.0 / (1.0 + jnp.exp(-x))


def _proj_kernel(x_ref, tab_ref, wsm_ref, wret_ref, wg_ref, qg_ref, wq_ref, kvg_ref, wk_ref, wv_ref,
                 qh_ref, ckv_ref, kr_ref, krp_ref, rq_ref, rk_ref, rv_ref, rg_ref, ga_ref, gb_ref,
                 kh_ref, vh_ref):
    xb = x_ref[...].astype(BF16)

    def tab(i):
        return tab_ref[:, i * LANES:(i + 1) * LANES]

    small = _dot(xb, wsm_ref[...])
    cq = small[:, :MLA_Q_LORA]
    ckv = small[:, MLA_Q_LORA:MLA_Q_LORA + MLA_KV_LORA]
    krb = small[:, MLA_Q_LORA + MLA_KV_LORA:]
    q = _dot(_rms_norm(cq, qg_ref[...]).astype(BF16), wq_ref[...])
    cos_q, sinp_q, sinm_q = tab(5), tab(6), tab(7)
    for h in range(MLA_HEADS):
        blk = q[:, h * LANES:(h + 1) * LANES]
        rot = (blk * cos_q + pltpu.roll(blk, HALF_ROPE, 1) * sinp_q
               + pltpu.roll(blk, LANES - HALF_ROPE, 1) * sinm_q)
        qh_ref[:, h * LANES:(h + 1) * LANES] = rot.astype(BF16)
    ckvn = _rms_norm(ckv, kvg_ref[...])
    ckv_ref[...] = ckvn
    krr = (krb * tab(2) + pltpu.roll(krb, HALF_ROPE, 1) * tab(3)
           + pltpu.roll(krb, LANES - HALF_ROPE, 1) * tab(4))
    kr_ref[...] = krr[:, :MLA_ROPE]
    krp_ref[...] = krr
    kcat = jnp.concatenate([ckvn, krr], axis=1).astype(BF16)
    kh_ref[...] = _dot(kcat, wk_ref[...]).astype(BF16)
    vv = _dot(kcat[:, :MLA_KV_LORA], wv_ref[...])
    lane = lax.broadcasted_iota(jnp.int32, vv.shape, 1) % LANES
    vh_ref[...] = jnp.where(lane == MLA_V, 1.0, vv).astype(BF16)

    r = _dot(xb, wret_ref[...])
    cos_r, sin_r = tab(0), tab(1)
    nq = RET_HEADS * RET_DK
    for h in range(RET_HEADS):
        sl = slice(h * RET_DK, (h + 1) * RET_DK)
        a = r[:, sl]
        rq_ref[:, sl] = (a * cos_r + pltpu.roll(a, RET_DK // 2, 1) * sin_r).astype(rq_ref.dtype)
        b = r[:, nq + h * RET_DK:nq + (h + 1) * RET_DK]
        rk_ref[:, sl] = ((b * cos_r + pltpu.roll(b, RET_DK // 2, 1) * sin_r)
                         * (RET_DK ** -0.5)).astype(rk_ref.dtype)
    rv_ref[...] = r[:, 2 * nq:2 * nq + RET_HEADS * RET_DV].astype(rv_ref.dtype)
    rg = r[:, 2 * nq + RET_HEADS * RET_DV:]
    rg_ref[...] = (rg * _sigmoid(rg)).astype(rg_ref.dtype)

    g = _dot(xb, wg_ref[...])
    d = ga_ref.shape[1]
    ga_ref[...] = _sigmoid(g[:, :d]).astype(ga_ref.dtype)
    gb_ref[...] = _sigmoid(g[:, d:]).astype(gb_ref.dtype)


def _proj(x2d, tab, weights, q_norm_g, kv_norm_g, *, tm, ret_dtype):
    T, D = x2d.shape
    w_small, w_ret, w_gate, w_q, w_k, w_v = weights
    n_tab = tab.shape[0] // tm
    nr = RET_HEADS * RET_DK
    hp = MLA_HEADS * LANES

    def row(i):
        return (i, 0)

    def const(i):
        return (0, 0)

    def full(a):
        return pl.BlockSpec(a.shape, const)

    out_shapes = [
        jax.ShapeDtypeStruct((T, hp), BF16),
        jax.ShapeDtypeStruct((T, MLA_KV_LORA), F32),
        jax.ShapeDtypeStruct((T, MLA_ROPE), F32),
        jax.ShapeDtypeStruct((T, LANES), F32),
        jax.ShapeDtypeStruct((T, nr), ret_dtype),
        jax.ShapeDtypeStruct((T, nr), ret_dtype),
        jax.ShapeDtypeStruct((T, nr), ret_dtype),
        jax.ShapeDtypeStruct((T, nr), BF16),
        jax.ShapeDtypeStruct((T, D), BF16),
        jax.ShapeDtypeStruct((T, D), BF16),
        jax.ShapeDtypeStruct((T, hp), BF16),
        jax.ShapeDtypeStruct((T, hp), BF16),
    ]
    out_specs = [pl.BlockSpec((tm, s.shape[1]), row) for s in out_shapes]
    qg = q_norm_g.reshape(1, -1)
    kvg = kv_norm_g.reshape(1, -1)
    in_specs = [pl.BlockSpec((tm, D), row),
                pl.BlockSpec((tm, N_TAB * LANES), lambda i: (i % n_tab, 0)),
                full(w_small), full(w_ret), full(w_gate), full(qg), full(w_q), full(kvg), full(w_k), full(w_v)]
    return pl.pallas_call(
        _proj_kernel,
        grid=(T // tm,),
        in_specs=in_specs,
        out_specs=out_specs,
        out_shape=out_shapes,
        compiler_params=_cparams(("parallel",)),
        name="proj",
    )(x2d, tab, w_small, w_ret, w_gate, qg, w_q, kvg, w_k, w_v)


def _attn_p_kernel(q_ref, k_ref, v_ref, o_ref, m_ref, acc_ref, *, tq, tk, hps):
    i = pl.program_id(2)
    m_ref[...] = jnp.full(m_ref.shape, NEG_INF, F32)
    acc_ref[...] = jnp.zeros(acc_ref.shape, F32)

    def step(j, masked):
        for hh in range(hps):
            cols = slice(hh * LANES, (hh + 1) * LANES)
            k = k_ref[pl.ds(j * tk, tk), cols]
            v = v_ref[pl.ds(j * tk, tk), cols]
            s = _dot_nt(q_ref[:, cols], k)
            if masked:
                row = lax.broadcasted_iota(jnp.int32, (tq, tk), 0) + i * tq
                col = lax.broadcasted_iota(jnp.int32, (tq, tk), 1) + j * tk
                s = jnp.where(col <= row, s, NEG_INF)
            m_prev = m_ref[hh]
            m_new = jnp.maximum(m_prev, jnp.max(s, axis=1, keepdims=True))
            p = jnp.concatenate([jnp.exp2(s[:, c * LANES:(c + 1) * LANES] - m_new)
                                 for c in range(tk // LANES)], axis=1)
            acc_ref[hh] = jnp.exp2(m_prev - m_new) * acc_ref[hh] + _dot(p.astype(BF16), v)
            m_ref[hh] = m_new

    n_full = (i * tq) // tk

    def body(j, c):
        step(j, False)
        return c

    lax.fori_loop(0, n_full, body, 0)
    for jj in range(tq // tk):
        step(n_full + jj, True)
    for hh in range(hps):
        acc = acc_ref[hh]
        o_ref[:, hh * LANES:(hh + 1) * LANES] = (acc / acc[:, MLA_V:MLA_V + 1]).astype(o_ref.dtype)


def _attn_p(qh, kh, vh, B, S, *, tq, tk, hps):
    assert tq % tk == 0 and MLA_HEADS % hps == 0
    nq = S // tq
    hp = MLA_HEADS * LANES
    kh3 = kh.reshape(B, S, hp)
    vh3 = vh.reshape(B, S, hp)
    return pl.pallas_call(
        functools.partial(_attn_p_kernel, tq=tq, tk=tk, hps=hps),
        grid=(B, MLA_HEADS // hps, nq),
        in_specs=[pl.BlockSpec((tq, hps * LANES), lambda b, h, i: (b * nq + i, h)),
                  pl.BlockSpec((None, S, hps * LANES), lambda b, h, i: (b, 0, h)),
                  pl.BlockSpec((None, S, hps * LANES), lambda b, h, i: (b, 0, h))],
        out_specs=pl.BlockSpec((tq, hps * LANES), lambda b, h, i: (b * nq + i, h)),
        out_shape=jax.ShapeDtypeStruct((B * S, hp), BF16),
        scratch_shapes=[pltpu.VMEM((hps, tq, LANES), F32), pltpu.VMEM((hps, tq, LANES), F32)],
        compiler_params=_cparams(("parallel", "parallel", "arbitrary")),
        name="attn_p",
    )(qh, kh3, vh3)


def _ret_consts(C):
    lg = jnp.log(1.0 - 2.0 ** (-5.0 - jnp.arange(RET_HEADS, dtype=F32)))
    idx = jnp.arange(C, dtype=F32)
    diff = idx[:, None] - idx[None, :]
    dmask = jnp.where(diff >= 0, jnp.exp(jnp.maximum(diff, 0.0)[None] * lg[:, None, None]), 0.0)
    q_dec = jnp.exp((idx[None, :] + 1.0) * lg[:, None])[:, :, None]
    k_dec = jnp.exp((C - 1.0 - idx)[None, :] * lg[:, None])[:, :, None]
    s_dec = jnp.exp(C * lg)
    return dmask, q_dec, k_dec, s_dec


def _head_norm_gate(o, gate, gn):
    mu = jnp.mean(o, axis=-1, keepdims=True)
    oc = o - mu
    var = jnp.mean(oc * oc, axis=-1, keepdims=True)
    return gate * (oc * lax.rsqrt(var + LN_EPS) * gn)


def _ret_p_kernel(sdec_ref, q_ref, k_ref, v_ref, g_ref, dm_ref, qd_ref, kd_ref, gn_ref,
                  o_ref, s_ref, *, nb, cb):
    C = RET_CHUNK

    @pl.when(pl.program_id(0) == 0)
    def _():
        s_ref[...] = jnp.zeros(s_ref.shape, F32)

    for c in range(cb):
        rows = slice(c * C, (c + 1) * C)
        for b in range(nb):
            for h in range(RET_HEADS):
                cols = slice(h * RET_DK, (h + 1) * RET_DK)
                q = q_ref[b, rows, cols]
                k = k_ref[b, rows, cols]
                v = v_ref[b, rows, cols]
                state = s_ref[b, h]
                att = _dot_nt(q, k) * dm_ref[h]
                o = _dot(att.astype(BF16), v) + _dot(q, state.astype(BF16)) * qd_ref[h]
                kd = (k.astype(F32) * kd_ref[h]).astype(BF16)
                s_ref[b, h] = state * sdec_ref[h] + _dot_tn(kd, v)
                gate = g_ref[b, rows, cols].astype(F32)
                o_ref[b, rows, cols] = _head_norm_gate(o, gate, gn_ref[:, cols]).astype(o_ref.dtype)


def _ret_p(rq, rk, rv, rg, ret_gn_g, B, S, *, cb):
    C = RET_CHUNK
    nr = RET_HEADS * RET_DK
    dmask, q_dec, k_dec, s_dec = _ret_consts(C)
    blk = pl.BlockSpec((B, cb * C, nr), lambda g: (0, g, 0))

    def full(a):
        return pl.BlockSpec(a.shape, lambda g: (0,) * a.ndim)

    gn = ret_gn_g.reshape(1, nr)
    args = [a.reshape(B, S, nr) for a in (rq, rk, rv, rg)]
    ret, state = pl.pallas_call(
        functools.partial(_ret_p_kernel, nb=B, cb=cb),
        grid=(S // (cb * C),),
        in_specs=[pl.BlockSpec(memory_space=pltpu.SMEM), blk, blk, blk, blk,
                  full(dmask), full(q_dec), full(k_dec), full(gn)],
        out_specs=[blk, pl.BlockSpec((B, RET_HEADS, RET_DK, RET_DV), lambda g: (0, 0, 0, 0))],
        out_shape=[jax.ShapeDtypeStruct((B, S, nr), BF16),
                   jax.ShapeDtypeStruct((B, RET_HEADS, RET_DK, RET_DV), F32)],
        compiler_params=_cparams(("arbitrary",)),
        name="ret_p",
    )(s_dec, *args, dmask, q_dec, k_dec, gn)
    return ret.reshape(B * S, nr), state


def _ret_s_kernel(sdec_ref, q_ref, k_ref, v_ref, g_ref, s0_ref, dm_ref, qd_ref, kd_ref, gn_ref,
                  o_ref, s_ref, *, gb, q_len):
    rows = gb * q_len
    row_b = lax.broadcasted_iota(jnp.int32, (rows, RET_DV), 0) // q_len
    for h in range(RET_HEADS):
        cols = slice(h * RET_DK, (h + 1) * RET_DK)
        q = q_ref[:, cols].astype(BF16)
        k = k_ref[:, cols]
        v = v_ref[:, cols].astype(BF16)
        att = _dot_nt(q, k.astype(BF16)) * dm_ref[h]
        o = _dot(att.astype(BF16), v)
        kd = k * kd_ref[h]
        inter = jnp.zeros((rows, RET_DV), F32)
        for b in range(gb):
            state = s0_ref[b, h]
            inter = jnp.where(row_b == b, _dot(q, state.astype(BF16)), inter)
            kd_b = jnp.where(row_b == b, kd, 0.0).astype(BF16)
            s_ref[b, h] = state * sdec_ref[h] + _dot_tn(kd_b, v)
        o = o + inter * qd_ref[h]
        gate = g_ref[:, cols].astype(F32)
        o_ref[:, cols] = _head_norm_gate(o, gate, gn_ref[:, cols]).astype(o_ref.dtype)


def _ret_s(rq, rk, rv, rg, state, ret_gn_g, *, gb):
    DB = state.shape[0]
    q_len = rq.shape[0] // DB
    nr = RET_HEADS * RET_DK
    rows = gb * q_len
    dmask, q_dec, k_dec, s_dec = _ret_consts(q_len)
    same = (jnp.arange(rows)[:, None] // q_len) == (jnp.arange(rows)[None, :] // q_len)
    dm = jnp.where(same[None], jnp.tile(dmask, (1, gb, gb)), 0.0)
    qd = jnp.tile(q_dec, (1, gb, 1))
    kd = jnp.tile(k_dec, (1, gb, 1))
    gn = ret_gn_g.reshape(1, nr)
    blk = pl.BlockSpec((rows, nr), lambda g: (g, 0))
    sblk = pl.BlockSpec((gb, RET_HEADS, RET_DK, RET_DV), lambda g: (g, 0, 0, 0))

    def full(a):
        return pl.BlockSpec(a.shape, lambda g: (0,) * a.ndim)

    return pl.pallas_call(
        functools.partial(_ret_s_kernel, gb=gb, q_len=q_len),
        grid=(DB // gb,),
        in_specs=[pl.BlockSpec(memory_space=pltpu.SMEM), blk, blk, blk, blk, sblk,
                  full(dm), full(qd), full(kd), full(gn)],
        out_specs=[blk, sblk],
        out_shape=[jax.ShapeDtypeStruct((DB * q_len, nr), BF16),
                   jax.ShapeDtypeStruct(state.shape, F32)],
        compiler_params=_cparams(("parallel",)),
        name="ret_s",
    )(s_dec, rq, rk, rv, rg, state, dm, qd, kd, gn)


SLAB = 16


def _attn_s_kernel(pt_ref, q_ref, cn_ref, kn_ref, wabs_ref, wsel_ref, wuv_ref, ckv_hbm, kr_hbm,
                   o_ref, ckv_buf, kr_buf, sem, o_scr, *, n_pages, page, q_len, tk):
    b = pl.program_id(0)
    nb = pl.num_programs(0)
    slot = b % 2
    per_slab = SLAB // q_len
    P = n_pages * page

    def page_copies(bi, s):
        out = []
        for p in range(n_pages):
            pg = pt_ref[bi, p]
            out.append(pltpu.make_async_copy(ckv_hbm.at[pg], ckv_buf.at[s, pl.ds(p * page, page), :], sem.at[0, s]))
            out.append(pltpu.make_async_copy(kr_hbm.at[pg], kr_buf.at[s, pl.ds(p * page, page), :], sem.at[1, s]))
        return out

    @pl.when(b == 0)
    def _():
        for cp in page_copies(0, 0):
            cp.start()

    @pl.when(b + 1 < nb)
    def _():
        for cp in page_copies(b + 1, 1 - slot):
            cp.start()

    ql, qr = [], []
    for h in range(MLA_HEADS):
        qh = q_ref[:, h * LANES:(h + 1) * LANES]
        ql.append(_dot(qh, wabs_ref[h]))
        qr.append(_dot(qh, wsel_ref[...]))
    ql = jnp.concatenate(ql, axis=0).astype(BF16)
    qr = jnp.concatenate(qr, axis=0).astype(BF16)
    R = MLA_HEADS * SLAB

    for cp in page_copies(b, slot):
        cp.wait()

    def chunk(j, carry):
        m, l, acc = carry
        kc = ckv_buf[slot, pl.ds(j * tk, tk), :].astype(BF16)
        kr = kr_buf[slot, pl.ds(j * tk, tk), :].astype(BF16)
        s = _dot_nt(ql, kc) + _dot_nt(qr[:, :MLA_ROPE], kr)
        m_new = jnp.maximum(m, jnp.max(s, axis=1, keepdims=True))
        alpha = jnp.exp2(m - m_new)
        p = jnp.exp2(s - m_new)
        l = alpha * l + jnp.sum(p, axis=1, keepdims=True)
        acc = alpha * acc + _dot(p.astype(BF16), kc)
        return m_new, l, acc

    init = (jnp.full((R, 1), NEG_INF, F32), jnp.zeros((R, 1), F32), jnp.zeros((R, LANES), F32))
    m, l, acc = lax.fori_loop(0, P // tk, chunk, init)

    kn = cn_ref[...].astype(BF16)
    s = _dot_nt(ql, kn) + _dot_nt(qr, kn_ref[...].astype(BF16))
    row_t = lax.broadcasted_iota(jnp.int32, (R, SLAB), 0) % SLAB
    col_t = lax.broadcasted_iota(jnp.int32, (R, SLAB), 1)
    mine = b % per_slab
    ok = (col_t // q_len == mine) & (col_t % q_len <= row_t % q_len)
    s = jnp.where(ok, s, NEG_INF)
    m_new = jnp.maximum(m, jnp.max(s, axis=1, keepdims=True))
    alpha = jnp.exp2(m - m_new)
    p = jnp.exp2(s - m_new)
    l = alpha * l + jnp.sum(p, axis=1, keepdims=True)
    acc = alpha * acc + _dot(p.astype(BF16), kn)
    o_lat = (acc / l).astype(BF16)

    @pl.when(mine == 0)
    def _():
        o_scr[...] = jnp.zeros(o_scr.shape, F32)

    sel = lax.broadcasted_iota(jnp.int32, (SLAB, LANES), 0) // q_len == mine
    for h in range(MLA_HEADS):
        cols = slice(h * LANES, (h + 1) * LANES)
        o_h = _dot(o_lat[h * SLAB:(h + 1) * SLAB], wuv_ref[h])
        o_scr[:, cols] = jnp.where(sel, o_h, o_scr[:, cols])
    o_ref[...] = o_scr[...].astype(o_ref.dtype)


def _attn_s(page_table, qh, ckv_new, krp_new, cache_ckv, cache_krope, w_uk, w_uv, *, tk):
    DB, n_pages = page_table.shape
    page = cache_ckv.shape[1]
    q_len = qh.shape[0] // DB
    hp = MLA_HEADS * LANES
    per_slab = SLAB // q_len
    wabs = jnp.transpose(w_uk.reshape(MLA_KV_LORA, MLA_HEADS, MLA_NOPE), (1, 2, 0))
    wabs = jnp.pad(wabs, ((0, 0), (0, LANES - MLA_NOPE), (0, 0))).astype(BF16)
    idx = jnp.arange(MLA_ROPE)
    wsel = jnp.zeros((LANES, LANES), F32).at[MLA_NOPE + idx, idx].set(1.0).astype(BF16)
    wuv = jnp.pad(jnp.transpose(w_uv.reshape(MLA_KV_LORA, MLA_HEADS, MLA_V), (1, 0, 2)),
                  ((0, 0), (0, 0), (0, LANES - MLA_V))).astype(BF16)
    P = n_pages * page

    def slab(b, pt):
        return (b // per_slab, 0)

    grid_spec = pltpu.PrefetchScalarGridSpec(
        num_scalar_prefetch=1,
        grid=(DB,),
        in_specs=[pl.BlockSpec((SLAB, hp), slab),
                  pl.BlockSpec((SLAB, MLA_KV_LORA), slab),
                  pl.BlockSpec((SLAB, LANES), slab),
                  pl.BlockSpec(wabs.shape, lambda b, pt: (0, 0, 0)),
                  pl.BlockSpec(wsel.shape, lambda b, pt: (0, 0)),
                  pl.BlockSpec(wuv.shape, lambda b, pt: (0, 0, 0)),
                  pl.BlockSpec(memory_space=pl.ANY),
                  pl.BlockSpec(memory_space=pl.ANY)],
        out_specs=pl.BlockSpec((SLAB, hp), slab),
        scratch_shapes=[pltpu.VMEM((2, P, MLA_KV_LORA), F32),
                        pltpu.VMEM((2, P, MLA_ROPE), F32),
                        pltpu.SemaphoreType.DMA((2, 2)),
                        pltpu.VMEM((SLAB, hp), F32)],
    )
    return pl.pallas_call(
        functools.partial(_attn_s_kernel, n_pages=n_pages, page=page, q_len=q_len, tk=tk),
        grid_spec=grid_spec,
        out_shape=jax.ShapeDtypeStruct((DB * q_len, hp), BF16),
        compiler_params=_cparams(("arbitrary",)),
        name="attn_s",
    )(page_table, qh, ckv_new, krp_new, wabs, wsel, wuv, cache_ckv, cache_krope)


ROUTE_ROWS = 8


def _first_index(hit, idx, big):
    return jnp.min(jnp.where(hit, idx, big), axis=0, keepdims=True)


def _route(scores, sel):
    tm = scores.shape[1]
    gsz = N_EXPERTS // N_GROUPS
    sub = lax.broadcasted_iota(jnp.int32, (gsz, tm), 0)
    grp_rows = lax.broadcasted_iota(jnp.int32, (N_GROUPS, tm), 0)
    groups = [sel[g * gsz:(g + 1) * gsz, :] for g in range(N_GROUPS)]
    gscore = jnp.zeros((N_GROUPS, tm), F32)
    for g, x in enumerate(groups):
        m1 = jnp.max(x, axis=0, keepdims=True)
        first = _first_index(x == m1, sub, gsz)
        m2 = jnp.max(jnp.where(sub == first, NEG_INF, x), axis=0, keepdims=True)
        gscore = jnp.where(grp_rows == g, m1 + m2, gscore)
    chosen = jnp.zeros((N_GROUPS, tm), jnp.bool_)
    y = gscore
    for _ in range(TOPK_GROUPS):
        m = jnp.max(y, axis=0, keepdims=True)
        hit = grp_rows == _first_index(y == m, grp_rows, N_GROUPS)
        chosen = chosen | hit
        y = jnp.where(hit, NEG_INF, y)
    cand = [jnp.where(chosen[g:g + 1, :], x, NEG_INF) for g, x in enumerate(groups)]
    eids = [sub + g * gsz for g in range(N_GROUPS)]
    out_rows = lax.broadcasted_iota(jnp.int32, (ROUTE_ROWS, tm), 0)
    eidx = jnp.zeros((ROUTE_ROWS, tm), jnp.int32)
    wsel = jnp.zeros((ROUTE_ROWS, tm), F32)
    hits = []
    for k in range(TOP_K):
        m = functools.reduce(jnp.maximum, [jnp.max(c, axis=0, keepdims=True) for c in cand])
        first = functools.reduce(jnp.minimum, [_first_index(c == m, e, N_EXPERTS) for c, e in zip(cand, eids)])
        wk = jnp.zeros((1, tm), F32)
        hit_k = []
        for g in range(N_GROUPS):
            hit = eids[g] == first
            hit_k.append(jnp.where(hit, 1.0, 0.0))
            wk = wk + jnp.sum(jnp.where(hit, scores[g * gsz:(g + 1) * gsz, :], 0.0), axis=0, keepdims=True)
            cand[g] = jnp.where(hit, NEG_INF, cand[g])
        hits.append(jnp.concatenate(hit_k, axis=0))
        eidx = jnp.where(out_rows == k, first, eidx)
        wsel = jnp.where(out_rows == k, wk, wsel)
    total = jnp.sum(wsel, axis=0, keepdims=True)
    return eidx, wsel / total * ROUTED_SCALE, hits


def _ranks(hits, run):
    tm = hits[0].shape[1]
    sel = functools.reduce(jnp.add, hits)
    before = (lax.broadcasted_iota(jnp.int32, (tm, tm), 0) < lax.broadcasted_iota(jnp.int32, (tm, tm), 1))
    prefix = _dot(sel.astype(BF16), jnp.where(before, 1.0, 0.0).astype(BF16)) + run
    out_rows = lax.broadcasted_iota(jnp.int32, (ROUTE_ROWS, tm), 0)
    rank = jnp.zeros((ROUTE_ROWS, tm), F32)
    for k, hit in enumerate(hits):
        rank = jnp.where(out_rows == k, jnp.sum(hit * prefix, axis=0, keepdims=True), rank)
    run = run + jnp.sum(sel, axis=1, keepdims=True)
    return rank.astype(jnp.int32), run


def _split_hi_lo(a):
    hi = a.astype(BF16)
    lo = (a - hi.astype(F32)).astype(BF16)
    return hi, lo


def _mix_kernel(x_ref, om_ref, ret_ref, ga_ref, gb_ref, wo_ref, wr_ref, wout_ref, g1_ref, b1_ref,
                wrt_hi_ref, wrt_lo_ref, rb_ref, run0_ref,
                h_ref, hrow_ref, eidx_ref, gw_ref, rank_ref, run_ref):
    @pl.when(pl.program_id(0) == 0)
    def _():
        run_ref[...] = run0_ref[...]

    tm = h_ref.shape[0]
    y_a = _dot(om_ref[...], wo_ref[...])
    y_b = _dot(ret_ref[...], wr_ref[...])
    mixed_in = ga_ref[...].astype(F32) * y_a + gb_ref[...].astype(F32) * y_b
    mixed = _dot(mixed_in.astype(BF16), wout_ref[...])
    h = _layer_norm(DEEPNORM_ALPHA * x_ref[...] + mixed, g1_ref[...], b1_ref[...])
    h_ref[...] = h
    for s in range(h.shape[1] // LANES):
        hrow_ref[pl.ds(s, tm, stride=ROW_TILE), :] = h[:, s * LANES:(s + 1) * LANES]
    h_hi, h_lo = _split_hi_lo(h)
    logits = _dot_nt(wrt_hi_ref[...], h_hi) + (_dot_nt(wrt_hi_ref[...], h_lo) + _dot_nt(wrt_lo_ref[...], h_hi))
    scores = _sigmoid(logits)
    eidx, gw, hits = _route(scores, scores + rb_ref[...])
    rank, run = _ranks(hits, run_ref[...])
    eidx_ref[...] = eidx
    gw_ref[...] = gw
    rank_ref[...] = rank
    run_ref[...] = run


def _mix(x2d, o_mla, ret, ga, gb, w_o, w_ret_o, w_out, ln1_g, ln1_b, w_router, router_bias, run0, *, tm):
    T, D = x2d.shape
    assert D == ROW_TILE * LANES
    wrt = w_router.T
    wrt_hi, wrt_lo = _split_hi_lo(wrt)
    rb = router_bias.reshape(N_EXPERTS, 1).astype(F32)
    g1, b1 = ln1_g.reshape(1, D), ln1_b.reshape(1, D)
    wr = w_ret_o.astype(BF16)
    wout = w_out.astype(BF16)

    def row(i):
        return (i, 0)

    def full(a):
        return pl.BlockSpec(a.shape, lambda i: (0,) * a.ndim)

    route_spec = pl.BlockSpec((ROUTE_ROWS, tm), lambda i: (0, i))
    return pl.pallas_call(
        _mix_kernel,
        grid=(T // tm,),
        in_specs=[pl.BlockSpec((tm, D), row), pl.BlockSpec((tm, o_mla.shape[1]), row),
                  pl.BlockSpec((tm, ret.shape[1]), row), pl.BlockSpec((tm, D), row), pl.BlockSpec((tm, D), row),
                  full(w_o), full(wr), full(wout), full(g1), full(b1), full(wrt_hi), full(wrt_lo), full(rb),
                  full(run0)],
        out_specs=[pl.BlockSpec((tm, D), row), pl.BlockSpec((tm * ROW_TILE, LANES), row),
                   route_spec, route_spec, route_spec, pl.BlockSpec((N_EXPERTS, 1), lambda i: (0, 0))],
        out_shape=[jax.ShapeDtypeStruct((T, D), F32),
                   jax.ShapeDtypeStruct((T * ROW_TILE, LANES), F32),
                   jax.ShapeDtypeStruct((ROUTE_ROWS, T), jnp.int32),
                   jax.ShapeDtypeStruct((ROUTE_ROWS, T), F32),
                   jax.ShapeDtypeStruct((ROUTE_ROWS, T), jnp.int32),
                   jax.ShapeDtypeStruct((N_EXPERTS, 1), F32)],
        compiler_params=_cparams(("arbitrary",)),
        name="mix",
    )(x2d, o_mla, ret, ga, gb, w_o, wr, wout, g1, b1, wrt_hi, wrt_lo, rb, run0)


def _n_row_blocks(n_tokens):
    return (n_tokens * TOP_K + N_EXPERTS * (MOE_BLOCK - 1) + MOE_BLOCK - 1) // MOE_BLOCK


def _block_plan(counts, n_blocks):
    counts = counts.reshape(N_EXPERTS).astype(jnp.int32)
    pad_len = (counts + MOE_BLOCK - 1) // MOE_BLOCK * MOE_BLOCK
    pad_end = jnp.cumsum(pad_len)
    pad_start = pad_end - pad_len
    blk_e = jnp.minimum(jnp.searchsorted(pad_end, jnp.arange(n_blocks) * MOE_BLOCK, side="right"),
                        N_EXPERTS - 1).astype(jnp.int32)
    n_used = (pad_end[-1:] // MOE_BLOCK).astype(jnp.int32)
    return pad_start.astype(jnp.int32), pad_len.astype(jnp.int32), blk_e, n_used


def _tile_copy(src, src_row, dst, dst_row, sem):
    return pltpu.make_async_copy(src.at[pl.ds(src_row * ROW_TILE, ROW_TILE), :],
                                 dst.at[pl.ds(dst_row * ROW_TILE, ROW_TILE), :], sem)


def _dispatch_kernel(ps_ref, pl_ref, nu_ref, eidx_ref, rank_ref, eidx2_ref, rank2_ref, hrow_ref, hrow2_ref,
                     xs_hbm, zbuf, sem, zsem, *, tm, n_blocks):
    blk_rows = MOE_BLOCK * ROW_TILE

    def scatter(e_ref, r_ref, src_ref, n):
        def start(t, c):
            for k in range(TOP_K):
                dst = ps_ref[e_ref[k, t]] + r_ref[k, t]
                _tile_copy(src_ref, t, xs_hbm, dst, sem).start()
            return c

        def wait(t, c):
            for k in range(TOP_K):
                _tile_copy(src_ref, t, xs_hbm, 0, sem).wait()
            return c

        lax.fori_loop(0, n, start, 0)
        lax.fori_loop(0, n, wait, 0)

    @pl.when(pl.program_id(0) == 0)
    def _():
        zbuf[...] = jnp.zeros(zbuf.shape, F32)

        def zcopy(block_row):
            return pltpu.make_async_copy(zbuf, xs_hbm.at[pl.ds(block_row * ROW_TILE, blk_rows), :], zsem)

        def each_expert(fn):
            def body(e, c):
                @pl.when(pl_ref[e] > 0)
                def _():
                    fn(zcopy(ps_ref[e] + pl_ref[e] - MOE_BLOCK))
                return c
            lax.fori_loop(0, N_EXPERTS, body, 0)

        def each_tail(fn):
            def body(j, c):
                fn(zcopy(j * MOE_BLOCK))
                return c
            lax.fori_loop(nu_ref[0], n_blocks, body, 0)

        each_expert(lambda cp: cp.start())
        each_tail(lambda cp: cp.start())
        each_expert(lambda cp: cp.wait())
        each_tail(lambda cp: cp.wait())
        scatter(eidx2_ref, rank2_ref, hrow2_ref, eidx2_ref.shape[1])

    scatter(eidx_ref, rank_ref, hrow_ref, tm)


def _dispatch(pad_start, pad_len, n_used, n_blocks, main, second, *, tm):
    eidx, rank, hrow = main
    eidx2, rank2, hrow2 = second
    T = eidx.shape[1]
    smem = pl.BlockSpec(memory_space=pltpu.SMEM)
    route = pl.BlockSpec((ROUTE_ROWS, tm), lambda i: (0, i), memory_space=pltpu.SMEM)
    return pl.pallas_call(
        functools.partial(_dispatch_kernel, tm=tm, n_blocks=n_blocks),
        grid=(T // tm,),
        in_specs=[smem, smem, smem, route, route, smem, smem,
                  pl.BlockSpec((tm * ROW_TILE, LANES), lambda i: (i, 0)),
                  pl.BlockSpec(hrow2.shape, lambda i: (0, 0))],
        out_specs=pl.BlockSpec(memory_space=pl.ANY),
        out_shape=jax.ShapeDtypeStruct((n_blocks * MOE_BLOCK * ROW_TILE, LANES), F32),
        scratch_shapes=[pltpu.VMEM((MOE_BLOCK * ROW_TILE, LANES), F32),
                        pltpu.SemaphoreType.DMA(()), pltpu.SemaphoreType.DMA(())],
        compiler_params=_cparams(("arbitrary",)),
        name="dispatch",
    )(pad_start, pad_len, n_used, eidx, rank, eidx2, rank2, hrow, hrow2)


def _from_row_tiles(ref, n_rows, base=0):
    return jnp.concatenate([ref[pl.ds(base * ROW_TILE + s, n_rows, stride=ROW_TILE), :]
                            for s in range(ROW_TILE)], axis=1)


def _experts_kernel(be_ref, nu_ref, x_ref, wg_ref, wu_ref, wd_ref, y_ref):
    del be_ref

    @pl.when(pl.program_id(0) < nu_ref[0])
    def _():
        x = _from_row_tiles(x_ref, MOE_BLOCK).astype(BF16)
        gate = _dot(x, wg_ref[...].astype(BF16))
        up = _dot(x, wu_ref[...].astype(BF16))
        hid = (gate * _sigmoid(gate) * up).astype(BF16)
        y = _dot(hid, wd_ref[...].astype(BF16))
        for s in range(ROW_TILE):
            y_ref[pl.ds(s, MOE_BLOCK, stride=ROW_TILE), :] = y[:, s * LANES:(s + 1) * LANES]

    @pl.when(pl.program_id(0) >= nu_ref[0])
    def _():
        y_ref[...] = jnp.zeros(y_ref.shape, F32)


def _experts(xs, blk_e, n_used, w_gate, w_up, w_down):
    n_blocks = blk_e.shape[0]
    D = w_gate.shape[1]

    def blk(j, be, nu):
        return (jnp.minimum(j, nu[0] - 1), 0)

    def out_blk(j, be, nu):
        return (j, 0)

    def wsel(j, be, nu):
        return (be[jnp.minimum(j, nu[0] - 1)], 0, 0)

    rows = MOE_BLOCK * ROW_TILE
    grid_spec = pltpu.PrefetchScalarGridSpec(
        num_scalar_prefetch=2,
        grid=(n_blocks,),
        in_specs=[pl.BlockSpec((rows, LANES), blk),
                  pl.BlockSpec((None, D, EXPERT_FF), wsel),
                  pl.BlockSpec((None, D, EXPERT_FF), wsel),
                  pl.BlockSpec((None, EXPERT_FF, D), wsel)],
        out_specs=pl.BlockSpec((rows, LANES), out_blk),
    )
    return pl.pallas_call(
        _experts_kernel,
        grid_spec=grid_spec,
        out_shape=jax.ShapeDtypeStruct(xs.shape, F32),
        compiler_params=_cparams(("arbitrary",)),
        name="experts",
    )(blk_e, n_used, xs, w_gate, w_up, w_down)


def _combine_kernel(ps_ref, eidx_ref, rank_ref, h_ref, gw_ref, y_hbm, wsg_ref, wsu_ref, wsd_ref, g2_ref, b2_ref,
                    o_ref, ybuf, sem, *, tm):
    def start(t, c):
        for k in range(TOP_K):
            src = ps_ref[eidx_ref[k, t]] + rank_ref[k, t]
            _tile_copy(y_hbm, src, ybuf, k * tm + t, sem).start()
        return c

    def wait(t, c):
        for k in range(TOP_K):
            _tile_copy(y_hbm, 0, ybuf, k * tm + t, sem).wait()
        return c

    lax.fori_loop(0, tm, start, 0)
    h = h_ref[...]
    hb = h.astype(BF16)
    gate = _dot(hb, wsg_ref[...])
    up = _dot(hb, wsu_ref[...])
    ffn = _dot((gate * _sigmoid(gate) * up).astype(BF16), wsd_ref[...])
    lax.fori_loop(0, tm, wait, 0)
    gw = gw_ref[...]
    for k in range(TOP_K):
        ffn = ffn + _from_row_tiles(ybuf, tm, base=k * tm) * gw[:, k:k + 1]
    o_ref[...] = _layer_norm(DEEPNORM_ALPHA * h + ffn, g2_ref[...], b2_ref[...])


def _combine(pad_start, eidx, rank, h, gw_t, ys, w_sh_gate, w_sh_up, w_sh_down, ln2_g, ln2_b, *, tm):
    T, D = h.shape
    wsg, wsu, wsd = w_sh_gate.astype(BF16), w_sh_up.astype(BF16), w_sh_down.astype(BF16)
    g2, b2 = ln2_g.reshape(1, D), ln2_b.reshape(1, D)

    def full(a):
        return pl.BlockSpec(a.shape, lambda i: (0,) * a.ndim)

    route = pl.BlockSpec((ROUTE_ROWS, tm), lambda i: (0, i), memory_space=pltpu.SMEM)
    return pl.pallas_call(
        functools.partial(_combine_kernel, tm=tm),
        grid=(T // tm,),
        in_specs=[pl.BlockSpec(memory_space=pltpu.SMEM), route, route,
                  pl.BlockSpec((tm, D), lambda i: (i, 0)),
                  pl.BlockSpec((tm, ROUTE_ROWS), lambda i: (i, 0)),
                  pl.BlockSpec(memory_space=pl.ANY),
                  full(wsg), full(wsu), full(wsd), full(g2), full(b2)],
        out_specs=pl.BlockSpec((tm, D), lambda i: (i, 0)),
        out_shape=jax.ShapeDtypeStruct((T, D), F32),
        scratch_shapes=[pltpu.VMEM((TOP_K * tm * ROW_TILE, LANES), F32), pltpu.SemaphoreType.DMA(())],
        compiler_params=_cparams(("arbitrary",)),
        name="combine",
    )(pad_start, eidx, rank, h, gw_t, ys, wsg, wsu, wsd, g2, b2)


def kernel(x_prompt, x_sample, cache_ckv, cache_krope, state_ret, page_table, w_in, q_norm_g, w_q_up, kv_norm_g,
           w_uk, w_uv, ret_gn_g, w_mla_o, w_ret_o, w_out, ln1_g, ln1_b, w_router, router_bias,
           w_exp_gate, w_exp_up, w_exp_down, w_sh_gate, w_sh_up, w_sh_down, ln2_g, ln2_b):
    B, S, D = x_prompt.shape
    DB, Q, _ = x_sample.shape
    Tp, Ts = B * S, DB * Q
    past_len = page_table.shape[1] * cache_ckv.shape[1]
    w_small, w_ret, w_gate, w_q, w_k, w_v, w_o = _prep_weights(w_in, w_q_up, w_uk, w_uv, w_mla_o)
    pw = (w_small, w_ret, w_gate, w_q, w_k, w_v)

    tab_p = _rope_tables(jnp.arange(S))
    (qh, ckv_p, kr_p, _, rq, rk, rv, rg, ga, gb, kh, vh) = _proj(
        x_prompt.reshape(Tp, D), tab_p, pw, q_norm_g, kv_norm_g, tm=512, ret_dtype=BF16)
    o_mla = _attn_p(qh, kh, vh, B, S, tq=512, tk=512, hps=2)
    ret, ret_state_p = _ret_p(rq, rk, rv, rg, ret_gn_g, B, S, cb=4)
    mix_w = (w_o, w_ret_o, w_out, ln1_g, ln1_b, w_router, router_bias)
    h_p, hrow_p, eidx_p, gw_p, rank_p, cnt_p = _mix(x_prompt.reshape(Tp, D), o_mla, ret, ga, gb, *mix_w,
                                                      jnp.zeros((N_EXPERTS, 1), F32), tm=256)

    tab_s = _rope_tables(jnp.tile(past_len + jnp.arange(Q), DB))
    (qh_s, ckv_s, kr_s, krp_s, rq_s, rk_s, rv_s, rg_s, ga_s, gb_s, _, _) = _proj(
        x_sample.reshape(Ts, D), tab_s, pw, q_norm_g, kv_norm_g, tm=Ts, ret_dtype=F32)
    o_mla_s = _attn_s(page_table, qh_s, ckv_s, krp_s, cache_ckv, cache_krope, w_uk, w_uv, tk=min(1024, past_len))
    ret_s, ret_state_s = _ret_s(rq_s, rk_s, rv_s, rg_s, state_ret, ret_gn_g, gb=min(16, DB))
    h_s, hrow_s, eidx_s, gw_s, rank_s, cnt = _mix(x_sample.reshape(Ts, D), o_mla_s, ret_s, ga_s, gb_s, *mix_w,
                                                  cnt_p, tm=min(256, Ts))

    n_blocks = _n_row_blocks(Tp + Ts)
    pad_start, pad_len, blk_e, n_used = _block_plan(cnt, n_blocks)
    xs = _dispatch(pad_start, pad_len, n_used, n_blocks, (eidx_p, rank_p, hrow_p), (eidx_s, rank_s, hrow_s), tm=256)
    ys = _experts(xs, blk_e, n_used, w_exp_gate, w_exp_up, w_exp_down)
    tmc = 128
    shared = (w_sh_gate, w_sh_up, w_sh_down, ln2_g, ln2_b)
    y_p = _combine(pad_start, eidx_p, rank_p, h_p, gw_p.T, ys, *shared, tm=tmc)
    y_s = _combine(pad_start, eidx_s, rank_s, h_s, gw_s.T, ys, *shared, tm=tmc)

    return (y_p.reshape(B, S, D), y_s.reshape(DB, Q, D),
            ckv_p.reshape(B, S, -1), kr_p.reshape(B, S, -1), ret_state_p,
            ckv_s.reshape(DB, Q, -1), kr_s.reshape(DB, Q, -1), ret_state_s)
```

```python
import functools
import math

import numpy as np
import jax
import jax.numpy as jnp
from jax import lax
from jax.experimental import pallas as pl
from jax.experimental.pallas import tpu as pltpu

F32 = jnp.float32
BF16 = jnp.bfloat16

MLA_HEADS = 8
MLA_NOPE = 64
MLA_ROPE = 32
MLA_V = 64
MLA_Q_LORA = 256
MLA_KV_LORA = 128
MLA_SCALE = (MLA_NOPE + MLA_ROPE) ** -0.5
Q_SCALE = MLA_SCALE * math.log2(math.e)
RET_HEADS = 4
RET_DK = 128
RET_DV = 128
RET_CHUNK = 128
N_EXPERTS = 64
N_GROUPS = 8
TOPK_GROUPS = 4
TOP_K = 6
EXPERT_FF = 256
SHARED_FF = 256
ROUTED_SCALE = 2.5
MOE_BLOCK = 128
ROPE_BASE = 10000.0
LN_EPS = 1e-5
RMS_EPS = 1e-6
DEPTH = 1
DEEPNORM_ALPHA = (2 * DEPTH) ** 0.25

LANES = 128
ROW_TILE = 8
N_DMA_PRIORITIES = 2
HALF_ROPE = MLA_ROPE // 2
VMEM_LIMIT = 56 * 1024 * 1024
NEG_INF = float("-inf")


def _cparams(sem):
    return pltpu.CompilerParams(dimension_semantics=sem, vmem_limit_bytes=VMEM_LIMIT)


def _dot(a, b):
    return jnp.dot(a, b, preferred_element_type=F32)


def _dot_nt(a, b):
    return lax.dot_general(a, b, (((1,), (1,)), ((), ())), preferred_element_type=F32)


def _dot_tn(a, b):
    return lax.dot_general(a, b, (((0,), (0,)), ((), ())), preferred_element_type=F32)


N_TAB = 8


def _rope_tables(pos):
    L = pos.shape[0]
    posf = pos.astype(F32)[:, None]
    half_r = RET_DK // 2
    ang_r = posf * (ROPE_BASE ** (-jnp.arange(half_r, dtype=F32) / half_r))[None, :]
    cos_r = jnp.concatenate([jnp.cos(ang_r), jnp.cos(ang_r)], axis=1)
    sin_r = jnp.concatenate([-jnp.sin(ang_r), jnp.sin(ang_r)], axis=1)
    ang_m = posf * (ROPE_BASE ** (-jnp.arange(HALF_ROPE, dtype=F32) / HALF_ROPE))[None, :]
    c, s = jnp.cos(ang_m), jnp.sin(ang_m)
    z16 = jnp.zeros((L, HALF_ROPE), F32)

    def place(parts, offset):
        body = jnp.concatenate(parts, axis=1)
        return jnp.concatenate([jnp.zeros((L, offset), F32), body,
                                jnp.zeros((L, LANES - offset - body.shape[1]), F32)], axis=1)

    cos_k = place([c, c], 0)
    sinp_k = place([z16, s], 0)
    sinm_k = place([-s, z16], 0)
    ones = jnp.ones((L, MLA_NOPE), F32)
    cos_q = jnp.concatenate([ones, c, c, jnp.zeros((L, LANES - MLA_NOPE - MLA_ROPE), F32)], axis=1) * Q_SCALE
    sinp_q = place([z16, s], MLA_NOPE) * Q_SCALE
    sinm_q = place([-s, z16], MLA_NOPE) * Q_SCALE
    return jnp.concatenate([cos_r, sin_r, cos_k, sinp_k, sinm_k, cos_q, sinp_q, sinm_q], axis=1)


def _prep_weights(w_in, w_q_up, w_uk, w_uv, w_mla_o):
    d = w_in.shape[0]
    c_q, c_kv, c_kr = MLA_Q_LORA, MLA_KV_LORA, MLA_ROPE
    o_ret = c_q + c_kv + c_kr
    n_ret = 2 * RET_HEADS * RET_DK + 2 * RET_HEADS * RET_DV
    w_small = jnp.concatenate([w_in[:, :o_ret], jnp.zeros((d, LANES - c_kr), F32)], axis=1).astype(BF16)
    w_ret = w_in[:, o_ret:o_ret + n_ret].astype(BF16)
    w_gate = w_in[:, o_ret + n_ret:].astype(BF16)
    hd = MLA_NOPE + MLA_ROPE
    w_q = jnp.pad(w_q_up.reshape(c_q, MLA_HEADS, hd), ((0, 0), (0, 0), (0, LANES - hd)))
    w_q = w_q.reshape(c_q, MLA_HEADS * LANES).astype(BF16)
    wk_top = jnp.pad(w_uk.reshape(c_kv, MLA_HEADS, MLA_NOPE), ((0, 0), (0, 0), (0, LANES - MLA_NOPE)))
    place = jnp.zeros((LANES, MLA_HEADS, LANES), F32)
    idx = jnp.arange(MLA_ROPE)
    place = place.at[idx, :, MLA_NOPE + idx].set(1.0)
    w_k = jnp.concatenate([wk_top.reshape(c_kv, -1), place.reshape(LANES, -1)], axis=0).astype(BF16)
    w_v = jnp.pad(w_uv.reshape(c_kv, MLA_HEADS, MLA_V), ((0, 0), (0, 0), (0, LANES - MLA_V)))
    w_v = w_v.reshape(c_kv, MLA_HEADS * LANES).astype(BF16)
    w_o = jnp.pad(w_mla_o.reshape(MLA_HEADS, MLA_V, -1), ((0, 0), (0, LANES - MLA_V), (0, 0)))
    w_o = w_o.reshape(MLA_HEADS * LANES, -1).astype(BF16)
    return w_small, w_ret, w_gate, w_q, w_k, w_v, w_o


def _rms_norm(x, g):
    inv = lax.rsqrt(jnp.mean(x * x, axis=-1, keepdims=True) + RMS_EPS)
    return x * inv * g


def _layer_norm(x, g, b):
    mu = jnp.mean(x, axis=-1, keepdims=True)
    xc = x - mu
    var = jnp.mean(xc * xc, axis=-1, keepdims=True)
    return xc * lax.rsqrt(var + LN_EPS) * g + b


def _sigmoid(x):
    return 1.0 / (1.0 + jnp.exp(-x))


def _proj_kernel(x_ref, tab_ref, wsm_ref, wret_ref, wg_ref, qg_ref, wq_ref, kvg_ref, wk_ref, wv_ref,
                 qh_ref, ckv_ref, kr_ref, krp_ref, rq_ref, rk_ref, rv_ref, rg_ref, ga_ref, gb_ref,
                 kh_ref, vh_ref):
    xb = x_ref[...].astype(BF16)

    def tab(i):
        return tab_ref[:, i * LANES:(i + 1) * LANES]

    small = _dot(xb, wsm_ref[...])
    cq = small[:, :MLA_Q_LORA]
    ckv = small[:, MLA_Q_LORA:MLA_Q_LORA + MLA_KV_LORA]
    krb = small[:, MLA_Q_LORA + MLA_KV_LORA:]
    q = _dot(_rms_norm(cq, qg_ref[...]).astype(BF16), wq_ref[...])
    cos_q, sinp_q, sinm_q = tab(5), tab(6), tab(7)
    for h in range(MLA_HEADS):
        blk = q[:, h * LANES:(h + 1) * LANES]
        rot = (blk * cos_q + pltpu.roll(blk, HALF_ROPE, 1) * sinp_q
               + pltpu.roll(blk, LANES - HALF_ROPE, 1) * sinm_q)
        qh_ref[:, h * LANES:(h + 1) * LANES] = rot.astype(BF16)
    ckvn = _rms_norm(ckv, kvg_ref[...])
    ckv_ref[...] = ckvn
    krr = (krb * tab(2) + pltpu.roll(krb, HALF_ROPE, 1) * tab(3)
           + pltpu.roll(krb, LANES - HALF_ROPE, 1) * tab(4))
    kr_ref[...] = krr[:, :MLA_ROPE]
    krp_ref[...] = krr
    kcat = jnp.concatenate([ckvn, krr], axis=1).astype(BF16)
    kh_ref[...] = _dot(kcat, wk_ref[...]).astype(BF16)
    vv = _dot(kcat[:, :MLA_KV_LORA], wv_ref[...])
    lane = lax.broadcasted_iota(jnp.int32, vv.shape, 1) % LANES
    vh_ref[...] = jnp.where(lane == MLA_V, 1.0, vv).astype(BF16)

    r = _dot(xb, wret_ref[...])
    cos_r, sin_r = tab(0), tab(1)
    nq = RET_HEADS * RET_DK
    for h in range(RET_HEADS):
        sl = slice(h * RET_DK, (h + 1) * RET_DK)
        a = r[:, sl]
        rq_ref[:, sl] = (a * cos_r + pltpu.roll(a, RET_DK // 2, 1) * sin_r).astype(rq_ref.dtype)
        b = r[:, nq + h * RET_DK:nq + (h + 1) * RET_DK]
        rk_ref[:, sl] = ((b * cos_r + pltpu.roll(b, RET_DK // 2, 1) * sin_r)
                         * (RET_DK ** -0.5)).astype(rk_ref.dtype)
    rv_ref[...] = r[:, 2 * nq:2 * nq + RET_HEADS * RET_DV].astype(rv_ref.dtype)
    rg = r[:, 2 * nq + RET_HEADS * RET_DV:]
    rg_ref[...] = (rg * _sigmoid(rg)).astype(rg_ref.dtype)

    g = _dot(xb, wg_ref[...])
    d = ga_ref.shape[1]
    ga_ref[...] = _sigmoid(g[:, :d]).astype(ga_ref.dtype)
    gb_ref[...] = _sigmoid(g[:, d:]).astype(gb_ref.dtype)


def _proj(x2d, tab, weights, q_norm_g, kv_norm_g, *, tm, ret_dtype):
    T, D = x2d.shape
    w_small, w_ret, w_gate, w_q, w_k, w_v = weights
    n_tab = tab.shape[0] // tm
    nr = RET_HEADS * RET_DK
    hp = MLA_HEADS * LANES

    def row(i):
        return (i, 0)

    def const(i):
        return (0, 0)

    def full(a):
        return pl.BlockSpec(a.shape, const)

    out_shapes = [
        jax.ShapeDtypeStruct((T, hp), BF16),
        jax.ShapeDtypeStruct((T, MLA_KV_LORA), F32),
        jax.ShapeDtypeStruct((T, MLA_ROPE), F32),
        jax.ShapeDtypeStruct((T, LANES), F32),
        jax.ShapeDtypeStruct((T, nr), ret_dtype),
        jax.ShapeDtypeStruct((T, nr), ret_dtype),
        jax.ShapeDtypeStruct((T, nr), ret_dtype),
        jax.ShapeDtypeStruct((T, nr), BF16),
        jax.ShapeDtypeStruct((T, D), BF16),
        jax.ShapeDtypeStruct((T, D), BF16),
        jax.ShapeDtypeStruct((T, hp), BF16),
        jax.ShapeDtypeStruct((T, hp), BF16),
    ]
    out_specs = [pl.BlockSpec((tm, s.shape[1]), row) for s in out_shapes]
    qg = q_norm_g.reshape(1, -1)
    kvg = kv_norm_g.reshape(1, -1)
    in_specs = [pl.BlockSpec((tm, D), row),
                pl.BlockSpec((tm, N_TAB * LANES), lambda i: (i % n_tab, 0)),
                full(w_small), full(w_ret), full(w_gate), full(qg), full(w_q), full(kvg), full(w_k), full(w_v)]
    return pl.pallas_call(
        _proj_kernel,
        grid=(T // tm,),
        in_specs=in_specs,
        out_specs=out_specs,
        out_shape=out_shapes,
        compiler_params=_cparams(("parallel",)),
        name="proj",
    )(x2d, tab, w_small, w_ret, w_gate, qg, w_q, kvg, w_k, w_v)


def _attn_p_kernel(q_ref, k_ref, v_ref, o_ref, m_ref, acc_ref, *, tq, tk, hps):
    i = pl.program_id(2)
    m_ref[...] = jnp.full(m_ref.shape, NEG_INF, F32)
    acc_ref[...] = jnp.zeros(acc_ref.shape, F32)

    def step(j, masked):
        for hh in range(hps):
            cols = slice(hh * LANES, (hh + 1) * LANES)
            k = k_ref[pl.ds(j * tk, tk), cols]
            v = v_ref[pl.ds(j * tk, tk), cols]
            s = _dot_nt(q_ref[:, cols], k)
            if masked:
                row = lax.broadcasted_iota(jnp.int32, (tq, tk), 0) + i * tq
                col = lax.broadcasted_iota(jnp.int32, (tq, tk), 1) + j * tk
                s = jnp.where(col <= row, s, NEG_INF)
            m_prev = m_ref[hh]
            m_new = jnp.maximum(m_prev, jnp.max(s, axis=1, keepdims=True))
            p = jnp.concatenate([jnp.exp2(s[:, c * LANES:(c + 1) * LANES] - m_new)
                                 for c in range(tk // LANES)], axis=1)
            acc_ref[hh] = jnp.exp2(m_prev - m_new) * acc_ref[hh] + _dot(p.astype(BF16), v)
            m_ref[hh] = m_new

    n_full = (i * tq) // tk

    def body(j, c):
        step(j, False)
        return c

    lax.fori_loop(0, n_full, body, 0)
    for jj in range(tq // tk):
        step(n_full + jj, True)
    for hh in range(hps):
        acc = acc_ref[hh]
        o_ref[:, hh * LANES:(hh + 1) * LANES] = (acc / acc[:, MLA_V:MLA_V + 1]).astype(o_ref.dtype)


def _attn_p(qh, kh, vh, B, S, *, tq, tk, hps):
    assert tq % tk == 0 and MLA_HEADS % hps == 0
    nq = S // tq
    hp = MLA_HEADS * LANES
    kh3 = kh.reshape(B, S, hp)
    vh3 = vh.reshape(B, S, hp)
    return pl.pallas_call(
        functools.partial(_attn_p_kernel, tq=tq, tk=tk, hps=hps),
        grid=(B, MLA_HEADS // hps, nq),
        in_specs=[pl.BlockSpec((tq, hps * LANES), lambda b, h, i: (b * nq + i, h)),
                  pl.BlockSpec((None, S, hps * LANES), lambda b, h, i: (b, 0, h)),
                  pl.BlockSpec((None, S, hps * LANES), lambda b, h, i: (b, 0, h))],
        out_specs=pl.BlockSpec((tq, hps * LANES), lambda b, h, i: (b * nq + i, h)),
        out_shape=jax.ShapeDtypeStruct((B * S, hp), BF16),
        scratch_shapes=[pltpu.VMEM((hps, tq, LANES), F32), pltpu.VMEM((hps, tq, LANES), F32)],
        compiler_params=_cparams(("parallel", "parallel", "arbitrary")),
        name="attn_p",
    )(qh, kh3, vh3)


def _ret_consts(C):
    lg = jnp.log(1.0 - 2.0 ** (-5.0 - jnp.arange(RET_HEADS, dtype=F32)))
    idx = jnp.arange(C, dtype=F32)
    diff = idx[:, None] - idx[None, :]
    dmask = jnp.where(diff >= 0, jnp.exp(jnp.maximum(diff, 0.0)[None] * lg[:, None, None]), 0.0)
    q_dec = jnp.exp((idx[None, :] + 1.0) * lg[:, None])[:, :, None]
    k_dec = jnp.exp((C - 1.0 - idx)[None, :] * lg[:, None])[:, :, None]
    s_dec = jnp.exp(C * lg)
    return dmask, q_dec, k_dec, s_dec


def _head_norm_gate(o, gate, gn):
    mu = jnp.mean(o, axis=-1, keepdims=True)
    oc = o - mu
    var = jnp.mean(oc * oc, axis=-1, keepdims=True)
    return gate * (oc * lax.rsqrt(var + LN_EPS) * gn)


def _ret_p_kernel(sdec_ref, q_ref, k_ref, v_ref, g_ref, dm_ref, qd_ref, kd_ref, gn_ref,
                  o_ref, s_ref, *, nb, cb):
    C = RET_CHUNK

    @pl.when(pl.program_id(0) == 0)
    def _():
        s_ref[...] = jnp.zeros(s_ref.shape, F32)

    for c in range(cb):
        rows = slice(c * C, (c + 1) * C)
        for b in range(nb):
            for h in range(RET_HEADS):
                cols = slice(h * RET_DK, (h + 1) * RET_DK)
                q = q_ref[b, rows, cols]
                k = k_ref[b, rows, cols]
                v = v_ref[b, rows, cols]
                state = s_ref[b, h]
                att = _dot_nt(q, k) * dm_ref[h]
                o = _dot(att.astype(BF16), v) + _dot(q, state.astype(BF16)) * qd_ref[h]
                kd = (k.astype(F32) * kd_ref[h]).astype(BF16)
                s_ref[b, h] = state * sdec_ref[h] + _dot_tn(kd, v)
                gate = g_ref[b, rows, cols].astype(F32)
                o_ref[b, rows, cols] = _head_norm_gate(o, gate, gn_ref[:, cols]).astype(o_ref.dtype)


def _ret_p(rq, rk, rv, rg, ret_gn_g, B, S, *, cb):
    C = RET_CHUNK
    nr = RET_HEADS * RET_DK
    dmask, q_dec, k_dec, s_dec = _ret_consts(C)
    blk = pl.BlockSpec((B, cb * C, nr), lambda g: (0, g, 0))

    def full(a):
        return pl.BlockSpec(a.shape, lambda g: (0,) * a.ndim)

    gn = ret_gn_g.reshape(1, nr)
    args = [a.reshape(B, S, nr) for a in (rq, rk, rv, rg)]
    ret, state = pl.pallas_call(
        functools.partial(_ret_p_kernel, nb=B, cb=cb),
        grid=(S // (cb * C),),
        in_specs=[pl.BlockSpec(memory_space=pltpu.SMEM), blk, blk, blk, blk,
                  full(dmask), full(q_dec), full(k_dec), full(gn)],
        out_specs=[blk, pl.BlockSpec((B, RET_HEADS, RET_DK, RET_DV), lambda g: (0, 0, 0, 0))],
        out_shape=[jax.ShapeDtypeStruct((B, S, nr), BF16),
                   jax.ShapeDtypeStruct((B, RET_HEADS, RET_DK, RET_DV), F32)],
        compiler_params=_cparams(("arbitrary",)),
        name="ret_p",
    )(s_dec, *args, dmask, q_dec, k_dec, gn)
    return ret.reshape(B * S, nr), state


def _ret_s_kernel(sdec_ref, q_ref, k_ref, v_ref, g_ref, s0_ref, dm_ref, qd_ref, kd_ref, gn_ref,
                  o_ref, s_ref, *, gb, q_len):
    rows = gb * q_len
    row_b = lax.broadcasted_iota(jnp.int32, (rows, RET_DV), 0) // q_len
    for h in range(RET_HEADS):
        cols = slice(h * RET_DK, (h + 1) * RET_DK)
        q = q_ref[:, cols].astype(BF16)
        k = k_ref[:, cols]
        v = v_ref[:, cols].astype(BF16)
        att = _dot_nt(q, k.astype(BF16)) * dm_ref[h]
        o = _dot(att.astype(BF16), v)
        kd = k * kd_ref[h]
        inter = jnp.zeros((rows, RET_DV), F32)
        for b in range(gb):
            state = s0_ref[b, h]
            inter = jnp.where(row_b == b, _dot(q, state.astype(BF16)), inter)
            kd_b = jnp.where(row_b == b, kd, 0.0).astype(BF16)
            s_ref[b, h] = state * sdec_ref[h] + _dot_tn(kd_b, v)
        o = o + inter * qd_ref[h]
        gate = g_ref[:, cols].astype(F32)
        o_ref[:, cols] = _head_norm_gate(o, gate, gn_ref[:, cols]).astype(o_ref.dtype)


def _ret_s(rq, rk, rv, rg, state, ret_gn_g, *, gb):
    DB = state.shape[0]
    q_len = rq.shape[0] // DB
    nr = RET_HEADS * RET_DK
    rows = gb * q_len
    dmask, q_dec, k_dec, s_dec = _ret_consts(q_len)
    same = (jnp.arange(rows)[:, None] // q_len) == (jnp.arange(rows)[None, :] // q_len)
    dm = jnp.where(same[None], jnp.tile(dmask, (1, gb, gb)), 0.0)
    qd = jnp.tile(q_dec, (1, gb, 1))
    kd = jnp.tile(k_dec, (1, gb, 1))
    gn = ret_gn_g.reshape(1, nr)
    blk = pl.BlockSpec((rows, nr), lambda g: (g, 0))
    sblk = pl.BlockSpec((gb, RET_HEADS, RET_DK, RET_DV), lambda g: (g, 0, 0, 0))

    def full(a):
        return pl.BlockSpec(a.shape, lambda g: (0,) * a.ndim)

    return pl.pallas_call(
        functools.partial(_ret_s_kernel, gb=gb, q_len=q_len),
        grid=(DB // gb,),
        in_specs=[pl.BlockSpec(memory_space=pltpu.SMEM), blk, blk, blk, blk, sblk,
                  full(dm), full(qd), full(kd), full(gn)],
        out_specs=[blk, sblk],
        out_shape=[jax.ShapeDtypeStruct((DB * q_len, nr), BF16),
                   jax.ShapeDtypeStruct(state.shape, F32)],
        compiler_params=_cparams(("parallel",)),
        name="ret_s",
    )(s_dec, rq, rk, rv, rg, state, dm, qd, kd, gn)


SLAB = 16


def _attn_s_kernel(pt_ref, q_ref, cn_ref, kn_ref, wabs_ref, wsel_ref, wuv_ref, ckv_hbm, krt_hbm,
                   o_ref, ckv_buf, krt_buf, sem, o_scr, *, n_pages, page, q_len, tk):
    b = pl.program_id(0)
    nb = pl.num_programs(0)
    slot = b % 2
    per_slab = SLAB // q_len
    P = n_pages * page
    mine = b % per_slab
    R = MLA_HEADS * q_len

    def page_copies(bi, s):
        out = []
        for p in range(n_pages):
            pg = pt_ref[bi, p]
            out.append(pltpu.make_async_copy(ckv_hbm.at[pg], ckv_buf.at[s, pl.ds(p * page, page), :], sem.at[0, s]))
            out.append(pltpu.make_async_copy(krt_hbm.at[pg], krt_buf.at[s, pl.ds(p * MLA_ROPE, MLA_ROPE), :],
                                             sem.at[1, s]))
        return out

    @pl.when(b == 0)
    def _():
        for cp in page_copies(0, 0):
            cp.start()

    @pl.when(b + 1 < nb)
    def _():
        for cp in page_copies(b + 1, 1 - slot):
            cp.start()

    ql, qr = [], []
    for h in range(MLA_HEADS):
        qh = q_ref[:, h * LANES:(h + 1) * LANES]
        ql.append(_dot(qh, wabs_ref[h]))
        qr.append(_dot(qh, wsel_ref[...]))
    ql = jnp.concatenate(ql, axis=0).astype(BF16)
    qr = jnp.concatenate(qr, axis=0).astype(BF16)
    r_i = lax.broadcasted_iota(jnp.int32, (R, MLA_HEADS * SLAB), 0)
    c_i = lax.broadcasted_iota(jnp.int32, (R, MLA_HEADS * SLAB), 1)
    pick = jnp.where(c_i == (r_i // q_len) * SLAB + mine * q_len + r_i % q_len, 1.0, 0.0).astype(BF16)
    ql = _dot(pick, ql).astype(BF16)
    qr = _dot(pick, qr).astype(BF16)

    for cp in page_copies(b, slot):
        cp.wait()

    ppc = tk // page

    def chunk(j, carry):
        m, l, acc = carry
        kc = ckv_buf[slot, pl.ds(j * tk, tk), :].astype(BF16)
        krt = jnp.concatenate([krt_buf[slot, pl.ds((j * ppc + pp) * MLA_ROPE, MLA_ROPE), :] for pp in range(ppc)],
                              axis=1).astype(BF16)
        s = _dot_nt(ql, kc) + _dot(qr[:, :MLA_ROPE], krt)
        m_new = jnp.maximum(m, jnp.max(s, axis=1, keepdims=True))
        alpha = jnp.exp2(m - m_new)
        p = jnp.exp2(s - m_new)
        l = alpha * l + jnp.sum(p, axis=1, keepdims=True)
        acc = alpha * acc + _dot(p.astype(BF16), kc)
        return m_new, l, acc

    init = (jnp.full((R, 1), NEG_INF, F32), jnp.zeros((R, 1), F32), jnp.zeros((R, LANES), F32))
    m, l, acc = lax.fori_loop(0, P // tk, chunk, init)

    kn = cn_ref[...].astype(BF16)
    s = _dot_nt(ql, kn) + _dot_nt(qr, kn_ref[...].astype(BF16))
    row_t = lax.broadcasted_iota(jnp.int32, (R, SLAB), 0)
    col_t = lax.broadcasted_iota(jnp.int32, (R, SLAB), 1)
    ok = (col_t // q_len == mine) & (col_t % q_len <= row_t % q_len)
    s = jnp.where(ok, s, NEG_INF)
    m_new = jnp.maximum(m, jnp.max(s, axis=1, keepdims=True))
    alpha = jnp.exp2(m - m_new)
    p = jnp.exp2(s - m_new)
    l = alpha * l + jnp.sum(p, axis=1, keepdims=True)
    acc = alpha * acc + _dot(p.astype(BF16), kn)
    o_lat = (acc / l).astype(BF16)

    @pl.when(mine == 0)
    def _():
        o_scr[...] = jnp.zeros(o_scr.shape, F32)

    t_i = lax.broadcasted_iota(jnp.int32, (SLAB, R), 0)
    r_o = lax.broadcasted_iota(jnp.int32, (SLAB, R), 1)
    for h in range(MLA_HEADS):
        cols = slice(h * LANES, (h + 1) * LANES)
        o_h = _dot(o_lat, wuv_ref[h]).astype(BF16)
        place = jnp.where(r_o == h * q_len + t_i - mine * q_len, 1.0, 0.0)
        place = jnp.where(t_i // q_len == mine, place, 0.0).astype(BF16)
        o_scr[:, cols] = o_scr[:, cols] + _dot(place, o_h)
    o_ref[...] = o_scr[...].astype(o_ref.dtype)


def _attn_s(page_table, qh, ckv_new, krp_new, cache_ckv, cache_krope, w_uk, w_uv, *, tk):
    DB, n_pages = page_table.shape
    page = cache_ckv.shape[1]
    q_len = qh.shape[0] // DB
    hp = MLA_HEADS * LANES
    per_slab = SLAB // q_len
    wabs = jnp.transpose(w_uk.reshape(MLA_KV_LORA, MLA_HEADS, MLA_NOPE), (1, 2, 0))
    wabs = jnp.pad(wabs, ((0, 0), (0, LANES - MLA_NOPE), (0, 0))).astype(BF16)
    idx = jnp.arange(MLA_ROPE)
    wsel = jnp.zeros((LANES, LANES), F32).at[MLA_NOPE + idx, idx].set(1.0).astype(BF16)
    wuv = jnp.pad(jnp.transpose(w_uv.reshape(MLA_KV_LORA, MLA_HEADS, MLA_V), (1, 0, 2)),
                  ((0, 0), (0, 0), (0, LANES - MLA_V))).astype(BF16)
    P = n_pages * page
    assert tk % page == 0 and P % tk == 0 and page == LANES
    krt = jnp.swapaxes(cache_krope, 1, 2)

    def slab(b, pt):
        return (b // per_slab, 0)

    grid_spec = pltpu.PrefetchScalarGridSpec(
        num_scalar_prefetch=1,
        grid=(DB,),
        in_specs=[pl.BlockSpec((SLAB, hp), slab),
                  pl.BlockSpec((SLAB, MLA_KV_LORA), slab),
                  pl.BlockSpec((SLAB, LANES), slab),
                  pl.BlockSpec(wabs.shape, lambda b, pt: (0, 0, 0)),
                  pl.BlockSpec(wsel.shape, lambda b, pt: (0, 0)),
                  pl.BlockSpec(wuv.shape, lambda b, pt: (0, 0, 0)),
                  pl.BlockSpec(memory_space=pl.ANY),
                  pl.BlockSpec(memory_space=pl.ANY)],
        out_specs=pl.BlockSpec((SLAB, hp), slab),
        scratch_shapes=[pltpu.VMEM((2, P, MLA_KV_LORA), F32),
                        pltpu.VMEM((2, n_pages * MLA_ROPE, page), F32),
                        pltpu.SemaphoreType.DMA((2, 2)),
                        pltpu.VMEM((SLAB, hp), F32)],
    )
    return pl.pallas_call(
        functools.partial(_attn_s_kernel, n_pages=n_pages, page=page, q_len=q_len, tk=tk),
        grid_spec=grid_spec,
        out_shape=jax.ShapeDtypeStruct((DB * q_len, hp), BF16),
        compiler_params=_cparams(("arbitrary",)),
        name="attn_s",
    )(page_table, qh, ckv_new, krp_new, wabs, wsel, wuv, cache_ckv, krt)


ROUTE_ROWS = 8


def _first_index(hit, idx, big):
    return jnp.min(jnp.where(hit, idx, big), axis=0, keepdims=True)


def _route(scores, sel):
    tm = scores.shape[1]
    gsz = N_EXPERTS // N_GROUPS
    sub = lax.broadcasted_iota(jnp.int32, (gsz, tm), 0)
    grp_rows = lax.broadcasted_iota(jnp.int32, (N_GROUPS, tm), 0)
    groups = [sel[g * gsz:(g + 1) * gsz, :] for g in range(N_GROUPS)]
    gscore = jnp.zeros((N_GROUPS, tm), F32)
    for g, x in enumerate(groups):
        m1 = jnp.max(x, axis=0, keepdims=True)
        first = _first_index(x == m1, sub, gsz)
        m2 = jnp.max(jnp.where(sub == first, NEG_INF, x), axis=0, keepdims=True)
        gscore = jnp.where(grp_rows == g, m1 + m2, gscore)
    chosen = jnp.zeros((N_GROUPS, tm), jnp.bool_)
    y = gscore
    for _ in range(TOPK_GROUPS):
        m = jnp.max(y, axis=0, keepdims=True)
        hit = grp_rows == _first_index(y == m, grp_rows, N_GROUPS)
        chosen = chosen | hit
        y = jnp.where(hit, NEG_INF, y)
    cand = [jnp.where(chosen[g:g + 1, :], x, NEG_INF) for g, x in enumerate(groups)]
    eids = [sub + g * gsz for g in range(N_GROUPS)]
    out_rows = lax.broadcasted_iota(jnp.int32, (ROUTE_ROWS, tm), 0)
    eidx = jnp.zeros((ROUTE_ROWS, tm), jnp.int32)
    wsel = jnp.zeros((ROUTE_ROWS, tm), F32)
    hits = []
    for k in range(TOP_K):
        m = functools.reduce(jnp.maximum, [jnp.max(c, axis=0, keepdims=True) for c in cand])
        first = functools.reduce(jnp.minimum, [_first_index(c == m, e, N_EXPERTS) for c, e in zip(cand, eids)])
        wk = jnp.zeros((1, tm), F32)
        hit_k = []
        for g in range(N_GROUPS):
            hit = eids[g] == first
            hit_k.append(jnp.where(hit, 1.0, 0.0))
            wk = wk + jnp.sum(jnp.where(hit, scores[g * gsz:(g + 1) * gsz, :], 0.0), axis=0, keepdims=True)
            cand[g] = jnp.where(hit, NEG_INF, cand[g])
        hits.append(jnp.concatenate(hit_k, axis=0))
        eidx = jnp.where(out_rows == k, first, eidx)
        wsel = jnp.where(out_rows == k, wk, wsel)
    total = jnp.sum(wsel, axis=0, keepdims=True)
    return eidx, wsel / total * ROUTED_SCALE, hits


def _ranks(hits, run):
    tm = hits[0].shape[1]
    sel = functools.reduce(jnp.add, hits)
    before = (lax.broadcasted_iota(jnp.int32, (tm, tm), 0) < lax.broadcasted_iota(jnp.int32, (tm, tm), 1))
    prefix = _dot(sel.astype(BF16), jnp.where(before, 1.0, 0.0).astype(BF16)) + run
    out_rows = lax.broadcasted_iota(jnp.int32, (ROUTE_ROWS, tm), 0)
    rank = jnp.zeros((ROUTE_ROWS, tm), F32)
    for k, hit in enumerate(hits):
        rank = jnp.where(out_rows == k, jnp.sum(hit * prefix, axis=0, keepdims=True), rank)
    run = run + jnp.sum(sel, axis=1, keepdims=True)
    return rank.astype(jnp.int32), run


def _split_hi_lo(a):
    hi = a.astype(BF16)
    lo = (a - hi.astype(F32)).astype(BF16)
    return hi, lo


def _mix_kernel(x_ref, om_ref, ret_ref, ga_ref, gb_ref, wo_ref, wr_ref, wout_ref, g1_ref, b1_ref,
                wrt_hi_ref, wrt_lo_ref, rb_ref, run0_ref,
                h_ref, hrow_ref, eidx_ref, gw_ref, rank_ref, run_ref):
    @pl.when(pl.program_id(0) == 0)
    def _():
        run_ref[...] = run0_ref[...]

    tm = h_ref.shape[0]
    y_a = _dot(om_ref[...], wo_ref[...])
    y_b = _dot(ret_ref[...], wr_ref[...])
    mixed_in = ga_ref[...].astype(F32) * y_a + gb_ref[...].astype(F32) * y_b
    mixed = _dot(mixed_in.astype(BF16), wout_ref[...])
    h = _layer_norm(DEEPNORM_ALPHA * x_ref[...] + mixed, g1_ref[...], b1_ref[...])
    h_ref[...] = h
    for s in range(h.shape[1] // LANES):
        hrow_ref[pl.ds(s, tm, stride=ROW_TILE), :] = h[:, s * LANES:(s + 1) * LANES]
    h_hi, h_lo = _split_hi_lo(h)
    logits = _dot_nt(wrt_hi_ref[...], h_hi) + (_dot_nt(wrt_hi_ref[...], h_lo) + _dot_nt(wrt_lo_ref[...], h_hi))
    scores = _sigmoid(logits)
    eidx, gw, hits = _route(scores, scores + rb_ref[...])
    rank, run = _ranks(hits, run_ref[...])
    eidx_ref[...] = eidx
    gw_ref[...] = gw
    rank_ref[...] = rank
    run_ref[...] = run


def _mix(x2d, o_mla, ret, ga, gb, w_o, w_ret_o, w_out, ln1_g, ln1_b, w_router, router_bias, run0, *, tm):
    T, D = x2d.shape
    assert D == ROW_TILE * LANES
    wrt = w_router.T
    wrt_hi, wrt_lo = _split_hi_lo(wrt)
    rb = router_bias.reshape(N_EXPERTS, 1).astype(F32)
    g1, b1 = ln1_g.reshape(1, D), ln1_b.reshape(1, D)
    wr = w_ret_o.astype(BF16)
    wout = w_out.astype(BF16)

    def row(i):
        return (i, 0)

    def full(a):
        return pl.BlockSpec(a.shape, lambda i: (0,) * a.ndim)

    route_spec = pl.BlockSpec((ROUTE_ROWS, tm), lambda i: (0, i))
    return pl.pallas_call(
        _mix_kernel,
        grid=(T // tm,),
        in_specs=[pl.BlockSpec((tm, D), row), pl.BlockSpec((tm, o_mla.shape[1]), row),
                  pl.BlockSpec((tm, ret.shape[1]), row), pl.BlockSpec((tm, D), row), pl.BlockSpec((tm, D), row),
                  full(w_o), full(wr), full(wout), full(g1), full(b1), full(wrt_hi), full(wrt_lo), full(rb),
                  full(run0)],
        out_specs=[pl.BlockSpec((tm, D), row), pl.BlockSpec((tm * ROW_TILE, LANES), row),
                   route_spec, route_spec, route_spec, pl.BlockSpec((N_EXPERTS, 1), lambda i: (0, 0))],
        out_shape=[jax.ShapeDtypeStruct((T, D), F32),
                   jax.ShapeDtypeStruct((T * ROW_TILE, LANES), F32),
                   jax.ShapeDtypeStruct((ROUTE_ROWS, T), jnp.int32),
                   jax.ShapeDtypeStruct((ROUTE_ROWS, T), F32),
                   jax.ShapeDtypeStruct((ROUTE_ROWS, T), jnp.int32),
                   jax.ShapeDtypeStruct((N_EXPERTS, 1), F32)],
        compiler_params=_cparams(("arbitrary",)),
        name="mix",
    )(x2d, o_mla, ret, ga, gb, w_o, wr, wout, g1, b1, wrt_hi, wrt_lo, rb, run0)


def _n_row_blocks(n_tokens):
    return (n_tokens * TOP_K + N_EXPERTS * (MOE_BLOCK - 1) + MOE_BLOCK - 1) // MOE_BLOCK


def _block_plan(counts, n_blocks):
    counts = counts.reshape(N_EXPERTS).astype(jnp.int32)
    pad_len = (counts + MOE_BLOCK - 1) // MOE_BLOCK * MOE_BLOCK
    pad_end = jnp.cumsum(pad_len)
    pad_start = pad_end - pad_len
    first_row = jnp.arange(n_blocks, dtype=jnp.int32) * MOE_BLOCK
    blk_e = jnp.minimum(jnp.sum((pad_end[None, :] <= first_row[:, None]).astype(jnp.int32), axis=1), N_EXPERTS - 1)
    n_used = (pad_end[-1:] // MOE_BLOCK).astype(jnp.int32)
    return pad_start.astype(jnp.int32), pad_len.astype(jnp.int32), blk_e, n_used


def _tile_copy(src, src_row, dst, dst_row, sem):
    return pltpu.make_async_copy(src.at[pl.ds(src_row * ROW_TILE, ROW_TILE), :],
                                 dst.at[pl.ds(dst_row * ROW_TILE, ROW_TILE), :], sem)


def _dispatch_kernel(ps_ref, pl_ref, nu_ref, eidx_ref, rank_ref, eidx2_ref, rank2_ref, hrow_ref, hrow2_ref,
                     xs_hbm, zbuf, sem, zsem, *, tm, n_blocks):
    blk_rows = MOE_BLOCK * ROW_TILE

    def scatter(e_ref, r_ref, src_ref, n):
        def start(t, c):
            for k in range(TOP_K):
                dst = ps_ref[e_ref[k, t]] + r_ref[k, t]
                _tile_copy(src_ref, t, xs_hbm, dst, sem).start(priority=k % N_DMA_PRIORITIES)
            return c

        def wait(t, c):
            for k in range(TOP_K):
                _tile_copy(src_ref, t, xs_hbm, 0, sem).wait()
            return c

        lax.fori_loop(0, n, start, 0)
        lax.fori_loop(0, n, wait, 0)

    @pl.when(pl.program_id(0) == 0)
    def _():
        zbuf[...] = jnp.zeros(zbuf.shape, F32)

        def zcopy(block_row):
            return pltpu.make_async_copy(zbuf, xs_hbm.at[pl.ds(block_row * ROW_TILE, blk_rows), :], zsem)

        def each_expert(fn):
            def body(e, c):
                @pl.when(pl_ref[e] > 0)
                def _():
                    fn(zcopy(ps_ref[e] + pl_ref[e] - MOE_BLOCK))
                return c
            lax.fori_loop(0, N_EXPERTS, body, 0)

        def each_tail(fn):
            def body(j, c):
                fn(zcopy(j * MOE_BLOCK))
                return c
            lax.fori_loop(nu_ref[0], n_blocks, body, 0)

        each_expert(lambda cp: cp.start())
        each_tail(lambda cp: cp.start())
        each_expert(lambda cp: cp.wait())
        each_tail(lambda cp: cp.wait())
        scatter(eidx2_ref, rank2_ref, hrow2_ref, eidx2_ref.shape[1])

    scatter(eidx_ref, rank_ref, hrow_ref, tm)


def _dispatch(pad_start, pad_len, n_used, n_blocks, main, second, *, tm):
    eidx, rank, hrow = main
    eidx2, rank2, hrow2 = second
    T = eidx.shape[1]
    smem = pl.BlockSpec(memory_space=pltpu.SMEM)
    route = pl.BlockSpec((ROUTE_ROWS, tm), lambda i: (0, i), memory_space=pltpu.SMEM)
    return pl.pallas_call(
        functools.partial(_dispatch_kernel, tm=tm, n_blocks=n_blocks),
        grid=(T // tm,),
        in_specs=[smem, smem, smem, route, route, smem, smem,
                  pl.BlockSpec((tm * ROW_TILE, LANES), lambda i: (i, 0)),
                  pl.BlockSpec(hrow2.shape, lambda i: (0, 0))],
        out_specs=pl.BlockSpec(memory_space=pl.ANY),
        out_shape=jax.ShapeDtypeStruct((n_blocks * MOE_BLOCK * ROW_TILE, LANES), F32),
        scratch_shapes=[pltpu.VMEM((MOE_BLOCK * ROW_TILE, LANES), F32),
                        pltpu.SemaphoreType.DMA(()), pltpu.SemaphoreType.DMA(())],
        compiler_params=_cparams(("arbitrary",)),
        name="dispatch",
    )(pad_start, pad_len, n_used, eidx, rank, eidx2, rank2, hrow, hrow2)


def _from_row_tiles(ref, n_rows, base=0):
    return jnp.concatenate([ref[pl.ds(base * ROW_TILE + s, n_rows, stride=ROW_TILE), :]
                            for s in range(ROW_TILE)], axis=1)


def _experts_kernel(be_ref, nu_ref, x_ref, wg_ref, wu_ref, wd_ref, y_ref, wgb, wub, wdb):
    j = pl.program_id(0)
    used = j < nu_ref[0]

    @pl.when(used & ((j == 0) | (be_ref[j] != be_ref[jnp.maximum(j - 1, 0)])))
    def _():
        wgb[...] = wg_ref[...].astype(BF16)
        wub[...] = wu_ref[...].astype(BF16)
        wdb[...] = wd_ref[...].astype(BF16)

    @pl.when(used)
    def _():
        x = _from_row_tiles(x_ref, MOE_BLOCK).astype(BF16)
        gate = _dot(x, wgb[...])
        up = _dot(x, wub[...])
        hid = (gate * _sigmoid(gate) * up).astype(BF16)
        y = _dot(hid, wdb[...])
        for s in range(ROW_TILE):
            y_ref[pl.ds(s, MOE_BLOCK, stride=ROW_TILE), :] = y[:, s * LANES:(s + 1) * LANES]

    @pl.when(pl.program_id(0) >= nu_ref[0])
    def _():
        y_ref[...] = jnp.zeros(y_ref.shape, F32)


def _experts(xs, blk_e, n_used, w_gate, w_up, w_down):
    n_blocks = blk_e.shape[0]
    D = w_gate.shape[1]

    def blk(j, be, nu):
        return (jnp.minimum(j, nu[0] - 1), 0)

    def out_blk(j, be, nu):
        return (j, 0)

    def wsel(j, be, nu):
        return (be[jnp.minimum(j, nu[0] - 1)], 0, 0)

    rows = MOE_BLOCK * ROW_TILE
    grid_spec = pltpu.PrefetchScalarGridSpec(
        num_scalar_prefetch=2,
        grid=(n_blocks,),
        in_specs=[pl.BlockSpec((rows, LANES), blk),
                  pl.BlockSpec((None, D, EXPERT_FF), wsel),
                  pl.BlockSpec((None, D, EXPERT_FF), wsel),
                  pl.BlockSpec((None, EXPERT_FF, D), wsel)],
        out_specs=pl.BlockSpec((rows, LANES), out_blk),
        scratch_shapes=[pltpu.VMEM((D, EXPERT_FF), BF16), pltpu.VMEM((D, EXPERT_FF), BF16),
                        pltpu.VMEM((EXPERT_FF, D), BF16)],
    )
    return pl.pallas_call(
        _experts_kernel,
        grid_spec=grid_spec,
        out_shape=jax.ShapeDtypeStruct(xs.shape, F32),
        compiler_params=_cparams(("arbitrary",)),
        name="experts",
    )(blk_e, n_used, xs, w_gate, w_up, w_down)


def _combine_kernel(ps_ref, eidx_ref, rank_ref, h_ref, gw_ref, y_hbm, wsg_ref, wsu_ref, wsd_ref, g2_ref, b2_ref,
                    o_ref, ybuf, sem, *, tm):
    def start(t, c):
        for k in range(TOP_K):
            src = ps_ref[eidx_ref[k, t]] + rank_ref[k, t]
            _tile_copy(y_hbm, src, ybuf, k * tm + t, sem).start(priority=k % N_DMA_PRIORITIES)
        return c

    def wait(t, c):
        for k in range(TOP_K):
            _tile_copy(y_hbm, 0, ybuf, k * tm + t, sem).wait()
        return c

    lax.fori_loop(0, tm, start, 0)
    h = h_ref[...]
    hb = h.astype(BF16)
    gate = _dot(hb, wsg_ref[...])
    up = _dot(hb, wsu_ref[...])
    ffn = _dot((gate * _sigmoid(gate) * up).astype(BF16), wsd_ref[...])
    lax.fori_loop(0, tm, wait, 0)
    gw = gw_ref[...]
    for k in range(TOP_K):
        ffn = ffn + _from_row_tiles(ybuf, tm, base=k * tm) * gw[:, k:k + 1]
    o_ref[...] = _layer_norm(DEEPNORM_ALPHA * h + ffn, g2_ref[...], b2_ref[...])


def _combine(pad_start, eidx, rank, h, gw_t, ys, w_sh_gate, w_sh_up, w_sh_down, ln2_g, ln2_b, *, tm):
    T, D = h.shape
    wsg, wsu, wsd = w_sh_gate.astype(BF16), w_sh_up.astype(BF16), w_sh_down.astype(BF16)
    g2, b2 = ln2_g.reshape(1, D), ln2_b.reshape(1, D)

    def full(a):
        return pl.BlockSpec(a.shape, lambda i: (0,) * a.ndim)

    route = pl.BlockSpec((ROUTE_ROWS, tm), lambda i: (0, i), memory_space=pltpu.SMEM)
    return pl.pallas_call(
        functools.partial(_combine_kernel, tm=tm),
        grid=(T // tm,),
        in_specs=[pl.BlockSpec(memory_space=pltpu.SMEM), route, route,
                  pl.BlockSpec((tm, D), lambda i: (i, 0)),
                  pl.BlockSpec((tm, ROUTE_ROWS), lambda i: (i, 0)),
                  pl.BlockSpec(memory_space=pl.ANY),
                  full(wsg), full(wsu), full(wsd), full(g2), full(b2)],
        out_specs=pl.BlockSpec((tm, D), lambda i: (i, 0)),
        out_shape=jax.ShapeDtypeStruct((T, D), F32),
        scratch_shapes=[pltpu.VMEM((TOP_K * tm * ROW_TILE, LANES), F32), pltpu.SemaphoreType.DMA(())],
        compiler_params=_cparams(("arbitrary",)),
        name="combine",
    )(pad_start, eidx, rank, h, gw_t, ys, wsg, wsu, wsd, g2, b2)


def kernel(x_prompt, x_sample, cache_ckv, cache_krope, state_ret, page_table, w_in, q_norm_g, w_q_up, kv_norm_g,
           w_uk, w_uv, ret_gn_g, w_mla_o, w_ret_o, w_out, ln1_g, ln1_b, w_router, router_bias,
           w_exp_gate, w_exp_up, w_exp_down, w_sh_gate, w_sh_up, w_sh_down, ln2_g, ln2_b):
    B, S, D = x_prompt.shape
    DB, Q, _ = x_sample.shape
    Tp, Ts = B * S, DB * Q
    past_len = page_table.shape[1] * cache_ckv.shape[1]
    w_small, w_ret, w_gate, w_q, w_k, w_v, w_o = _prep_weights(w_in, w_q_up, w_uk, w_uv, w_mla_o)
    pw = (w_small, w_ret, w_gate, w_q, w_k, w_v)

    tab_p = _rope_tables(jnp.arange(S))
    (qh, ckv_p, kr_p, _, rq, rk, rv, rg, ga, gb, kh, vh) = _proj(
        x_prompt.reshape(Tp, D), tab_p, pw, q_norm_g, kv_norm_g, tm=512, ret_dtype=BF16)
    o_mla = _attn_p(qh, kh, vh, B, S, tq=512, tk=512, hps=4)
    ret, ret_state_p = _ret_p(rq, rk, rv, rg, ret_gn_g, B, S, cb=4)
    mix_w = (w_o, w_ret_o, w_out, ln1_g, ln1_b, w_router, router_bias)
    h_p, hrow_p, eidx_p, gw_p, rank_p, cnt_p = _mix(x_prompt.reshape(Tp, D), o_mla, ret, ga, gb, *mix_w,
                                                      jnp.zeros((N_EXPERTS, 1), F32), tm=256)

    tab_s = _rope_tables(jnp.tile(past_len + jnp.arange(Q), DB))
    (qh_s, ckv_s, kr_s, krp_s, rq_s, rk_s, rv_s, rg_s, ga_s, gb_s, _, _) = _proj(
        x_sample.reshape(Ts, D), tab_s, pw, q_norm_g, kv_norm_g, tm=Ts, ret_dtype=F32)
    o_mla_s = _attn_s(page_table, qh_s, ckv_s, krp_s, cache_ckv, cache_krope, w_uk, w_uv, tk=min(1024, past_len))
    ret_s, ret_state_s = _ret_s(rq_s, rk_s, rv_s, rg_s, state_ret, ret_gn_g, gb=min(16, DB))
    h_s, hrow_s, eidx_s, gw_s, rank_s, cnt = _mix(x_sample.reshape(Ts, D), o_mla_s, ret_s, ga_s, gb_s, *mix_w,
                                                  cnt_p, tm=min(256, Ts))

    n_blocks = _n_row_blocks(Tp + Ts)
    pad_start, pad_len, blk_e, n_used = _block_plan(cnt, n_blocks)
    xs = _dispatch(pad_start, pad_len, n_used, n_blocks, (eidx_p, rank_p, hrow_p), (eidx_s, rank_s, hrow_s), tm=256)
    ys = _experts(xs, blk_e, n_used, w_exp_gate, w_exp_up, w_exp_down)
    tmc = 128
    shared = (w_sh_gate, w_sh_up, w_sh_down, ln2_g, ln2_b)
    y_p = _combine(pad_start, eidx_p, rank_p, h_p, gw_p.T, ys, *shared, tm=tmc)
    y_s = _combine(pad_start, eidx_s, rank_s, h_s, gw_s.T, ys, *shared, tm=tmc)

    return (y_p.reshape(B, S, D), y_s.reshape(DB, Q, D),
            ckv_p.reshape(B, S, -1), kr_p.reshape(B, S, -1), ret_state_p,
            ckv_s.reshape(DB, Q, -1), kr_s.reshape(DB, Q, -1), ret_state_s)
```

```python
import functools
import math

import numpy as np
import jax
import jax.numpy as jnp
from jax import lax
from jax.experimental import pallas as pl
from jax.experimental.pallas import tpu as pltpu

F32 = jnp.float32
BF16 = jnp.bfloat16

MLA_HEADS = 8
MLA_NOPE = 64
MLA_ROPE = 32
MLA_V = 64
MLA_Q_LORA = 256
MLA_KV_LORA = 128
MLA_SCALE = (MLA_NOPE + MLA_ROPE) ** -0.5
Q_SCALE = MLA_SCALE * math.log2(math.e)
RET_HEADS = 4
RET_DK = 128
RET_DV = 128
RET_CHUNK = 128
N_EXPERTS = 64
N_GROUPS = 8
TOPK_GROUPS = 4
TOP_K = 6
EXPERT_FF = 256
SHARED_FF = 256
ROUTED_SCALE = 2.5
MOE_BLOCK = 256
ROPE_BASE = 10000.0
LN_EPS = 1e-5
RMS_EPS = 1e-6
DEPTH = 1
DEEPNORM_ALPHA = (2 * DEPTH) ** 0.25

LANES = 128
ROW_TILE = 8
N_DMA_PRIORITIES = 2
HALF_ROPE = MLA_ROPE // 2
VMEM_LIMIT = 56 * 1024 * 1024
NEG_INF = float("-inf")


def _cparams(sem):
    return pltpu.CompilerParams(dimension_semantics=sem, vmem_limit_bytes=VMEM_LIMIT)


def _dot(a, b):
    return jnp.dot(a, b, preferred_element_type=F32)


def _dot_nt(a, b):
    return lax.dot_general(a, b, (((1,), (1,)), ((), ())), preferred_element_type=F32)


def _dot_tn(a, b):
    return lax.dot_general(a, b, (((0,), (0,)), ((), ())), preferred_element_type=F32)


N_TAB = 8


def _rope_tables(pos):
    L = pos.shape[0]
    posf = pos.astype(F32)[:, None]
    half_r = RET_DK // 2
    ang_r = posf * (ROPE_BASE ** (-jnp.arange(half_r, dtype=F32) / half_r))[None, :]
    cos_r = jnp.concatenate([jnp.cos(ang_r), jnp.cos(ang_r)], axis=1)
    sin_r = jnp.concatenate([-jnp.sin(ang_r), jnp.sin(ang_r)], axis=1)
    ang_m = posf * (ROPE_BASE ** (-jnp.arange(HALF_ROPE, dtype=F32) / HALF_ROPE))[None, :]
    c, s = jnp.cos(ang_m), jnp.sin(ang_m)
    z16 = jnp.zeros((L, HALF_ROPE), F32)

    def place(parts, offset):
        body = jnp.concatenate(parts, axis=1)
        return jnp.concatenate([jnp.zeros((L, offset), F32), body,
                                jnp.zeros((L, LANES - offset - body.shape[1]), F32)], axis=1)

    cos_k = place([c, c], 0)
    sinp_k = place([z16, s], 0)
    sinm_k = place([-s, z16], 0)
    ones = jnp.ones((L, MLA_NOPE), F32)
    cos_q = jnp.concatenate([ones, c, c, jnp.zeros((L, LANES - MLA_NOPE - MLA_ROPE), F32)], axis=1) * Q_SCALE
    sinp_q = place([z16, s], MLA_NOPE) * Q_SCALE
    sinm_q = place([-s, z16], MLA_NOPE) * Q_SCALE
    return jnp.concatenate([cos_r, sin_r, cos_k, sinp_k, sinm_k, cos_q, sinp_q, sinm_q], axis=1)


def _prep_weights(w_in, w_q_up, w_uk, w_uv, w_mla_o):
    d = w_in.shape[0]
    c_q, c_kv, c_kr = MLA_Q_LORA, MLA_KV_LORA, MLA_ROPE
    o_ret = c_q + c_kv + c_kr
    n_ret = 2 * RET_HEADS * RET_DK + 2 * RET_HEADS * RET_DV
    w_small = jnp.concatenate([w_in[:, :o_ret], jnp.zeros((d, LANES - c_kr), F32)], axis=1).astype(BF16)
    w_ret = w_in[:, o_ret:o_ret + n_ret].astype(BF16)
    w_gate = w_in[:, o_ret + n_ret:].astype(BF16)
    hd = MLA_NOPE + MLA_ROPE
    w_q = jnp.pad(w_q_up.reshape(c_q, MLA_HEADS, hd), ((0, 0), (0, 0), (0, LANES - hd)))
    w_q = w_q.reshape(c_q, MLA_HEADS * LANES).astype(BF16)
    wk_top = jnp.pad(w_uk.reshape(c_kv, MLA_HEADS, MLA_NOPE), ((0, 0), (0, 0), (0, LANES - MLA_NOPE)))
    place = jnp.zeros((LANES, MLA_HEADS, LANES), F32)
    idx = jnp.arange(MLA_ROPE)
    place = place.at[idx, :, MLA_NOPE + idx].set(1.0)
    w_k = jnp.concatenate([wk_top.reshape(c_kv, -1), place.reshape(LANES, -1)], axis=0).astype(BF16)
    w_v = jnp.pad(w_uv.reshape(c_kv, MLA_HEADS, MLA_V), ((0, 0), (0, 0), (0, LANES - MLA_V)))
    w_v = w_v.reshape(c_kv, MLA_HEADS * LANES).astype(BF16)
    w_o = jnp.pad(w_mla_o.reshape(MLA_HEADS, MLA_V, -1), ((0, 0), (0, LANES - MLA_V), (0, 0)))
    w_o = w_o.reshape(MLA_HEADS * LANES, -1).astype(BF16)
    return w_small, w_ret, w_gate, w_q, w_k, w_v, w_o


def _rms_norm(x, g):
    inv = lax.rsqrt(jnp.mean(x * x, axis=-1, keepdims=True) + RMS_EPS)
    return x * inv * g


def _layer_norm(x, g, b):
    mu = jnp.mean(x, axis=-1, keepdims=True)
    xc = x - mu
    var = jnp.mean(xc * xc, axis=-1, keepdims=True)
    return xc * lax.rsqrt(var + LN_EPS) * g + b


def _sigmoid(x):
    return 1.0 / (1.0 + jnp.exp(-x))


def _proj_kernel(x_ref, tab_ref, wsm_ref, wret_ref, wg_ref, qg_ref, wq_ref, kvg_ref, wa_ref, wb_ref,
                 qh_ref, ckv_ref, kr_ref, krp_ref, rq_ref, rk_ref, rv_ref, rg_ref, ga_ref, gb_ref,
                 oa_ref, ob_ref, *, decode):
    xb = x_ref[...].astype(BF16)

    def tab(i):
        return tab_ref[:, i * LANES:(i + 1) * LANES]

    small = _dot(xb, wsm_ref[...])
    cq = small[:, :MLA_Q_LORA]
    ckv = small[:, MLA_Q_LORA:MLA_Q_LORA + MLA_KV_LORA]
    krb = small[:, MLA_Q_LORA + MLA_KV_LORA:]
    q = _dot(_rms_norm(cq, qg_ref[...]).astype(BF16), wq_ref[...])
    cos_q, sinp_q, sinm_q = tab(5), tab(6), tab(7)
    for h in range(MLA_HEADS):
        blk = q[:, h * LANES:(h + 1) * LANES]
        rot = (blk * cos_q + pltpu.roll(blk, HALF_ROPE, 1) * sinp_q
               + pltpu.roll(blk, LANES - HALF_ROPE, 1) * sinm_q)
        rot = rot.astype(BF16)
        qh_ref[:, h * LANES:(h + 1) * LANES] = rot
        if decode:
            oa_ref[:, h * LANES:(h + 1) * LANES] = _dot(rot, wa_ref[h]).astype(BF16)
            ob_ref[:, h * LANES:(h + 1) * LANES] = _dot(rot, wb_ref[...]).astype(BF16)
    ckvn = _rms_norm(ckv, kvg_ref[...])
    ckv_ref[...] = ckvn
    krr = (krb * tab(2) + pltpu.roll(krb, HALF_ROPE, 1) * tab(3)
           + pltpu.roll(krb, LANES - HALF_ROPE, 1) * tab(4))
    kr_ref[...] = krr[:, :MLA_ROPE]
    krp_ref[...] = krr
    if not decode:
        kcat = jnp.concatenate([ckvn, krr], axis=1).astype(BF16)
        oa_ref[...] = _dot(kcat, wa_ref[...]).astype(BF16)
        vv = _dot(kcat[:, :MLA_KV_LORA], wb_ref[...])
        lane = lax.broadcasted_iota(jnp.int32, vv.shape, 1) % LANES
        ob_ref[...] = jnp.where(lane == MLA_V, 1.0, vv).astype(BF16)

    r = _dot(xb, wret_ref[...])
    cos_r, sin_r = tab(0), tab(1)
    nq = RET_HEADS * RET_DK
    for h in range(RET_HEADS):
        sl = slice(h * RET_DK, (h + 1) * RET_DK)
        a = r[:, sl]
        rq_ref[:, sl] = (a * cos_r + pltpu.roll(a, RET_DK // 2, 1) * sin_r).astype(rq_ref.dtype)
        b = r[:, nq + h * RET_DK:nq + (h + 1) * RET_DK]
        rk_ref[:, sl] = ((b * cos_r + pltpu.roll(b, RET_DK // 2, 1) * sin_r)
                         * (RET_DK ** -0.5)).astype(rk_ref.dtype)
    rv_ref[...] = r[:, 2 * nq:2 * nq + RET_HEADS * RET_DV].astype(rv_ref.dtype)
    rg = r[:, 2 * nq + RET_HEADS * RET_DV:]
    rg_ref[...] = (rg * _sigmoid(rg)).astype(rg_ref.dtype)

    g = _dot(xb, wg_ref[...])
    d = ga_ref.shape[1]
    ga_ref[...] = _sigmoid(g[:, :d]).astype(ga_ref.dtype)
    gb_ref[...] = _sigmoid(g[:, d:]).astype(gb_ref.dtype)


def _proj(x2d, tab, weights, q_norm_g, kv_norm_g, *, tm, ret_dtype, decode):
    T, D = x2d.shape
    w_small, w_ret, w_gate, w_q, w_a, w_b = weights
    n_tab = tab.shape[0] // tm
    nr = RET_HEADS * RET_DK
    hp = MLA_HEADS * LANES

    def row(i):
        return (i, 0)

    def full(a):
        return pl.BlockSpec(a.shape, lambda i: (0,) * a.ndim)

    out_shapes = [
        jax.ShapeDtypeStruct((T, hp), BF16),
        jax.ShapeDtypeStruct((T, MLA_KV_LORA), F32),
        jax.ShapeDtypeStruct((T, MLA_ROPE), F32),
        jax.ShapeDtypeStruct((T, LANES), F32),
        jax.ShapeDtypeStruct((T, nr), ret_dtype),
        jax.ShapeDtypeStruct((T, nr), ret_dtype),
        jax.ShapeDtypeStruct((T, nr), ret_dtype),
        jax.ShapeDtypeStruct((T, nr), BF16),
        jax.ShapeDtypeStruct((T, D), BF16),
        jax.ShapeDtypeStruct((T, D), BF16),
        jax.ShapeDtypeStruct((T, hp), BF16),
        jax.ShapeDtypeStruct((T, hp), BF16),
    ]
    out_specs = [pl.BlockSpec((tm, s.shape[1]), row) for s in out_shapes]
    qg = q_norm_g.reshape(1, -1)
    kvg = kv_norm_g.reshape(1, -1)
    in_specs = [pl.BlockSpec((tm, D), row),
                pl.BlockSpec((tm, N_TAB * LANES), lambda i: (i % n_tab, 0)),
                full(w_small), full(w_ret), full(w_gate), full(qg), full(w_q), full(kvg), full(w_a), full(w_b)]
    return pl.pallas_call(
        functools.partial(_proj_kernel, decode=decode),
        grid=(T // tm,),
        in_specs=in_specs,
        out_specs=out_specs,
        out_shape=out_shapes,
        compiler_params=_cparams(("parallel",)),
        name="proj",
    )(x2d, tab, w_small, w_ret, w_gate, qg, w_q, kvg, w_a, w_b)


def _attn_p_kernel(q_ref, k_ref, v_ref, o_ref, m_ref, acc_ref, *, tq, tk, hps):
    i = pl.program_id(2)
    m_ref[...] = jnp.full(m_ref.shape, NEG_INF, F32)
    acc_ref[...] = jnp.zeros(acc_ref.shape, F32)

    def step(j, masked):
        for hh in range(hps):
            cols = slice(hh * LANES, (hh + 1) * LANES)
            k = k_ref[pl.ds(j * tk, tk), cols]
            v = v_ref[pl.ds(j * tk, tk), cols]
            s = _dot_nt(q_ref[:, cols], k)
            if masked:
                row = lax.broadcasted_iota(jnp.int32, (tq, tk), 0) + i * tq
                col = lax.broadcasted_iota(jnp.int32, (tq, tk), 1) + j * tk
                s = jnp.where(col <= row, s, NEG_INF)
            m_prev = m_ref[hh]
            m_new = jnp.maximum(m_prev, jnp.max(s, axis=1, keepdims=True))
            p = jnp.concatenate([jnp.exp2(s[:, c * LANES:(c + 1) * LANES] - m_new)
                                 for c in range(tk // LANES)], axis=1)
            acc_ref[hh] = jnp.exp2(m_prev - m_new) * acc_ref[hh] + _dot(p.astype(BF16), v)
            m_ref[hh] = m_new

    n_full = (i * tq) // tk

    def body(j, c):
        step(j, False)
        return c

    lax.fori_loop(0, n_full, body, 0)
    for jj in range(tq // tk):
        step(n_full + jj, True)
    for hh in range(hps):
        acc = acc_ref[hh]
        o_ref[:, hh * LANES:(hh + 1) * LANES] = (acc / acc[:, MLA_V:MLA_V + 1]).astype(o_ref.dtype)


def _attn_p(qh, kh, vh, B, S, *, tq, tk, hps):
    assert tq % tk == 0 and MLA_HEADS % hps == 0
    nq = S // tq
    hp = MLA_HEADS * LANES
    kh3 = kh.reshape(B, S, hp)
    vh3 = vh.reshape(B, S, hp)
    return pl.pallas_call(
        functools.partial(_attn_p_kernel, tq=tq, tk=tk, hps=hps),
        grid=(B, MLA_HEADS // hps, nq),
        in_specs=[pl.BlockSpec((tq, hps * LANES), lambda b, h, i: (b * nq + i, h)),
                  pl.BlockSpec((None, S, hps * LANES), lambda b, h, i: (b, 0, h)),
                  pl.BlockSpec((None, S, hps * LANES), lambda b, h, i: (b, 0, h))],
        out_specs=pl.BlockSpec((tq, hps * LANES), lambda b, h, i: (b * nq + i, h)),
        out_shape=jax.ShapeDtypeStruct((B * S, hp), BF16),
        scratch_shapes=[pltpu.VMEM((hps, tq, LANES), F32), pltpu.VMEM((hps, tq, LANES), F32)],
        compiler_params=_cparams(("parallel", "parallel", "arbitrary")),
        name="attn_p",
    )(qh, kh3, vh3)


def _ret_consts(C):
    lg = jnp.log(1.0 - 2.0 ** (-5.0 - jnp.arange(RET_HEADS, dtype=F32)))
    idx = jnp.arange(C, dtype=F32)
    diff = idx[:, None] - idx[None, :]
    dmask = jnp.where(diff >= 0, jnp.exp(jnp.maximum(diff, 0.0)[None] * lg[:, None, None]), 0.0)
    q_dec = jnp.exp((idx[None, :] + 1.0) * lg[:, None])[:, :, None]
    k_dec = jnp.exp((C - 1.0 - idx)[None, :] * lg[:, None])[:, :, None]
    s_dec = jnp.exp(C * lg)
    return dmask, q_dec, k_dec, s_dec


def _head_norm_gate(o, gate, gn):
    mu = jnp.mean(o, axis=-1, keepdims=True)
    oc = o - mu
    var = jnp.mean(oc * oc, axis=-1, keepdims=True)
    return gate * (oc * lax.rsqrt(var + LN_EPS) * gn)


def _ret_p_kernel(sdec_ref, q_ref, k_ref, v_ref, g_ref, dm_ref, qd_ref, kd_ref, gn_ref,
                  o_ref, s_ref, *, nb, cb):
    C = RET_CHUNK

    @pl.when(pl.program_id(0) == 0)
    def _():
        s_ref[...] = jnp.zeros(s_ref.shape, F32)

    for c in range(cb):
        rows = slice(c * C, (c + 1) * C)
        for b in range(nb):
            for h in range(RET_HEADS):
                cols = slice(h * RET_DK, (h + 1) * RET_DK)
                q = q_ref[b, rows, cols]
                k = k_ref[b, rows, cols]
                v = v_ref[b, rows, cols]
                state = s_ref[b, h]
                att = _dot_nt(q, k) * dm_ref[h]
                o = _dot(att.astype(BF16), v) + _dot(q, state.astype(BF16)) * qd_ref[h]
                kd = (k.astype(F32) * kd_ref[h]).astype(BF16)
                s_ref[b, h] = state * sdec_ref[h] + _dot_tn(kd, v)
                gate = g_ref[b, rows, cols].astype(F32)
                o_ref[b, rows, cols] = _head_norm_gate(o, gate, gn_ref[:, cols]).astype(o_ref.dtype)


def _ret_p(rq, rk, rv, rg, ret_gn_g, B, S, *, cb):
    C = RET_CHUNK
    nr = RET_HEADS * RET_DK
    dmask, q_dec, k_dec, s_dec = _ret_consts(C)
    blk = pl.BlockSpec((B, cb * C, nr), lambda g: (0, g, 0))

    def full(a):
        return pl.BlockSpec(a.shape, lambda g: (0,) * a.ndim)

    gn = ret_gn_g.reshape(1, nr)
    args = [a.reshape(B, S, nr) for a in (rq, rk, rv, rg)]
    ret, state = pl.pallas_call(
        functools.partial(_ret_p_kernel, nb=B, cb=cb),
        grid=(S // (cb * C),),
        in_specs=[pl.BlockSpec(memory_space=pltpu.SMEM), blk, blk, blk, blk,
                  full(dmask), full(q_dec), full(k_dec), full(gn)],
        out_specs=[blk, pl.BlockSpec((B, RET_HEADS, RET_DK, RET_DV), lambda g: (0, 0, 0, 0))],
        out_shape=[jax.ShapeDtypeStruct((B, S, nr), BF16),
                   jax.ShapeDtypeStruct((B, RET_HEADS, RET_DK, RET_DV), F32)],
        compiler_params=_cparams(("arbitrary",)),
        name="ret_p",
    )(s_dec, *args, dmask, q_dec, k_dec, gn)
    return ret.reshape(B * S, nr), state


def _ret_s_kernel(sdec_ref, q_ref, k_ref, v_ref, g_ref, s0_ref, dm_ref, qd_ref, kd_ref, gn_ref,
                  o_ref, s_ref, *, gb, q_len):
    rows = gb * q_len
    row_b = lax.broadcasted_iota(jnp.int32, (rows, RET_DV), 0) // q_len
    for h in range(RET_HEADS):
        cols = slice(h * RET_DK, (h + 1) * RET_DK)
        q = q_ref[:, cols].astype(BF16)
        k = k_ref[:, cols]
        v = v_ref[:, cols].astype(BF16)
        att = _dot_nt(q, k.astype(BF16)) * dm_ref[h]
        o = _dot(att.astype(BF16), v)
        kd = k * kd_ref[h]
        inter = jnp.zeros((rows, RET_DV), F32)
        for b in range(gb):
            state = s0_ref[b, h]
            inter = jnp.where(row_b == b, _dot(q, state.astype(BF16)), inter)
            kd_b = jnp.where(row_b == b, kd, 0.0).astype(BF16)
            s_ref[b, h] = state * sdec_ref[h] + _dot_tn(kd_b, v)
        o = o + inter * qd_ref[h]
        gate = g_ref[:, cols].astype(F32)
        o_ref[:, cols] = _head_norm_gate(o, gate, gn_ref[:, cols]).astype(o_ref.dtype)


def _ret_s(rq, rk, rv, rg, state, ret_gn_g, *, gb):
    DB = state.shape[0]
    q_len = rq.shape[0] // DB
    nr = RET_HEADS * RET_DK
    rows = gb * q_len
    dmask, q_dec, k_dec, s_dec = _ret_consts(q_len)
    same = (jnp.arange(rows)[:, None] // q_len) == (jnp.arange(rows)[None, :] // q_len)
    dm = jnp.where(same[None], jnp.tile(dmask, (1, gb, gb)), 0.0)
    qd = jnp.tile(q_dec, (1, gb, 1))
    kd = jnp.tile(k_dec, (1, gb, 1))
    gn = ret_gn_g.reshape(1, nr)
    blk = pl.BlockSpec((rows, nr), lambda g: (g, 0))
    sblk = pl.BlockSpec((gb, RET_HEADS, RET_DK, RET_DV), lambda g: (g, 0, 0, 0))

    def full(a):
        return pl.BlockSpec(a.shape, lambda g: (0,) * a.ndim)

    return pl.pallas_call(
        functools.partial(_ret_s_kernel, gb=gb, q_len=q_len),
        grid=(DB // gb,),
        in_specs=[pl.BlockSpec(memory_space=pltpu.SMEM), blk, blk, blk, blk, sblk,
                  full(dm), full(qd), full(kd), full(gn)],
        out_specs=[blk, sblk],
        out_shape=[jax.ShapeDtypeStruct((DB * q_len, nr), BF16),
                   jax.ShapeDtypeStruct(state.shape, F32)],
        compiler_params=_cparams(("parallel",)),
        name="ret_s",
    )(s_dec, rq, rk, rv, rg, state, dm, qd, kd, gn)


SLAB = 16


def _attn_s_kernel(pt_ref, ql_ref, qr_ref, cn_ref, kn_ref, ckv_hbm, krt_hbm, o_ref, ckv_buf, krt_buf, sem,
                   *, n_pages, page, q_len, tk):
    b = pl.program_id(0)
    nb = pl.num_programs(0)
    slot = b % 2
    mine = b % (SLAB // q_len)
    P = n_pages * page
    R = MLA_HEADS * q_len

    def page_copies(bi, s):
        out = []
        for p in range(n_pages):
            pg = pt_ref[bi, p]
            out.append(pltpu.make_async_copy(ckv_hbm.at[pg], ckv_buf.at[s, pl.ds(p * page, page), :], sem.at[0, s]))
            out.append(pltpu.make_async_copy(krt_hbm.at[pg], krt_buf.at[s, pl.ds(p * MLA_ROPE, MLA_ROPE), :],
                                             sem.at[1, s]))
        return out

    @pl.when(b == 0)
    def _():
        for cp in page_copies(0, 0):
            cp.start()

    @pl.when(b + 1 < nb)
    def _():
        for cp in page_copies(b + 1, 1 - slot):
            cp.start()

    ql = ql_ref[...]
    qr = qr_ref[:, :MLA_ROPE]
    for cp in page_copies(b, slot):
        cp.wait()

    ppc = tk // page

    def keys(j):
        return ckv_buf[slot, j * tk:(j + 1) * tk, :].astype(BF16)

    scores = []
    for j in range(P // tk):
        krt = jnp.concatenate([krt_buf[slot, (j * ppc + pp) * MLA_ROPE:(j * ppc + pp + 1) * MLA_ROPE, :]
                               for pp in range(ppc)], axis=1).astype(BF16)
        scores.append(_dot_nt(ql, keys(j)) + _dot(qr, krt))
    kn = cn_ref[...].astype(BF16)
    s_new = _dot_nt(ql, kn) + _dot_nt(qr_ref[...], kn_ref[...].astype(BF16))
    row_t = lax.broadcasted_iota(jnp.int32, (R, SLAB), 0)
    col_t = lax.broadcasted_iota(jnp.int32, (R, SLAB), 1)
    ok = (col_t // q_len == mine) & (col_t % q_len <= row_t % q_len)
    s_new = jnp.where(ok, s_new, NEG_INF)
    m = functools.reduce(jnp.maximum, [jnp.max(s, axis=1, keepdims=True) for s in scores + [s_new]])
    p_new = jnp.exp2(s_new - m)
    l = jnp.sum(p_new, axis=1, keepdims=True)
    acc = _dot(p_new.astype(BF16), kn)
    for j, s in enumerate(scores):
        p = jnp.exp2(s - m)
        l = l + jnp.sum(p, axis=1, keepdims=True)
        acc = acc + _dot(p.astype(BF16), keys(j))
    o_ref[...] = (acc / l).astype(o_ref.dtype)


def _attn_s(page_table, ql, qr, ckv_new, krp_new, cache_ckv, cache_krope, *, tk):
    DB, n_pages = page_table.shape
    page = cache_ckv.shape[1]
    R = ql.shape[1]
    q_len = R // MLA_HEADS
    per_slab = SLAB // q_len
    P = n_pages * page
    assert tk % page == 0 and P % tk == 0 and page == LANES
    krt = jnp.swapaxes(cache_krope, 1, 2)

    def slab(b, pt):
        return (b // per_slab, 0)

    def seq(b, pt):
        return (b, 0, 0)

    grid_spec = pltpu.PrefetchScalarGridSpec(
        num_scalar_prefetch=1,
        grid=(DB,),
        in_specs=[pl.BlockSpec((None, R, LANES), seq),
                  pl.BlockSpec((None, R, LANES), seq),
                  pl.BlockSpec((SLAB, MLA_KV_LORA), slab),
                  pl.BlockSpec((SLAB, LANES), slab),
                  pl.BlockSpec(memory_space=pl.ANY),
                  pl.BlockSpec(memory_space=pl.ANY)],
        out_specs=pl.BlockSpec((None, R, LANES), seq),
        scratch_shapes=[pltpu.VMEM((2, P, MLA_KV_LORA), F32),
                        pltpu.VMEM((2, n_pages * MLA_ROPE, page), F32),
                        pltpu.SemaphoreType.DMA((2, 2))],
    )
    return pl.pallas_call(
        functools.partial(_attn_s_kernel, n_pages=n_pages, page=page, q_len=q_len, tk=tk),
        grid_spec=grid_spec,
        out_shape=jax.ShapeDtypeStruct((DB, R, LANES), BF16),
        compiler_params=_cparams(("arbitrary",)),
        name="attn_s",
    )(page_table, ql, qr, ckv_new, krp_new, cache_ckv, krt)


def _absorb_weights(w_uk, w_uv):
    wabs = jnp.transpose(w_uk.reshape(MLA_KV_LORA, MLA_HEADS, MLA_NOPE), (1, 2, 0))
    wabs = jnp.pad(wabs, ((0, 0), (0, LANES - MLA_NOPE), (0, 0))).astype(BF16)
    idx = jnp.arange(MLA_ROPE)
    wsel = jnp.zeros((LANES, LANES), F32).at[MLA_NOPE + idx, idx].set(1.0).astype(BF16)
    wuv = jnp.pad(jnp.transpose(w_uv.reshape(MLA_KV_LORA, MLA_HEADS, MLA_V), (1, 0, 2)),
                  ((0, 0), (0, 0), (0, LANES - MLA_V))).astype(BF16)
    return wabs, wsel, wuv


def _lat_up_kernel(o_ref, wuv_ref, out_ref):
    for h in range(MLA_HEADS):
        cols = slice(h * LANES, (h + 1) * LANES)
        out_ref[:, cols] = _dot(o_ref[:, cols], wuv_ref[h]).astype(out_ref.dtype)


def _lat_up(o_lat_tok, wuv):
    return pl.pallas_call(
        _lat_up_kernel,
        grid=(1,),
        in_specs=[pl.BlockSpec(o_lat_tok.shape, lambda i: (0, 0)), pl.BlockSpec(wuv.shape, lambda i: (0, 0, 0))],
        out_specs=pl.BlockSpec(o_lat_tok.shape, lambda i: (0, 0)),
        out_shape=jax.ShapeDtypeStruct(o_lat_tok.shape, BF16),
        compiler_params=_cparams(("arbitrary",)),
        name="lat_up",
    )(o_lat_tok, wuv)


ROUTE_ROWS = 8


def _first_index(hit, idx, big):
    return jnp.min(jnp.where(hit, idx, big), axis=0, keepdims=True)


def _route(scores, sel):
    tm = scores.shape[1]
    gsz = N_EXPERTS // N_GROUPS
    sub = lax.broadcasted_iota(jnp.int32, (gsz, tm), 0)
    grp_rows = lax.broadcasted_iota(jnp.int32, (N_GROUPS, tm), 0)
    groups = [sel[g * gsz:(g + 1) * gsz, :] for g in range(N_GROUPS)]
    gscore = jnp.zeros((N_GROUPS, tm), F32)
    for g, x in enumerate(groups):
        m1 = jnp.max(x, axis=0, keepdims=True)
        first = _first_index(x == m1, sub, gsz)
        m2 = jnp.max(jnp.where(sub == first, NEG_INF, x), axis=0, keepdims=True)
        gscore = jnp.where(grp_rows == g, m1 + m2, gscore)
    chosen = jnp.zeros((N_GROUPS, tm), jnp.bool_)
    y = gscore
    for _ in range(TOPK_GROUPS):
        m = jnp.max(y, axis=0, keepdims=True)
        hit = grp_rows == _first_index(y == m, grp_rows, N_GROUPS)
        chosen = chosen | hit
        y = jnp.where(hit, NEG_INF, y)
    cand = [jnp.where(chosen[g:g + 1, :], x, NEG_INF) for g, x in enumerate(groups)]
    eids = [sub + g * gsz for g in range(N_GROUPS)]
    out_rows = lax.broadcasted_iota(jnp.int32, (ROUTE_ROWS, tm), 0)
    eidx = jnp.zeros((ROUTE_ROWS, tm), jnp.int32)
    wsel = jnp.zeros((ROUTE_ROWS, tm), F32)
    hits = []
    for k in range(TOP_K):
        m = functools.reduce(jnp.maximum, [jnp.max(c, axis=0, keepdims=True) for c in cand])
        first = functools.reduce(jnp.minimum, [_first_index(c == m, e, N_EXPERTS) for c, e in zip(cand, eids)])
        wk = jnp.zeros((1, tm), F32)
        hit_k = []
        for g in range(N_GROUPS):
            hit = eids[g] == first
            hit_k.append(jnp.where(hit, 1.0, 0.0))
            wk = wk + jnp.sum(jnp.where(hit, scores[g * gsz:(g + 1) * gsz, :], 0.0), axis=0, keepdims=True)
            cand[g] = jnp.where(hit, NEG_INF, cand[g])
        hits.append(jnp.concatenate(hit_k, axis=0))
        eidx = jnp.where(out_rows == k, first, eidx)
        wsel = jnp.where(out_rows == k, wk, wsel)
    total = jnp.sum(wsel, axis=0, keepdims=True)
    return eidx, wsel / total * ROUTED_SCALE, hits


def _ranks(hits, run):
    tm = hits[0].shape[1]
    sel = functools.reduce(jnp.add, hits)
    before = (lax.broadcasted_iota(jnp.int32, (tm, tm), 0) < lax.broadcasted_iota(jnp.int32, (tm, tm), 1))
    prefix = _dot(sel.astype(BF16), jnp.where(before, 1.0, 0.0).astype(BF16)) + run
    out_rows = lax.broadcasted_iota(jnp.int32, (ROUTE_ROWS, tm), 0)
    rank = jnp.zeros((ROUTE_ROWS, tm), F32)
    for k, hit in enumerate(hits):
        rank = jnp.where(out_rows == k, jnp.sum(hit * prefix, axis=0, keepdims=True), rank)
    run = run + jnp.sum(sel, axis=1, keepdims=True)
    return rank.astype(jnp.int32), run


def _split_hi_lo(a):
    hi = a.astype(BF16)
    lo = (a - hi.astype(F32)).astype(BF16)
    return hi, lo


def _mix_kernel(x_ref, om_ref, ret_ref, ga_ref, gb_ref, wo_ref, wr_ref, wout_ref, g1_ref, b1_ref,
                wrt_hi_ref, wrt_lo_ref, rb_ref, run0_ref,
                h_ref, hrow_ref, eidx_ref, gw_ref, rank_ref, run_ref):
    @pl.when(pl.program_id(0) == 0)
    def _():
        run_ref[...] = run0_ref[...]

    tm = h_ref.shape[0]
    y_a = _dot(om_ref[...], wo_ref[...])
    y_b = _dot(ret_ref[...], wr_ref[...])
    mixed_in = ga_ref[...].astype(F32) * y_a + gb_ref[...].astype(F32) * y_b
    mixed = _dot(mixed_in.astype(BF16), wout_ref[...])
    h = _layer_norm(DEEPNORM_ALPHA * x_ref[...] + mixed, g1_ref[...], b1_ref[...])
    h_ref[...] = h
    for s in range(h.shape[1] // LANES):
        hrow_ref[pl.ds(s, tm, stride=ROW_TILE), :] = h[:, s * LANES:(s + 1) * LANES]
    h_hi, h_lo = _split_hi_lo(h)
    logits = _dot_nt(wrt_hi_ref[...], h_hi) + (_dot_nt(wrt_hi_ref[...], h_lo) + _dot_nt(wrt_lo_ref[...], h_hi))
    scores = _sigmoid(logits)
    eidx, gw, hits = _route(scores, scores + rb_ref[...])
    rank, run = _ranks(hits, run_ref[...])
    eidx_ref[...] = eidx
    gw_ref[...] = gw
    rank_ref[...] = rank
    run_ref[...] = run


def _mix(x2d, o_mla, ret, ga, gb, w_o, w_ret_o, w_out, ln1_g, ln1_b, w_router, router_bias, run0, *, tm):
    T, D = x2d.shape
    assert D == ROW_TILE * LANES
    wrt = w_router.T
    wrt_hi, wrt_lo = _split_hi_lo(wrt)
    rb = router_bias.reshape(N_EXPERTS, 1).astype(F32)
    g1, b1 = ln1_g.reshape(1, D), ln1_b.reshape(1, D)
    wr = w_ret_o.astype(BF16)
    wout = w_out.astype(BF16)

    def row(i):
        return (i, 0)

    def full(a):
        return pl.BlockSpec(a.shape, lambda i: (0,) * a.ndim)

    route_spec = pl.BlockSpec((ROUTE_ROWS, tm), lambda i: (0, i))
    return pl.pallas_call(
        _mix_kernel,
        grid=(T // tm,),
        in_specs=[pl.BlockSpec((tm, D), row), pl.BlockSpec((tm, o_mla.shape[1]), row),
                  pl.BlockSpec((tm, ret.shape[1]), row), pl.BlockSpec((tm, D), row), pl.BlockSpec((tm, D), row),
                  full(w_o), full(wr), full(wout), full(g1), full(b1), full(wrt_hi), full(wrt_lo), full(rb),
                  full(run0)],
        out_specs=[pl.BlockSpec((tm, D), row), pl.BlockSpec((tm * ROW_TILE, LANES), row),
                   route_spec, route_spec, route_spec, pl.BlockSpec((N_EXPERTS, 1), lambda i: (0, 0))],
        out_shape=[jax.ShapeDtypeStruct((T, D), F32),
                   jax.ShapeDtypeStruct((T * ROW_TILE, LANES), F32),
                   jax.ShapeDtypeStruct((ROUTE_ROWS, T), jnp.int32),
                   jax.ShapeDtypeStruct((ROUTE_ROWS, T), F32),
                   jax.ShapeDtypeStruct((ROUTE_ROWS, T), jnp.int32),
                   jax.ShapeDtypeStruct((N_EXPERTS, 1), F32)],
        compiler_params=_cparams(("arbitrary",)),
        name="mix",
    )(x2d, o_mla, ret, ga, gb, w_o, wr, wout, g1, b1, wrt_hi, wrt_lo, rb, run0)


def _n_row_blocks(n_tokens):
    return (n_tokens * TOP_K + N_EXPERTS * (MOE_BLOCK - 1) + MOE_BLOCK - 1) // MOE_BLOCK


def _block_plan(counts, n_blocks):
    counts = counts.reshape(N_EXPERTS).astype(jnp.int32)
    pad_len = (counts + MOE_BLOCK - 1) // MOE_BLOCK * MOE_BLOCK
    pad_end = jnp.cumsum(pad_len)
    pad_start = pad_end - pad_len
    first_row = jnp.arange(n_blocks, dtype=jnp.int32) * MOE_BLOCK
    blk_e = jnp.minimum(jnp.sum((pad_end[None, :] <= first_row[:, None]).astype(jnp.int32), axis=1), N_EXPERTS - 1)
    n_used = (pad_end[-1:] // MOE_BLOCK).astype(jnp.int32)
    return pad_start.astype(jnp.int32), pad_len.astype(jnp.int32), blk_e, n_used


def _dest_rows(pad_start, eidx, rank):
    onehot = eidx[None] == jnp.arange(N_EXPERTS, dtype=jnp.int32)[:, None, None]
    return jnp.sum(jnp.where(onehot, pad_start[:, None, None], 0), axis=0) + rank


def _tile_copy(src, src_row, dst, dst_row, sem):
    return pltpu.make_async_copy(src.at[pl.ds(src_row * ROW_TILE, ROW_TILE), :],
                                 dst.at[pl.ds(dst_row * ROW_TILE, ROW_TILE), :], sem)


def _dispatch_kernel(ps_ref, pl_ref, nu_ref, dest_ref, dest2_ref, hrow_ref, hrow2_ref,
                     xs_hbm, zbuf, sem, zsem, *, tm, n_blocks):
    blk_rows = MOE_BLOCK * ROW_TILE

    def scatter(d_ref, src_ref, n):
        def start(t, c):
            for k in range(TOP_K):
                _tile_copy(src_ref, t, xs_hbm, d_ref[k, t], sem).start(priority=k % N_DMA_PRIORITIES)
            return c

        def wait(t, c):
            for k in range(TOP_K):
                _tile_copy(src_ref, t, xs_hbm, 0, sem).wait()
            return c

        lax.fori_loop(0, n, start, 0)
        lax.fori_loop(0, n, wait, 0)

    @pl.when(pl.program_id(0) == 0)
    def _():
        zbuf[...] = jnp.zeros(zbuf.shape, F32)

        def zcopy(block_row):
            return pltpu.make_async_copy(zbuf, xs_hbm.at[pl.ds(block_row * ROW_TILE, blk_rows), :], zsem)

        def each_expert(fn):
            def body(e, c):
                @pl.when(pl_ref[e] > 0)
                def _():
                    fn(zcopy(ps_ref[e] + pl_ref[e] - MOE_BLOCK))
                return c
            lax.fori_loop(0, N_EXPERTS, body, 0)

        def each_tail(fn):
            def body(j, c):
                fn(zcopy(j * MOE_BLOCK))
                return c
            lax.fori_loop(nu_ref[0], n_blocks, body, 0)

        each_expert(lambda cp: cp.start())
        each_tail(lambda cp: cp.start())
        each_expert(lambda cp: cp.wait())
        each_tail(lambda cp: cp.wait())
        scatter(dest2_ref, hrow2_ref, dest2_ref.shape[1])

    scatter(dest_ref, hrow_ref, tm)


def _dispatch(pad_start, pad_len, n_used, n_blocks, main, second, *, tm):
    dest, hrow = main
    dest2, hrow2 = second
    T = dest.shape[1]
    smem = pl.BlockSpec(memory_space=pltpu.SMEM)
    route = pl.BlockSpec((ROUTE_ROWS, tm), lambda i: (0, i), memory_space=pltpu.SMEM)
    return pl.pallas_call(
        functools.partial(_dispatch_kernel, tm=tm, n_blocks=n_blocks),
        grid=(T // tm,),
        in_specs=[smem, smem, smem, route, smem,
                  pl.BlockSpec((tm * ROW_TILE, LANES), lambda i: (i, 0)),
                  pl.BlockSpec(hrow2.shape, lambda i: (0, 0))],
        out_specs=pl.BlockSpec(memory_space=pl.ANY),
        out_shape=jax.ShapeDtypeStruct((n_blocks * MOE_BLOCK * ROW_TILE, LANES), F32),
        scratch_shapes=[pltpu.VMEM((MOE_BLOCK * ROW_TILE, LANES), F32),
                        pltpu.SemaphoreType.DMA(()), pltpu.SemaphoreType.DMA(())],
        compiler_params=_cparams(("arbitrary",)),
        name="dispatch",
    )(pad_start, pad_len, n_used, dest, dest2, hrow, hrow2)


def _from_row_tiles(ref, n_rows, base=0):
    return jnp.concatenate([ref[pl.ds(base * ROW_TILE + s, n_rows, stride=ROW_TILE), :]
                            for s in range(ROW_TILE)], axis=1)


def _experts_kernel(be_ref, nu_ref, x_ref, wg_ref, wu_ref, wd_ref, y_ref, wgb, wub, wdb):
    j = pl.program_id(0)
    used = j < nu_ref[0]

    @pl.when(used & ((j == 0) | (be_ref[j] != be_ref[jnp.maximum(j - 1, 0)])))
    def _():
        wgb[...] = wg_ref[...].astype(BF16)
        wub[...] = wu_ref[...].astype(BF16)
        wdb[...] = wd_ref[...].astype(BF16)

    @pl.when(used)
    def _():
        x = _from_row_tiles(x_ref, MOE_BLOCK).astype(BF16)
        gate = _dot(x, wgb[...])
        up = _dot(x, wub[...])
        hid = (gate * _sigmoid(gate) * up).astype(BF16)
        y = _dot(hid, wdb[...])
        for s in range(ROW_TILE):
            y_ref[pl.ds(s, MOE_BLOCK, stride=ROW_TILE), :] = y[:, s * LANES:(s + 1) * LANES]

    @pl.when(pl.program_id(0) >= nu_ref[0])
    def _():
        y_ref[...] = jnp.zeros(y_ref.shape, F32)


def _experts(xs, blk_e, n_used, w_gate, w_up, w_down):
    n_blocks = blk_e.shape[0]
    D = w_gate.shape[1]

    def blk(j, be, nu):
        return (jnp.minimum(j, nu[0] - 1), 0)

    def out_blk(j, be, nu):
        return (j, 0)

    def wsel(j, be, nu):
        return (be[jnp.minimum(j, nu[0] - 1)], 0, 0)

    rows = MOE_BLOCK * ROW_TILE
    grid_spec = pltpu.PrefetchScalarGridSpec(
        num_scalar_prefetch=2,
        grid=(n_blocks,),
        in_specs=[pl.BlockSpec((rows, LANES), blk),
                  pl.BlockSpec((None, D, EXPERT_FF), wsel),
                  pl.BlockSpec((None, D, EXPERT_FF), wsel),
                  pl.BlockSpec((None, EXPERT_FF, D), wsel)],
        out_specs=pl.BlockSpec((rows, LANES), out_blk),
        scratch_shapes=[pltpu.VMEM((D, EXPERT_FF), BF16), pltpu.VMEM((D, EXPERT_FF), BF16),
                        pltpu.VMEM((EXPERT_FF, D), BF16)],
    )
    return pl.pallas_call(
        _experts_kernel,
        grid_spec=grid_spec,
        out_shape=jax.ShapeDtypeStruct(xs.shape, F32),
        compiler_params=_cparams(("arbitrary",)),
        name="experts",
    )(blk_e, n_used, xs, w_gate, w_up, w_down)


def _combine_kernel(dest_ref, h_ref, gw_ref, y_hbm, wsg_ref, wsu_ref, wsd_ref, g2_ref, b2_ref,
                    o_ref, ybuf, sem, *, tm):
    def start(t, c):
        for k in range(TOP_K):
            _tile_copy(y_hbm, dest_ref[k, t], ybuf, k * tm + t, sem).start(priority=k % N_DMA_PRIORITIES)
        return c

    def wait(t, c):
        for k in range(TOP_K):
            _tile_copy(y_hbm, 0, ybuf, k * tm + t, sem).wait()
        return c

    lax.fori_loop(0, tm, start, 0)
    h = h_ref[...]
    hb = h.astype(BF16)
    gate = _dot(hb, wsg_ref[...])
    up = _dot(hb, wsu_ref[...])
    ffn = _dot((gate * _sigmoid(gate) * up).astype(BF16), wsd_ref[...])
    lax.fori_loop(0, tm, wait, 0)
    gw = gw_ref[...]
    for k in range(TOP_K):
        ffn = ffn + _from_row_tiles(ybuf, tm, base=k * tm) * gw[:, k:k + 1]
    o_ref[...] = _layer_norm(DEEPNORM_ALPHA * h + ffn, g2_ref[...], b2_ref[...])


def _combine(dest, h, gw_t, ys, w_sh_gate, w_sh_up, w_sh_down, ln2_g, ln2_b, *, tm):
    T, D = h.shape
    wsg, wsu, wsd = w_sh_gate.astype(BF16), w_sh_up.astype(BF16), w_sh_down.astype(BF16)
    g2, b2 = ln2_g.reshape(1, D), ln2_b.reshape(1, D)

    def full(a):
        return pl.BlockSpec(a.shape, lambda i: (0,) * a.ndim)

    route = pl.BlockSpec((ROUTE_ROWS, tm), lambda i: (0, i), memory_space=pltpu.SMEM)
    return pl.pallas_call(
        functools.partial(_combine_kernel, tm=tm),
        grid=(T // tm,),
        in_specs=[route,
                  pl.BlockSpec((tm, D), lambda i: (i, 0)),
                  pl.BlockSpec((tm, ROUTE_ROWS), lambda i: (i, 0)),
                  pl.BlockSpec(memory_space=pl.ANY),
                  full(wsg), full(wsu), full(wsd), full(g2), full(b2)],
        out_specs=pl.BlockSpec((tm, D), lambda i: (i, 0)),
        out_shape=jax.ShapeDtypeStruct((T, D), F32),
        scratch_shapes=[pltpu.VMEM((TOP_K * tm * ROW_TILE, LANES), F32), pltpu.SemaphoreType.DMA(())],
        compiler_params=_cparams(("arbitrary",)),
        name="combine",
    )(dest, h, gw_t, ys, wsg, wsu, wsd, g2, b2)


def kernel(x_prompt, x_sample, cache_ckv, cache_krope, state_ret, page_table, w_in, q_norm_g, w_q_up, kv_norm_g,
           w_uk, w_uv, ret_gn_g, w_mla_o, w_ret_o, w_out, ln1_g, ln1_b, w_router, router_bias,
           w_exp_gate, w_exp_up, w_exp_down, w_sh_gate, w_sh_up, w_sh_down, ln2_g, ln2_b):
    B, S, D = x_prompt.shape
    DB, Q, _ = x_sample.shape
    Tp, Ts = B * S, DB * Q
    past_len = page_table.shape[1] * cache_ckv.shape[1]
    w_small, w_ret, w_gate, w_q, w_k, w_v, w_o = _prep_weights(w_in, w_q_up, w_uk, w_uv, w_mla_o)
    pw = (w_small, w_ret, w_gate, w_q, w_k, w_v)

    tab_p = _rope_tables(jnp.arange(S))
    (qh, ckv_p, kr_p, _, rq, rk, rv, rg, ga, gb, kh, vh) = _proj(
        x_prompt.reshape(Tp, D), tab_p, pw, q_norm_g, kv_norm_g, tm=512, ret_dtype=BF16, decode=False)
    o_mla = _attn_p(qh, kh, vh, B, S, tq=512, tk=512, hps=4)
    ret, ret_state_p = _ret_p(rq, rk, rv, rg, ret_gn_g, B, S, cb=4)
    mix_w = (w_o, w_ret_o, w_out, ln1_g, ln1_b, w_router, router_bias)
    h_p, hrow_p, eidx_p, gw_p, rank_p, cnt_p = _mix(x_prompt.reshape(Tp, D), o_mla, ret, ga, gb, *mix_w,
                                                      jnp.zeros((N_EXPERTS, 1), F32), tm=256)

    tab_s = _rope_tables(jnp.tile(past_len + jnp.arange(Q), DB))
    wabs, wsel, wuv = _absorb_weights(w_uk, w_uv)
    (_, ckv_s, kr_s, krp_s, rq_s, rk_s, rv_s, rg_s, ga_s, gb_s, ql_s, qr_s) = _proj(
        x_sample.reshape(Ts, D), tab_s, pw[:4] + (wabs, wsel), q_norm_g, kv_norm_g, tm=Ts, ret_dtype=F32,
        decode=True)

    def rows_by_head(a):
        return a.reshape(DB, Q, MLA_HEADS, LANES).transpose(0, 2, 1, 3).reshape(DB, MLA_HEADS * Q, LANES)

    o_lat = _attn_s(page_table, rows_by_head(ql_s), rows_by_head(qr_s), ckv_s, krp_s, cache_ckv, cache_krope,
                    tk=min(1024, past_len))
    o_lat_tok = o_lat.reshape(DB, MLA_HEADS, Q, LANES).transpose(0, 2, 1, 3).reshape(Ts, MLA_HEADS * LANES)
    o_mla_s = _lat_up(o_lat_tok, wuv)
    ret_s, ret_state_s = _ret_s(rq_s, rk_s, rv_s, rg_s, state_ret, ret_gn_g, gb=min(16, DB))
    h_s, hrow_s, eidx_s, gw_s, rank_s, cnt = _mix(x_sample.reshape(Ts, D), o_mla_s, ret_s, ga_s, gb_s, *mix_w,
                                                  cnt_p, tm=min(256, Ts))

    n_blocks = _n_row_blocks(Tp + Ts)
    pad_start, pad_len, blk_e, n_used = _block_plan(cnt, n_blocks)
    dest_p = _dest_rows(pad_start, eidx_p, rank_p)
    dest_s = _dest_rows(pad_start, eidx_s, rank_s)
    xs = _dispatch(pad_start, pad_len, n_used, n_blocks, (dest_p, hrow_p), (dest_s, hrow_s), tm=256)
    ys = _experts(xs, blk_e, n_used, w_exp_gate, w_exp_up, w_exp_down)
    tmc = 128
    shared = (w_sh_gate, w_sh_up, w_sh_down, ln2_g, ln2_b)
    y_p = _combine(dest_p, h_p, gw_p.T, ys, *shared, tm=tmc)
    y_s = _combine(dest_s, h_s, gw_s.T, ys, *shared, tm=tmc)

    return (y_p.reshape(B, S, D), y_s.reshape(DB, Q, D),
            ckv_p.reshape(B, S, -1), kr_p.reshape(B, S, -1), ret_state_p,
            ckv_s.reshape(DB, Q, -1), kr_s.reshape(DB, Q, -1), ret_state_s)
```

```python
import functools
import math

import numpy as np
import jax
import jax.numpy as jnp
from jax import lax
from jax.experimental import pallas as pl
from jax.experimental.pallas import tpu as pltpu

F32 = jnp.float32
BF16 = jnp.bfloat16

MLA_HEADS = 8
MLA_NOPE = 64
MLA_ROPE = 32
MLA_V = 64
MLA_Q_LORA = 256
MLA_KV_LORA = 128
MLA_SCALE = (MLA_NOPE + MLA_ROPE) ** -0.5
Q_SCALE = MLA_SCALE * math.log2(math.e)
RET_HEADS = 4
RET_DK = 128
RET_DV = 128
RET_CHUNK = 128
N_EXPERTS = 64
N_GROUPS = 8
TOPK_GROUPS = 4
TOP_K = 6
EXPERT_FF = 256
SHARED_FF = 256
ROUTED_SCALE = 2.5
MOE_BLOCK = 512
ROPE_BASE = 10000.0
LN_EPS = 1e-5
RMS_EPS = 1e-6
DEPTH = 1
DEEPNORM_ALPHA = (2 * DEPTH) ** 0.25

LANES = 128
ROW_TILE = 8
N_DMA_PRIORITIES = 2
HALF_ROPE = MLA_ROPE // 2
VMEM_LIMIT = 56 * 1024 * 1024
NEG_INF = float("-inf")


def _cparams(sem):
    return pltpu.CompilerParams(dimension_semantics=sem, vmem_limit_bytes=VMEM_LIMIT)


def _dot(a, b):
    return jnp.dot(a, b, preferred_element_type=F32)


def _dot_nt(a, b):
    return lax.dot_general(a, b, (((1,), (1,)), ((), ())), preferred_element_type=F32)


def _dot_tn(a, b):
    return lax.dot_general(a, b, (((0,), (0,)), ((), ())), preferred_element_type=F32)


N_TAB = 8


def _rope_tables(pos):
    L = pos.shape[0]
    posf = pos.astype(F32)[:, None]
    half_r = RET_DK // 2
    ang_r = posf * (ROPE_BASE ** (-jnp.arange(half_r, dtype=F32) / half_r))[None, :]
    cos_r = jnp.concatenate([jnp.cos(ang_r), jnp.cos(ang_r)], axis=1)
    sin_r = jnp.concatenate([-jnp.sin(ang_r), jnp.sin(ang_r)], axis=1)
    ang_m = posf * (ROPE_BASE ** (-jnp.arange(HALF_ROPE, dtype=F32) / HALF_ROPE))[None, :]
    c, s = jnp.cos(ang_m), jnp.sin(ang_m)
    z16 = jnp.zeros((L, HALF_ROPE), F32)

    def place(parts, offset):
        body = jnp.concatenate(parts, axis=1)
        return jnp.concatenate([jnp.zeros((L, offset), F32), body,
                                jnp.zeros((L, LANES - offset - body.shape[1]), F32)], axis=1)

    cos_k = place([c, c], 0)
    sinp_k = place([z16, s], 0)
    sinm_k = place([-s, z16], 0)
    ones = jnp.ones((L, MLA_NOPE), F32)
    cos_q = jnp.concatenate([ones, c, c, jnp.zeros((L, LANES - MLA_NOPE - MLA_ROPE), F32)], axis=1) * Q_SCALE
    sinp_q = place([z16, s], MLA_NOPE) * Q_SCALE
    sinm_q = place([-s, z16], MLA_NOPE) * Q_SCALE
    return jnp.concatenate([cos_r, sin_r, cos_k, sinp_k, sinm_k, cos_q, sinp_q, sinm_q], axis=1)


def _prep_weights(w_in, w_q_up, w_uk, w_uv, w_mla_o):
    d = w_in.shape[0]
    c_q, c_kv, c_kr = MLA_Q_LORA, MLA_KV_LORA, MLA_ROPE
    o_ret = c_q + c_kv + c_kr
    n_ret = 2 * RET_HEADS * RET_DK + 2 * RET_HEADS * RET_DV
    w_small = jnp.concatenate([w_in[:, :o_ret], jnp.zeros((d, LANES - c_kr), F32)], axis=1).astype(BF16)
    w_ret = w_in[:, o_ret:o_ret + n_ret].astype(BF16)
    w_gate = w_in[:, o_ret + n_ret:].astype(BF16)
    hd = MLA_NOPE + MLA_ROPE
    w_q = jnp.pad(w_q_up.reshape(c_q, MLA_HEADS, hd), ((0, 0), (0, 0), (0, LANES - hd)))
    w_q = w_q.reshape(c_q, MLA_HEADS * LANES).astype(BF16)
    wk_top = jnp.pad(w_uk.reshape(c_kv, MLA_HEADS, MLA_NOPE), ((0, 0), (0, 0), (0, LANES - MLA_NOPE)))
    place = jnp.zeros((LANES, MLA_HEADS, LANES), F32)
    idx = jnp.arange(MLA_ROPE)
    place = place.at[idx, :, MLA_NOPE + idx].set(1.0)
    w_k = jnp.concatenate([wk_top.reshape(c_kv, -1), place.reshape(LANES, -1)], axis=0).astype(BF16)
    w_v = jnp.pad(w_uv.reshape(c_kv, MLA_HEADS, MLA_V), ((0, 0), (0, 0), (0, LANES - MLA_V)))
    w_v = w_v.reshape(c_kv, MLA_HEADS * LANES).astype(BF16)
    w_o = jnp.pad(w_mla_o.reshape(MLA_HEADS, MLA_V, -1), ((0, 0), (0, LANES - MLA_V), (0, 0)))
    w_o = w_o.reshape(MLA_HEADS * LANES, -1).astype(BF16)
    return w_small, w_ret, w_gate, w_q, w_k, w_v, w_o


def _rms_norm(x, g):
    inv = lax.rsqrt(jnp.mean(x * x, axis=-1, keepdims=True) + RMS_EPS)
    return x * inv * g


def _layer_norm(x, g, b):
    mu = jnp.mean(x, axis=-1, keepdims=True)
    xc = x - mu
    var = jnp.mean(xc * xc, axis=-1, keepdims=True)
    return xc * lax.rsqrt(var + LN_EPS) * g + b


def _sigmoid(x):
    return 1.0 / (1.0 + jnp.exp(-x))


def _proj_kernel(x_ref, tab_ref, wsm_ref, wret_ref, wg_ref, qg_ref, wq_ref, kvg_ref, wa_ref, wb_ref,
                 qh_ref, ckv_ref, kr_ref, krp_ref, rq_ref, rk_ref, rv_ref, rg_ref, ga_ref, gb_ref,
                 oa_ref, ob_ref, *, decode):
    xb = x_ref[...].astype(BF16)

    def tab(i):
        return tab_ref[:, i * LANES:(i + 1) * LANES]

    small = _dot(xb, wsm_ref[...])
    cq = small[:, :MLA_Q_LORA]
    ckv = small[:, MLA_Q_LORA:MLA_Q_LORA + MLA_KV_LORA]
    krb = small[:, MLA_Q_LORA + MLA_KV_LORA:]
    q = _dot(_rms_norm(cq, qg_ref[...]).astype(BF16), wq_ref[...])
    cos_q, sinp_q, sinm_q = tab(5), tab(6), tab(7)
    for h in range(MLA_HEADS):
        blk = q[:, h * LANES:(h + 1) * LANES]
        rot = (blk * cos_q + pltpu.roll(blk, HALF_ROPE, 1) * sinp_q
               + pltpu.roll(blk, LANES - HALF_ROPE, 1) * sinm_q)
        rot = rot.astype(BF16)
        qh_ref[:, h * LANES:(h + 1) * LANES] = rot
        if decode:
            oa_ref[:, h * LANES:(h + 1) * LANES] = _dot(rot, wa_ref[h]).astype(BF16)
            ob_ref[:, h * LANES:(h + 1) * LANES] = _dot(rot, wb_ref[...]).astype(BF16)
    ckvn = _rms_norm(ckv, kvg_ref[...])
    ckv_ref[...] = ckvn
    krr = (krb * tab(2) + pltpu.roll(krb, HALF_ROPE, 1) * tab(3)
           + pltpu.roll(krb, LANES - HALF_ROPE, 1) * tab(4))
    kr_ref[...] = krr[:, :MLA_ROPE]
    krp_ref[...] = krr
    if not decode:
        kcat = jnp.concatenate([ckvn, krr], axis=1).astype(BF16)
        oa_ref[...] = _dot(kcat, wa_ref[...]).astype(BF16)
        vv = _dot(kcat[:, :MLA_KV_LORA], wb_ref[...])
        lane = lax.broadcasted_iota(jnp.int32, vv.shape, 1) % LANES
        ob_ref[...] = jnp.where(lane == MLA_V, 1.0, vv).astype(BF16)

    r = _dot(xb, wret_ref[...])
    cos_r, sin_r = tab(0), tab(1)
    nq = RET_HEADS * RET_DK
    for h in range(RET_HEADS):
        sl = slice(h * RET_DK, (h + 1) * RET_DK)
        a = r[:, sl]
        rq_ref[:, sl] = (a * cos_r + pltpu.roll(a, RET_DK // 2, 1) * sin_r).astype(rq_ref.dtype)
        b = r[:, nq + h * RET_DK:nq + (h + 1) * RET_DK]
        rk_ref[:, sl] = ((b * cos_r + pltpu.roll(b, RET_DK // 2, 1) * sin_r)
                         * (RET_DK ** -0.5)).astype(rk_ref.dtype)
    rv_ref[...] = r[:, 2 * nq:2 * nq + RET_HEADS * RET_DV].astype(rv_ref.dtype)
    rg = r[:, 2 * nq + RET_HEADS * RET_DV:]
    rg_ref[...] = (rg * _sigmoid(rg)).astype(rg_ref.dtype)

    g = _dot(xb, wg_ref[...])
    d = ga_ref.shape[1]
    ga_ref[...] = _sigmoid(g[:, :d]).astype(ga_ref.dtype)
    gb_ref[...] = _sigmoid(g[:, d:]).astype(gb_ref.dtype)


def _proj(x2d, tab, weights, q_norm_g, kv_norm_g, *, tm, ret_dtype, decode):
    T, D = x2d.shape
    w_small, w_ret, w_gate, w_q, w_a, w_b = weights
    n_tab = tab.shape[0] // tm
    nr = RET_HEADS * RET_DK
    hp = MLA_HEADS * LANES

    def row(i):
        return (i, 0)

    def full(a):
        return pl.BlockSpec(a.shape, lambda i: (0,) * a.ndim)

    out_shapes = [
        jax.ShapeDtypeStruct((T, hp), BF16),
        jax.ShapeDtypeStruct((T, MLA_KV_LORA), F32),
        jax.ShapeDtypeStruct((T, MLA_ROPE), F32),
        jax.ShapeDtypeStruct((T, LANES), F32),
        jax.ShapeDtypeStruct((T, nr), ret_dtype),
        jax.ShapeDtypeStruct((T, nr), ret_dtype),
        jax.ShapeDtypeStruct((T, nr), ret_dtype),
        jax.ShapeDtypeStruct((T, nr), BF16),
        jax.ShapeDtypeStruct((T, D), BF16),
        jax.ShapeDtypeStruct((T, D), BF16),
        jax.ShapeDtypeStruct((T, hp), BF16),
        jax.ShapeDtypeStruct((T, hp), BF16),
    ]
    out_specs = [pl.BlockSpec((tm, s.shape[1]), row) for s in out_shapes]
    qg = q_norm_g.reshape(1, -1)
    kvg = kv_norm_g.reshape(1, -1)
    in_specs = [pl.BlockSpec((tm, D), row),
                pl.BlockSpec((tm, N_TAB * LANES), lambda i: (i % n_tab, 0)),
                full(w_small), full(w_ret), full(w_gate), full(qg), full(w_q), full(kvg), full(w_a), full(w_b)]
    return pl.pallas_call(
        functools.partial(_proj_kernel, decode=decode),
        grid=(T // tm,),
        in_specs=in_specs,
        out_specs=out_specs,
        out_shape=out_shapes,
        compiler_params=_cparams(("parallel",)),
        name="proj",
    )(x2d, tab, w_small, w_ret, w_gate, qg, w_q, kvg, w_a, w_b)


def _attn_p_kernel(q_ref, k_ref, v_ref, o_ref, m_ref, acc_ref, *, tq, tk, hps):
    i = pl.program_id(2)
    m_ref[...] = jnp.full(m_ref.shape, NEG_INF, F32)
    acc_ref[...] = jnp.zeros(acc_ref.shape, F32)

    def step(j, masked):
        for hh in range(hps):
            cols = slice(hh * LANES, (hh + 1) * LANES)
            k = k_ref[pl.ds(j * tk, tk), cols]
            v = v_ref[pl.ds(j * tk, tk), cols]
            s = _dot_nt(q_ref[:, cols], k)
            if masked:
                row = lax.broadcasted_iota(jnp.int32, (tq, tk), 0) + i * tq
                col = lax.broadcasted_iota(jnp.int32, (tq, tk), 1) + j * tk
                s = jnp.where(col <= row, s, NEG_INF)
            m_prev = m_ref[hh]
            m_new = jnp.maximum(m_prev, jnp.max(s, axis=1, keepdims=True))
            p = jnp.concatenate([jnp.exp2(s[:, c * LANES:(c + 1) * LANES] - m_new)
                                 for c in range(tk // LANES)], axis=1)
            acc_ref[hh] = jnp.exp2(m_prev - m_new) * acc_ref[hh] + _dot(p.astype(BF16), v)
            m_ref[hh] = m_new

    n_full = (i * tq) // tk

    def body(j, c):
        step(j, False)
        return c

    lax.fori_loop(0, n_full, body, 0)
    for jj in range(tq // tk):
        step(n_full + jj, True)
    for hh in range(hps):
        acc = acc_ref[hh]
        o_ref[:, hh * LANES:(hh + 1) * LANES] = (acc / acc[:, MLA_V:MLA_V + 1]).astype(o_ref.dtype)


def _attn_p(qh, kh, vh, B, S, *, tq, tk, hps):
    assert tq % tk == 0 and MLA_HEADS % hps == 0
    nq = S // tq
    hp = MLA_HEADS * LANES
    kh3 = kh.reshape(B, S, hp)
    vh3 = vh.reshape(B, S, hp)
    return pl.pallas_call(
        functools.partial(_attn_p_kernel, tq=tq, tk=tk, hps=hps),
        grid=(B, MLA_HEADS // hps, nq),
        in_specs=[pl.BlockSpec((tq, hps * LANES), lambda b, h, i: (b * nq + i, h)),
                  pl.BlockSpec((None, S, hps * LANES), lambda b, h, i: (b, 0, h)),
                  pl.BlockSpec((None, S, hps * LANES), lambda b, h, i: (b, 0, h))],
        out_specs=pl.BlockSpec((tq, hps * LANES), lambda b, h, i: (b * nq + i, h)),
        out_shape=jax.ShapeDtypeStruct((B * S, hp), BF16),
        scratch_shapes=[pltpu.VMEM((hps, tq, LANES), F32), pltpu.VMEM((hps, tq, LANES), F32)],
        compiler_params=_cparams(("parallel", "parallel", "arbitrary")),
        name="attn_p",
    )(qh, kh3, vh3)


def _ret_consts(C):
    lg = jnp.log(1.0 - 2.0 ** (-5.0 - jnp.arange(RET_HEADS, dtype=F32)))
    idx = jnp.arange(C, dtype=F32)
    diff = idx[:, None] - idx[None, :]
    dmask = jnp.where(diff >= 0, jnp.exp(jnp.maximum(diff, 0.0)[None] * lg[:, None, None]), 0.0)
    q_dec = jnp.exp((idx[None, :] + 1.0) * lg[:, None])[:, :, None]
    k_dec = jnp.exp((C - 1.0 - idx)[None, :] * lg[:, None])[:, :, None]
    s_dec = jnp.exp(C * lg)
    return dmask, q_dec, k_dec, s_dec


def _head_norm_gate(o, gate, gn):
    mu = jnp.mean(o, axis=-1, keepdims=True)
    oc = o - mu
    var = jnp.mean(oc * oc, axis=-1, keepdims=True)
    return gate * (oc * lax.rsqrt(var + LN_EPS) * gn)


def _ret_p_kernel(sdec_ref, q_ref, k_ref, v_ref, g_ref, dm_ref, qd_ref, kd_ref, gn_ref,
                  o_ref, s_ref, *, nb, cb):
    C = RET_CHUNK

    @pl.when(pl.program_id(0) == 0)
    def _():
        s_ref[...] = jnp.zeros(s_ref.shape, F32)

    for c in range(cb):
        rows = slice(c * C, (c + 1) * C)
        for b in range(nb):
            for h in range(RET_HEADS):
                cols = slice(h * RET_DK, (h + 1) * RET_DK)
                q = q_ref[b, rows, cols]
                k = k_ref[b, rows, cols]
                v = v_ref[b, rows, cols]
                state = s_ref[b, h]
                att = _dot_nt(q, k) * dm_ref[h]
                o = _dot(att.astype(BF16), v) + _dot(q, state.astype(BF16)) * qd_ref[h]
                kd = (k.astype(F32) * kd_ref[h]).astype(BF16)
                s_ref[b, h] = state * sdec_ref[h] + _dot_tn(kd, v)
                gate = g_ref[b, rows, cols].astype(F32)
                o_ref[b, rows, cols] = _head_norm_gate(o, gate, gn_ref[:, cols]).astype(o_ref.dtype)


def _ret_p(rq, rk, rv, rg, ret_gn_g, B, S, *, cb):
    C = RET_CHUNK
    nr = RET_HEADS * RET_DK
    dmask, q_dec, k_dec, s_dec = _ret_consts(C)
    blk = pl.BlockSpec((B, cb * C, nr), lambda g: (0, g, 0))

    def full(a):
        return pl.BlockSpec(a.shape, lambda g: (0,) * a.ndim)

    gn = ret_gn_g.reshape(1, nr)
    args = [a.reshape(B, S, nr) for a in (rq, rk, rv, rg)]
    ret, state = pl.pallas_call(
        functools.partial(_ret_p_kernel, nb=B, cb=cb),
        grid=(S // (cb * C),),
        in_specs=[pl.BlockSpec(memory_space=pltpu.SMEM), blk, blk, blk, blk,
                  full(dmask), full(q_dec), full(k_dec), full(gn)],
        out_specs=[blk, pl.BlockSpec((B, RET_HEADS, RET_DK, RET_DV), lambda g: (0, 0, 0, 0))],
        out_shape=[jax.ShapeDtypeStruct((B, S, nr), BF16),
                   jax.ShapeDtypeStruct((B, RET_HEADS, RET_DK, RET_DV), F32)],
        compiler_params=_cparams(("arbitrary",)),
        name="ret_p",
    )(s_dec, *args, dmask, q_dec, k_dec, gn)
    return ret.reshape(B * S, nr), state


def _ret_s_kernel(sdec_ref, q_ref, k_ref, v_ref, g_ref, s0_ref, dm_ref, qd_ref, kd_ref, gn_ref,
                  o_ref, s_ref, *, gb, q_len):
    rows = gb * q_len
    row_b = lax.broadcasted_iota(jnp.int32, (rows, RET_DV), 0) // q_len
    for h in range(RET_HEADS):
        cols = slice(h * RET_DK, (h + 1) * RET_DK)
        q = q_ref[:, cols].astype(BF16)
        k = k_ref[:, cols]
        v = v_ref[:, cols].astype(BF16)
        att = _dot_nt(q, k.astype(BF16)) * dm_ref[h]
        o = _dot(att.astype(BF16), v)
        kd = k * kd_ref[h]
        inter = jnp.zeros((rows, RET_DV), F32)
        for b in range(gb):
            state = s0_ref[b, h]
            inter = jnp.where(row_b == b, _dot(q, state.astype(BF16)), inter)
            kd_b = jnp.where(row_b == b, kd, 0.0).astype(BF16)
            s_ref[b, h] = state * sdec_ref[h] + _dot_tn(kd_b, v)
        o = o + inter * qd_ref[h]
        gate = g_ref[:, cols].astype(F32)
        o_ref[:, cols] = _head_norm_gate(o, gate, gn_ref[:, cols]).astype(o_ref.dtype)


def _ret_s(rq, rk, rv, rg, state, ret_gn_g, *, gb):
    DB = state.shape[0]
    q_len = rq.shape[0] // DB
    nr = RET_HEADS * RET_DK
    rows = gb * q_len
    dmask, q_dec, k_dec, s_dec = _ret_consts(q_len)
    same = (jnp.arange(rows)[:, None] // q_len) == (jnp.arange(rows)[None, :] // q_len)
    dm = jnp.where(same[None], jnp.tile(dmask, (1, gb, gb)), 0.0)
    qd = jnp.tile(q_dec, (1, gb, 1))
    kd = jnp.tile(k_dec, (1, gb, 1))
    gn = ret_gn_g.reshape(1, nr)
    blk = pl.BlockSpec((rows, nr), lambda g: (g, 0))
    sblk = pl.BlockSpec((gb, RET_HEADS, RET_DK, RET_DV), lambda g: (g, 0, 0, 0))

    def full(a):
        return pl.BlockSpec(a.shape, lambda g: (0,) * a.ndim)

    return pl.pallas_call(
        functools.partial(_ret_s_kernel, gb=gb, q_len=q_len),
        grid=(DB // gb,),
        in_specs=[pl.BlockSpec(memory_space=pltpu.SMEM), blk, blk, blk, blk, sblk,
                  full(dm), full(qd), full(kd), full(gn)],
        out_specs=[blk, sblk],
        out_shape=[jax.ShapeDtypeStruct((DB * q_len, nr), BF16),
                   jax.ShapeDtypeStruct(state.shape, F32)],
        compiler_params=_cparams(("parallel",)),
        name="ret_s",
    )(s_dec, rq, rk, rv, rg, state, dm, qd, kd, gn)


SLAB = 16


def _attn_s_kernel(pt_ref, ql_ref, qr_ref, cn_ref, kn_ref, ckv_hbm, krt_hbm, o_ref, ckv_buf, krt_buf, sem,
                   *, n_pages, page, q_len, tk):
    b = pl.program_id(0)
    nb = pl.num_programs(0)
    slot = b % 2
    mine = b % (SLAB // q_len)
    P = n_pages * page
    R = MLA_HEADS * q_len

    def page_copies(bi, s):
        out = []
        for p in range(n_pages):
            pg = pt_ref[bi, p]
            out.append(pltpu.make_async_copy(ckv_hbm.at[pg], ckv_buf.at[s, pl.ds(p * page, page), :], sem.at[0, s]))
            out.append(pltpu.make_async_copy(krt_hbm.at[pg], krt_buf.at[s, pl.ds(p * MLA_ROPE, MLA_ROPE), :],
                                             sem.at[1, s]))
        return out

    @pl.when(b == 0)
    def _():
        for cp in page_copies(0, 0):
            cp.start()

    @pl.when(b + 1 < nb)
    def _():
        for cp in page_copies(b + 1, 1 - slot):
            cp.start()

    ql = ql_ref[...]
    qr = qr_ref[:, :MLA_ROPE]
    for cp in page_copies(b, slot):
        cp.wait()

    ppc = tk // page

    def keys(j):
        return ckv_buf[slot, j * tk:(j + 1) * tk, :].astype(BF16)

    scores = []
    for j in range(P // tk):
        krt = jnp.concatenate([krt_buf[slot, (j * ppc + pp) * MLA_ROPE:(j * ppc + pp + 1) * MLA_ROPE, :]
                               for pp in range(ppc)], axis=1).astype(BF16)
        scores.append(_dot_nt(ql, keys(j)) + _dot(qr, krt))
    kn = cn_ref[...].astype(BF16)
    s_new = _dot_nt(ql, kn) + _dot_nt(qr_ref[...], kn_ref[...].astype(BF16))
    row_t = lax.broadcasted_iota(jnp.int32, (R, SLAB), 0)
    col_t = lax.broadcasted_iota(jnp.int32, (R, SLAB), 1)
    ok = (col_t // q_len == mine) & (col_t % q_len <= row_t % q_len)
    s_new = jnp.where(ok, s_new, NEG_INF)
    m = functools.reduce(jnp.maximum, [jnp.max(s, axis=1, keepdims=True) for s in scores + [s_new]])
    p_new = jnp.exp2(s_new - m)
    l = jnp.sum(p_new, axis=1, keepdims=True)
    acc = _dot(p_new.astype(BF16), kn)
    for j, s in enumerate(scores):
        p = jnp.exp2(s - m)
        l = l + jnp.sum(p, axis=1, keepdims=True)
        acc = acc + _dot(p.astype(BF16), keys(j))
    o_ref[...] = (acc / l).astype(o_ref.dtype)


def _attn_s(page_table, ql, qr, ckv_new, krp_new, cache_ckv, cache_krope, *, tk):
    DB, n_pages = page_table.shape
    page = cache_ckv.shape[1]
    R = ql.shape[1]
    q_len = R // MLA_HEADS
    per_slab = SLAB // q_len
    P = n_pages * page
    assert tk % page == 0 and P % tk == 0 and page == LANES
    krt = jnp.swapaxes(cache_krope, 1, 2)

    def slab(b, pt):
        return (b // per_slab, 0)

    def seq(b, pt):
        return (b, 0, 0)

    grid_spec = pltpu.PrefetchScalarGridSpec(
        num_scalar_prefetch=1,
        grid=(DB,),
        in_specs=[pl.BlockSpec((None, R, LANES), seq),
                  pl.BlockSpec((None, R, LANES), seq),
                  pl.BlockSpec((SLAB, MLA_KV_LORA), slab),
                  pl.BlockSpec((SLAB, LANES), slab),
                  pl.BlockSpec(memory_space=pl.ANY),
                  pl.BlockSpec(memory_space=pl.ANY)],
        out_specs=pl.BlockSpec((None, R, LANES), seq),
        scratch_shapes=[pltpu.VMEM((2, P, MLA_KV_LORA), F32),
                        pltpu.VMEM((2, n_pages * MLA_ROPE, page), F32),
                        pltpu.SemaphoreType.DMA((2, 2))],
    )
    return pl.pallas_call(
        functools.partial(_attn_s_kernel, n_pages=n_pages, page=page, q_len=q_len, tk=tk),
        grid_spec=grid_spec,
        out_shape=jax.ShapeDtypeStruct((DB, R, LANES), BF16),
        compiler_params=_cparams(("arbitrary",)),
        name="attn_s",
    )(page_table, ql, qr, ckv_new, krp_new, cache_ckv, krt)


def _absorb_weights(w_uk, w_uv):
    wabs = jnp.transpose(w_uk.reshape(MLA_KV_LORA, MLA_HEADS, MLA_NOPE), (1, 2, 0))
    wabs = jnp.pad(wabs, ((0, 0), (0, LANES - MLA_NOPE), (0, 0))).astype(BF16)
    idx = jnp.arange(MLA_ROPE)
    wsel = jnp.zeros((LANES, LANES), F32).at[MLA_NOPE + idx, idx].set(1.0).astype(BF16)
    wuv = jnp.pad(jnp.transpose(w_uv.reshape(MLA_KV_LORA, MLA_HEADS, MLA_V), (1, 0, 2)),
                  ((0, 0), (0, 0), (0, LANES - MLA_V))).astype(BF16)
    return wabs, wsel, wuv


def _lat_up_kernel(o_ref, wuv_ref, out_ref):
    for h in range(MLA_HEADS):
        cols = slice(h * LANES, (h + 1) * LANES)
        out_ref[:, cols] = _dot(o_ref[:, cols], wuv_ref[h]).astype(out_ref.dtype)


def _lat_up(o_lat_tok, wuv):
    return pl.pallas_call(
        _lat_up_kernel,
        grid=(1,),
        in_specs=[pl.BlockSpec(o_lat_tok.shape, lambda i: (0, 0)), pl.BlockSpec(wuv.shape, lambda i: (0, 0, 0))],
        out_specs=pl.BlockSpec(o_lat_tok.shape, lambda i: (0, 0)),
        out_shape=jax.ShapeDtypeStruct(o_lat_tok.shape, BF16),
        compiler_params=_cparams(("arbitrary",)),
        name="lat_up",
    )(o_lat_tok, wuv)


ROUTE_ROWS = 8


def _first_index(hit, idx, big):
    return jnp.min(jnp.where(hit, idx, big), axis=0, keepdims=True)


def _route(scores, sel):
    tm = scores.shape[1]
    gsz = N_EXPERTS // N_GROUPS
    sub = lax.broadcasted_iota(jnp.int32, (gsz, tm), 0)
    grp_rows = lax.broadcasted_iota(jnp.int32, (N_GROUPS, tm), 0)
    groups = [sel[g * gsz:(g + 1) * gsz, :] for g in range(N_GROUPS)]
    gscore = jnp.zeros((N_GROUPS, tm), F32)
    for g, x in enumerate(groups):
        m1 = jnp.max(x, axis=0, keepdims=True)
        first = _first_index(x == m1, sub, gsz)
        m2 = jnp.max(jnp.where(sub == first, NEG_INF, x), axis=0, keepdims=True)
        gscore = jnp.where(grp_rows == g, m1 + m2, gscore)
    chosen = jnp.zeros((N_GROUPS, tm), jnp.bool_)
    y = gscore
    for _ in range(TOPK_GROUPS):
        m = jnp.max(y, axis=0, keepdims=True)
        hit = grp_rows == _first_index(y == m, grp_rows, N_GROUPS)
        chosen = chosen | hit
        y = jnp.where(hit, NEG_INF, y)
    cand = [jnp.where(chosen[g:g + 1, :], x, NEG_INF) for g, x in enumerate(groups)]
    eids = [sub + g * gsz for g in range(N_GROUPS)]
    out_rows = lax.broadcasted_iota(jnp.int32, (ROUTE_ROWS, tm), 0)
    eidx = jnp.zeros((ROUTE_ROWS, tm), jnp.int32)
    wsel = jnp.zeros((ROUTE_ROWS, tm), F32)
    hits = []
    for k in range(TOP_K):
        m = functools.reduce(jnp.maximum, [jnp.max(c, axis=0, keepdims=True) for c in cand])
        first = functools.reduce(jnp.minimum, [_first_index(c == m, e, N_EXPERTS) for c, e in zip(cand, eids)])
        wk = jnp.zeros((1, tm), F32)
        hit_k = []
        for g in range(N_GROUPS):
            hit = eids[g] == first
            hit_k.append(jnp.where(hit, 1.0, 0.0))
            wk = wk + jnp.sum(jnp.where(hit, scores[g * gsz:(g + 1) * gsz, :], 0.0), axis=0, keepdims=True)
            cand[g] = jnp.where(hit, NEG_INF, cand[g])
        hits.append(jnp.concatenate(hit_k, axis=0))
        eidx = jnp.where(out_rows == k, first, eidx)
        wsel = jnp.where(out_rows == k, wk, wsel)
    total = jnp.sum(wsel, axis=0, keepdims=True)
    return eidx, wsel / total * ROUTED_SCALE, hits


def _ranks(hits, run):
    tm = hits[0].shape[1]
    sel = functools.reduce(jnp.add, hits)
    before = (lax.broadcasted_iota(jnp.int32, (tm, tm), 0) < lax.broadcasted_iota(jnp.int32, (tm, tm), 1))
    prefix = _dot(sel.astype(BF16), jnp.where(before, 1.0, 0.0).astype(BF16)) + run
    out_rows = lax.broadcasted_iota(jnp.int32, (ROUTE_ROWS, tm), 0)
    rank = jnp.zeros((ROUTE_ROWS, tm), F32)
    for k, hit in enumerate(hits):
        rank = jnp.where(out_rows == k, jnp.sum(hit * prefix, axis=0, keepdims=True), rank)
    run = run + jnp.sum(sel, axis=1, keepdims=True)
    return rank.astype(jnp.int32), run


def _split_hi_lo(a):
    hi = a.astype(BF16)
    lo = (a - hi.astype(F32)).astype(BF16)
    return hi, lo


def _mix_kernel(x_ref, om_ref, ret_ref, ga_ref, gb_ref, wo_ref, wr_ref, wout_ref, g1_ref, b1_ref,
                wrt_hi_ref, wrt_lo_ref, rb_ref, run0_ref,
                h_ref, hrow_ref, eidx_ref, gw_ref, rank_ref, run_ref):
    @pl.when(pl.program_id(0) == 0)
    def _():
        run_ref[...] = run0_ref[...]

    tm = h_ref.shape[0]
    y_a = _dot(om_ref[...], wo_ref[...])
    y_b = _dot(ret_ref[...], wr_ref[...])
    mixed_in = ga_ref[...].astype(F32) * y_a + gb_ref[...].astype(F32) * y_b
    mixed = _dot(mixed_in.astype(BF16), wout_ref[...])
    h = _layer_norm(DEEPNORM_ALPHA * x_ref[...] + mixed, g1_ref[...], b1_ref[...])
    h_ref[...] = h
    for s in range(h.shape[1] // LANES):
        hrow_ref[pl.ds(s, tm, stride=ROW_TILE), :] = h[:, s * LANES:(s + 1) * LANES]
    h_hi, h_lo = _split_hi_lo(h)
    logits = _dot_nt(wrt_hi_ref[...], h_hi) + (_dot_nt(wrt_hi_ref[...], h_lo) + _dot_nt(wrt_lo_ref[...], h_hi))
    scores = _sigmoid(logits)
    eidx, gw, hits = _route(scores, scores + rb_ref[...])
    rank, run = _ranks(hits, run_ref[...])
    eidx_ref[...] = eidx
    gw_ref[...] = gw
    rank_ref[...] = rank
    run_ref[...] = run


def _mix(x2d, o_mla, ret, ga, gb, w_o, w_ret_o, w_out, ln1_g, ln1_b, w_router, router_bias, run0, *, tm):
    T, D = x2d.shape
    assert D == ROW_TILE * LANES
    wrt = w_router.T
    wrt_hi, wrt_lo = _split_hi_lo(wrt)
    rb = router_bias.reshape(N_EXPERTS, 1).astype(F32)
    g1, b1 = ln1_g.reshape(1, D), ln1_b.reshape(1, D)
    wr = w_ret_o.astype(BF16)
    wout = w_out.astype(BF16)

    def row(i):
        return (i, 0)

    def full(a):
        return pl.BlockSpec(a.shape, lambda i: (0,) * a.ndim)

    route_spec = pl.BlockSpec((ROUTE_ROWS, tm), lambda i: (0, i))
    return pl.pallas_call(
        _mix_kernel,
        grid=(T // tm,),
        in_specs=[pl.BlockSpec((tm, D), row), pl.BlockSpec((tm, o_mla.shape[1]), row),
                  pl.BlockSpec((tm, ret.shape[1]), row), pl.BlockSpec((tm, D), row), pl.BlockSpec((tm, D), row),
                  full(w_o), full(wr), full(wout), full(g1), full(b1), full(wrt_hi), full(wrt_lo), full(rb),
                  full(run0)],
        out_specs=[pl.BlockSpec((tm, D), row), pl.BlockSpec((tm * ROW_TILE, LANES), row),
                   route_spec, route_spec, route_spec, pl.BlockSpec((N_EXPERTS, 1), lambda i: (0, 0))],
        out_shape=[jax.ShapeDtypeStruct((T, D), F32),
                   jax.ShapeDtypeStruct((T * ROW_TILE, LANES), F32),
                   jax.ShapeDtypeStruct((ROUTE_ROWS, T), jnp.int32),
                   jax.ShapeDtypeStruct((ROUTE_ROWS, T), F32),
                   jax.ShapeDtypeStruct((ROUTE_ROWS, T), jnp.int32),
                   jax.ShapeDtypeStruct((N_EXPERTS, 1), F32)],
        compiler_params=_cparams(("arbitrary",)),
        name="mix",
    )(x2d, o_mla, ret, ga, gb, w_o, wr, wout, g1, b1, wrt_hi, wrt_lo, rb, run0)


def _n_row_blocks(n_tokens):
    return (n_tokens * TOP_K + N_EXPERTS * (MOE_BLOCK - 1) + MOE_BLOCK - 1) // MOE_BLOCK


def _block_plan(counts, n_blocks):
    counts = counts.reshape(N_EXPERTS).astype(jnp.int32)
    pad_len = (counts + MOE_BLOCK - 1) // MOE_BLOCK * MOE_BLOCK
    pad_end = jnp.cumsum(pad_len)
    pad_start = pad_end - pad_len
    first_row = jnp.arange(n_blocks, dtype=jnp.int32) * MOE_BLOCK
    blk_e = jnp.minimum(jnp.sum((pad_end[None, :] <= first_row[:, None]).astype(jnp.int32), axis=1), N_EXPERTS - 1)
    n_used = (pad_end[-1:] // MOE_BLOCK).astype(jnp.int32)
    return pad_start.astype(jnp.int32), pad_len.astype(jnp.int32), blk_e, n_used


def _dest_rows(pad_start, eidx, rank):
    onehot = eidx[None] == jnp.arange(N_EXPERTS, dtype=jnp.int32)[:, None, None]
    return jnp.sum(jnp.where(onehot, pad_start[:, None, None], 0), axis=0) + rank


def _tile_copy(src, src_row, dst, dst_row, sem):
    return pltpu.make_async_copy(src.at[pl.ds(src_row * ROW_TILE, ROW_TILE), :],
                                 dst.at[pl.ds(dst_row * ROW_TILE, ROW_TILE), :], sem)


def _dispatch_kernel(ps_ref, pl_ref, nu_ref, dest_ref, dest2_ref, hrow_ref, hrow2_ref,
                     xs_hbm, zbuf, sem, zsem, *, tm, n_blocks):
    blk_rows = MOE_BLOCK * ROW_TILE

    def scatter(d_ref, src_ref, n):
        def start(t, c):
            for k in range(TOP_K):
                _tile_copy(src_ref, t, xs_hbm, d_ref[k, t], sem).start(priority=k % N_DMA_PRIORITIES)
            return c

        def wait(t, c):
            for k in range(TOP_K):
                _tile_copy(src_ref, t, xs_hbm, 0, sem).wait()
            return c

        lax.fori_loop(0, n, start, 0)
        lax.fori_loop(0, n, wait, 0)

    @pl.when(pl.program_id(0) == 0)
    def _():
        zbuf[...] = jnp.zeros(zbuf.shape, F32)

        def zcopy(block_row):
            return pltpu.make_async_copy(zbuf, xs_hbm.at[pl.ds(block_row * ROW_TILE, blk_rows), :], zsem)

        def each_expert(fn):
            def body(e, c):
                @pl.when(pl_ref[e] > 0)
                def _():
                    fn(zcopy(ps_ref[e] + pl_ref[e] - MOE_BLOCK))
                return c
            lax.fori_loop(0, N_EXPERTS, body, 0)

        def each_tail(fn):
            def body(j, c):
                fn(zcopy(j * MOE_BLOCK))
                return c
            lax.fori_loop(nu_ref[0], n_blocks, body, 0)

        each_expert(lambda cp: cp.start())
        each_tail(lambda cp: cp.start())
        each_expert(lambda cp: cp.wait())
        each_tail(lambda cp: cp.wait())
        scatter(dest2_ref, hrow2_ref, dest2_ref.shape[1])

    scatter(dest_ref, hrow_ref, tm)


def _dispatch(pad_start, pad_len, n_used, n_blocks, main, second, *, tm):
    dest, hrow = main
    dest2, hrow2 = second
    T = dest.shape[1]
    smem = pl.BlockSpec(memory_space=pltpu.SMEM)
    route = pl.BlockSpec((ROUTE_ROWS, tm), lambda i: (0, i), memory_space=pltpu.SMEM)
    return pl.pallas_call(
        functools.partial(_dispatch_kernel, tm=tm, n_blocks=n_blocks),
        grid=(T // tm,),
        in_specs=[smem, smem, smem, route, smem,
                  pl.BlockSpec((tm * ROW_TILE, LANES), lambda i: (i, 0)),
                  pl.BlockSpec(hrow2.shape, lambda i: (0, 0))],
        out_specs=pl.BlockSpec(memory_space=pl.ANY),
        out_shape=jax.ShapeDtypeStruct((n_blocks * MOE_BLOCK * ROW_TILE, LANES), F32),
        scratch_shapes=[pltpu.VMEM((MOE_BLOCK * ROW_TILE, LANES), F32),
                        pltpu.SemaphoreType.DMA(()), pltpu.SemaphoreType.DMA(())],
        compiler_params=_cparams(("arbitrary",)),
        name="dispatch",
    )(pad_start, pad_len, n_used, dest, dest2, hrow, hrow2)


def _from_row_tiles(ref, n_rows, base=0):
    return jnp.concatenate([ref[pl.ds(base * ROW_TILE + s, n_rows, stride=ROW_TILE), :]
                            for s in range(ROW_TILE)], axis=1)


def _experts_kernel(be_ref, nu_ref, x_ref, wg_ref, wu_ref, wd_ref, y_ref, wgb, wub, wdb):
    j = pl.program_id(0)
    used = j < nu_ref[0]

    @pl.when(used & ((j == 0) | (be_ref[j] != be_ref[jnp.maximum(j - 1, 0)])))
    def _():
        wgb[...] = wg_ref[...].astype(BF16)
        wub[...] = wu_ref[...].astype(BF16)
        wdb[...] = wd_ref[...].astype(BF16)

    @pl.when(used)
    def _():
        x = _from_row_tiles(x_ref, MOE_BLOCK).astype(BF16)
        gate = _dot(x, wgb[...])
        up = _dot(x, wub[...])
        hid = (gate * _sigmoid(gate) * up).astype(BF16)
        y = _dot(hid, wdb[...])
        for s in range(ROW_TILE):
            y_ref[pl.ds(s, MOE_BLOCK, stride=ROW_TILE), :] = y[:, s * LANES:(s + 1) * LANES]

    @pl.when(pl.program_id(0) >= nu_ref[0])
    def _():
        y_ref[...] = jnp.zeros(y_ref.shape, F32)


def _experts(xs, blk_e, n_used, w_gate, w_up, w_down):
    n_blocks = blk_e.shape[0]
    D = w_gate.shape[1]

    def blk(j, be, nu):
        return (jnp.minimum(j, nu[0] - 1), 0)

    def out_blk(j, be, nu):
        return (j, 0)

    def wsel(j, be, nu):
        return (be[jnp.minimum(j, nu[0] - 1)], 0, 0)

    rows = MOE_BLOCK * ROW_TILE
    grid_spec = pltpu.PrefetchScalarGridSpec(
        num_scalar_prefetch=2,
        grid=(n_blocks,),
        in_specs=[pl.BlockSpec((rows, LANES), blk),
                  pl.BlockSpec((None, D, EXPERT_FF), wsel),
                  pl.BlockSpec((None, D, EXPERT_FF), wsel),
                  pl.BlockSpec((None, EXPERT_FF, D), wsel)],
        out_specs=pl.BlockSpec((rows, LANES), out_blk),
        scratch_shapes=[pltpu.VMEM((D, EXPERT_FF), BF16), pltpu.VMEM((D, EXPERT_FF), BF16),
                        pltpu.VMEM((EXPERT_FF, D), BF16)],
    )
    return pl.pallas_call(
        _experts_kernel,
        grid_spec=grid_spec,
        out_shape=jax.ShapeDtypeStruct(xs.shape, F32),
        compiler_params=_cparams(("arbitrary",)),
        name="experts",
    )(blk_e, n_used, xs, w_gate, w_up, w_down)


def _combine_kernel(dest_ref, dest_next_ref, h_ref, gw_ref, y_hbm, wsg_ref, wsu_ref, wsd_ref, g2_ref, b2_ref,
                    o_ref, ybuf, sem, *, tm):
    i = pl.program_id(0)
    slot = i % 2
    per_slot = TOP_K * tm

    def fetch(d_ref, s):
        def body(t, c):
            for k in range(TOP_K):
                _tile_copy(y_hbm, d_ref[k, t], ybuf, s * per_slot + k * tm + t,
                           sem.at[s]).start(priority=k % N_DMA_PRIORITIES)
            return c
        lax.fori_loop(0, tm, body, 0)

    @pl.when(i == 0)
    def _():
        fetch(dest_ref, 0)

    @pl.when(i + 1 < pl.num_programs(0))
    def _():
        fetch(dest_next_ref, 1 - slot)

    h = h_ref[...]
    hb = h.astype(BF16)
    gate = _dot(hb, wsg_ref[...])
    up = _dot(hb, wsu_ref[...])
    ffn = _dot((gate * _sigmoid(gate) * up).astype(BF16), wsd_ref[...])

    def wait(t, c):
        for k in range(TOP_K):
            _tile_copy(y_hbm, 0, ybuf, slot * per_slot + k * tm + t, sem.at[slot]).wait()
        return c

    lax.fori_loop(0, tm, wait, 0)
    gw = gw_ref[...]
    for k in range(TOP_K):
        ffn = ffn + _from_row_tiles(ybuf, tm, base=slot * per_slot + k * tm) * gw[:, k:k + 1]
    o_ref[...] = _layer_norm(DEEPNORM_ALPHA * h + ffn, g2_ref[...], b2_ref[...])


def _combine(dest, h, gw_t, ys, w_sh_gate, w_sh_up, w_sh_down, ln2_g, ln2_b, *, tm):
    T, D = h.shape
    wsg, wsu, wsd = w_sh_gate.astype(BF16), w_sh_up.astype(BF16), w_sh_down.astype(BF16)
    g2, b2 = ln2_g.reshape(1, D), ln2_b.reshape(1, D)

    def full(a):
        return pl.BlockSpec(a.shape, lambda i: (0,) * a.ndim)

    n_tiles = T // tm
    route = pl.BlockSpec((ROUTE_ROWS, tm), lambda i: (0, i), memory_space=pltpu.SMEM)
    route_next = pl.BlockSpec((ROUTE_ROWS, tm), lambda i: (0, jnp.minimum(i + 1, n_tiles - 1)),
                              memory_space=pltpu.SMEM)
    return pl.pallas_call(
        functools.partial(_combine_kernel, tm=tm),
        grid=(n_tiles,),
        in_specs=[route, route_next,
                  pl.BlockSpec((tm, D), lambda i: (i, 0)),
                  pl.BlockSpec((tm, ROUTE_ROWS), lambda i: (i, 0)),
                  pl.BlockSpec(memory_space=pl.ANY),
                  full(wsg), full(wsu), full(wsd), full(g2), full(b2)],
        out_specs=pl.BlockSpec((tm, D), lambda i: (i, 0)),
        out_shape=jax.ShapeDtypeStruct((T, D), F32),
        scratch_shapes=[pltpu.VMEM((2 * TOP_K * tm * ROW_TILE, LANES), F32), pltpu.SemaphoreType.DMA((2,))],
        compiler_params=_cparams(("arbitrary",)),
        name="combine",
    )(dest, dest, h, gw_t, ys, wsg, wsu, wsd, g2, b2)


def kernel(x_prompt, x_sample, cache_ckv, cache_krope, state_ret, page_table, w_in, q_norm_g, w_q_up, kv_norm_g,
           w_uk, w_uv, ret_gn_g, w_mla_o, w_ret_o, w_out, ln1_g, ln1_b, w_router, router_bias,
           w_exp_gate, w_exp_up, w_exp_down, w_sh_gate, w_sh_up, w_sh_down, ln2_g, ln2_b):
    B, S, D = x_prompt.shape
    DB, Q, _ = x_sample.shape
    Tp, Ts = B * S, DB * Q
    past_len = page_table.shape[1] * cache_ckv.shape[1]
    w_small, w_ret, w_gate, w_q, w_k, w_v, w_o = _prep_weights(w_in, w_q_up, w_uk, w_uv, w_mla_o)
    pw = (w_small, w_ret, w_gate, w_q, w_k, w_v)

    tab_p = _rope_tables(jnp.arange(S))
    (qh, ckv_p, kr_p, _, rq, rk, rv, rg, ga, gb, kh, vh) = _proj(
        x_prompt.reshape(Tp, D), tab_p, pw, q_norm_g, kv_norm_g, tm=512, ret_dtype=BF16, decode=False)
    o_mla = _attn_p(qh, kh, vh, B, S, tq=512, tk=512, hps=4)
    ret, ret_state_p = _ret_p(rq, rk, rv, rg, ret_gn_g, B, S, cb=4)
    mix_w = (w_o, w_ret_o, w_out, ln1_g, ln1_b, w_router, router_bias)
    h_p, hrow_p, eidx_p, gw_p, rank_p, cnt_p = _mix(x_prompt.reshape(Tp, D), o_mla, ret, ga, gb, *mix_w,
                                                      jnp.zeros((N_EXPERTS, 1), F32), tm=256)

    tab_s = _rope_tables(jnp.tile(past_len + jnp.arange(Q), DB))
    wabs, wsel, wuv = _absorb_weights(w_uk, w_uv)
    (_, ckv_s, kr_s, krp_s, rq_s, rk_s, rv_s, rg_s, ga_s, gb_s, ql_s, qr_s) = _proj(
        x_sample.reshape(Ts, D), tab_s, pw[:4] + (wabs, wsel), q_norm_g, kv_norm_g, tm=Ts, ret_dtype=F32,
        decode=True)

    def rows_by_head(a):
        return a.reshape(DB, Q, MLA_HEADS, LANES).transpose(0, 2, 1, 3).reshape(DB, MLA_HEADS * Q, LANES)

    o_lat = _attn_s(page_table, rows_by_head(ql_s), rows_by_head(qr_s), ckv_s, krp_s, cache_ckv, cache_krope,
                    tk=min(1024, past_len))
    o_lat_tok = o_lat.reshape(DB, MLA_HEADS, Q, LANES).transpose(0, 2, 1, 3).reshape(Ts, MLA_HEADS * LANES)
    o_mla_s = _lat_up(o_lat_tok, wuv)
    ret_s, ret_state_s = _ret_s(rq_s, rk_s, rv_s, rg_s, state_ret, ret_gn_g, gb=min(16, DB))
    h_s, hrow_s, eidx_s, gw_s, rank_s, cnt = _mix(x_sample.reshape(Ts, D), o_mla_s, ret_s, ga_s, gb_s, *mix_w,
                                                  cnt_p, tm=min(256, Ts))

    n_blocks = _n_row_blocks(Tp + Ts)
    pad_start, pad_len, blk_e, n_used = _block_plan(cnt, n_blocks)
    dest_p = _dest_rows(pad_start, eidx_p, rank_p)
    dest_s = _dest_rows(pad_start, eidx_s, rank_s)
    xs = _dispatch(pad_start, pad_len, n_used, n_blocks, (dest_p, hrow_p), (dest_s, hrow_s), tm=256)
    ys = _experts(xs, blk_e, n_used, w_exp_gate, w_exp_up, w_exp_down)
    tmc = 128
    shared = (w_sh_gate, w_sh_up, w_sh_down, ln2_g, ln2_b)
    y_p = _combine(dest_p, h_p, gw_p.T, ys, *shared, tm=tmc)
    y_s = _combine(dest_s, h_s, gw_s.T, ys, *shared, tm=tmc)

    return (y_p.reshape(B, S, D), y_s.reshape(DB, Q, D),
            ckv_p.reshape(B, S, -1), kr_p.reshape(B, S, -1), ret_state_p,
            ckv_s.reshape(DB, Q, -1), kr_s.reshape(DB, Q, -1), ret_state_s)
```

```python
import functools
import math

import numpy as np
import jax
import jax.numpy as jnp
from jax import lax
from jax.experimental import pallas as pl
from jax.experimental.pallas import tpu as pltpu

F32 = jnp.float32
BF16 = jnp.bfloat16

MLA_HEADS = 8
MLA_NOPE = 64
MLA_ROPE = 32
MLA_V = 64
MLA_Q_LORA = 256
MLA_KV_LORA = 128
MLA_SCALE = (MLA_NOPE + MLA_ROPE) ** -0.5
Q_SCALE = MLA_SCALE * math.log2(math.e)
RET_HEADS = 4
RET_DK = 128
RET_DV = 128
RET_CHUNK = 128
N_EXPERTS = 64
N_GROUPS = 8
TOPK_GROUPS = 4
TOP_K = 6
EXPERT_FF = 256
SHARED_FF = 256
ROUTED_SCALE = 2.5
MOE_BLOCK = 512
ROPE_BASE = 10000.0
LN_EPS = 1e-5
RMS_EPS = 1e-6
DEPTH = 1
DEEPNORM_ALPHA = (2 * DEPTH) ** 0.25

LANES = 128
ROW_TILE = 8
HALF_ROPE = MLA_ROPE // 2
VMEM_LIMIT = 56 * 1024 * 1024
NEG_INF = float("-inf")


def _cparams(sem):
    return pltpu.CompilerParams(dimension_semantics=sem, vmem_limit_bytes=VMEM_LIMIT)


def _dot(a, b):
    return jnp.dot(a, b, preferred_element_type=F32)


def _dot_nt(a, b):
    return lax.dot_general(a, b, (((1,), (1,)), ((), ())), preferred_element_type=F32)


def _dot_tn(a, b):
    return lax.dot_general(a, b, (((0,), (0,)), ((), ())), preferred_element_type=F32)


N_TAB = 8


def _rope_tables(pos):
    L = pos.shape[0]
    posf = pos.astype(F32)[:, None]
    half_r = RET_DK // 2
    ang_r = posf * (ROPE_BASE ** (-jnp.arange(half_r, dtype=F32) / half_r))[None, :]
    cos_r = jnp.concatenate([jnp.cos(ang_r), jnp.cos(ang_r)], axis=1)
    sin_r = jnp.concatenate([-jnp.sin(ang_r), jnp.sin(ang_r)], axis=1)
    ang_m = posf * (ROPE_BASE ** (-jnp.arange(HALF_ROPE, dtype=F32) / HALF_ROPE))[None, :]
    c, s = jnp.cos(ang_m), jnp.sin(ang_m)
    z16 = jnp.zeros((L, HALF_ROPE), F32)

    def place(parts, offset):
        body = jnp.concatenate(parts, axis=1)
        return jnp.concatenate([jnp.zeros((L, offset), F32), body,
                                jnp.zeros((L, LANES - offset - body.shape[1]), F32)], axis=1)

    cos_k = place([c, c], 0)
    sinp_k = place([z16, s], 0)
    sinm_k = place([-s, z16], 0)
    ones = jnp.ones((L, MLA_NOPE), F32)
    cos_q = jnp.concatenate([ones, c, c, jnp.zeros((L, LANES - MLA_NOPE - MLA_ROPE), F32)], axis=1) * Q_SCALE
    sinp_q = place([z16, s], MLA_NOPE) * Q_SCALE
    sinm_q = place([-s, z16], MLA_NOPE) * Q_SCALE
    return jnp.concatenate([cos_r, sin_r, cos_k, sinp_k, sinm_k, cos_q, sinp_q, sinm_q], axis=1)


def _prep_weights(w_in, w_q_up, w_uk, w_uv, w_mla_o):
    d = w_in.shape[0]
    c_q, c_kv, c_kr = MLA_Q_LORA, MLA_KV_LORA, MLA_ROPE
    o_ret = c_q + c_kv + c_kr
    n_ret = 2 * RET_HEADS * RET_DK + 2 * RET_HEADS * RET_DV
    w_small = jnp.concatenate([w_in[:, :o_ret], jnp.zeros((d, LANES - c_kr), F32)], axis=1).astype(BF16)
    w_ret = w_in[:, o_ret:o_ret + n_ret].astype(BF16)
    w_gate = w_in[:, o_ret + n_ret:].astype(BF16)
    hd = MLA_NOPE + MLA_ROPE
    w_q = jnp.pad(w_q_up.reshape(c_q, MLA_HEADS, hd), ((0, 0), (0, 0), (0, LANES - hd)))
    w_q = w_q.reshape(c_q, MLA_HEADS * LANES).astype(BF16)
    wk_top = jnp.pad(w_uk.reshape(c_kv, MLA_HEADS, MLA_NOPE), ((0, 0), (0, 0), (0, LANES - MLA_NOPE)))
    place = jnp.zeros((LANES, MLA_HEADS, LANES), F32)
    idx = jnp.arange(MLA_ROPE)
    place = place.at[idx, :, MLA_NOPE + idx].set(1.0)
    w_k = jnp.concatenate([wk_top.reshape(c_kv, -1), place.reshape(LANES, -1)], axis=0).astype(BF16)
    w_v = jnp.pad(w_uv.reshape(c_kv, MLA_HEADS, MLA_V), ((0, 0), (0, 0), (0, LANES - MLA_V)))
    w_v = w_v.reshape(c_kv, MLA_HEADS * LANES).astype(BF16)
    w_o = jnp.pad(w_mla_o.reshape(MLA_HEADS, MLA_V, -1), ((0, 0), (0, LANES - MLA_V), (0, 0)))
    w_o = w_o.reshape(MLA_HEADS * LANES, -1).astype(BF16)
    return w_small, w_ret, w_gate, w_q, w_k, w_v, w_o


def _rms_norm(x, g):
    inv = lax.rsqrt(jnp.mean(x * x, axis=-1, keepdims=True) + RMS_EPS)
    return x * inv * g


def _layer_norm(x, g, b):
    mu = jnp.mean(x, axis=-1, keepdims=True)
    xc = x - mu
    var = jnp.mean(xc * xc, axis=-1, keepdims=True)
    return xc * lax.rsqrt(var + LN_EPS) * g + b


def _sigmoid(x):
    return 1.0 / (1.0 + jnp.exp(-x))


def _proj_kernel(x_ref, tab_ref, wsm_ref, wret_ref, wg_ref, qg_ref, wq_ref, kvg_ref, wa_ref, wb_ref,
                 qh_ref, ckv_ref, kr_ref, krp_ref, rq_ref, rk_ref, rv_ref, rg_ref, ga_ref, gb_ref,
                 oa_ref, ob_ref, *, decode):
    xb = x_ref[...].astype(BF16)

    def tab(i):
        return tab_ref[:, i * LANES:(i + 1) * LANES]

    small = _dot(xb, wsm_ref[...])
    cq = small[:, :MLA_Q_LORA]
    ckv = small[:, MLA_Q_LORA:MLA_Q_LORA + MLA_KV_LORA]
    krb = small[:, MLA_Q_LORA + MLA_KV_LORA:]
    q = _dot(_rms_norm(cq, qg_ref[...]).astype(BF16), wq_ref[...])
    cos_q, sinp_q, sinm_q = tab(5), tab(6), tab(7)
    for h in range(MLA_HEADS):
        blk = q[:, h * LANES:(h + 1) * LANES]
        rot = (blk * cos_q + pltpu.roll(blk, HALF_ROPE, 1) * sinp_q
               + pltpu.roll(blk, LANES - HALF_ROPE, 1) * sinm_q)
        rot = rot.astype(BF16)
        qh_ref[:, h * LANES:(h + 1) * LANES] = rot
        if decode:
            oa_ref[:, h * LANES:(h + 1) * LANES] = _dot(rot, wa_ref[h]).astype(BF16)
            ob_ref[:, h * LANES:(h + 1) * LANES] = _dot(rot, wb_ref[...]).astype(BF16)
    ckvn = _rms_norm(ckv, kvg_ref[...])
    ckv_ref[...] = ckvn
    krr = (krb * tab(2) + pltpu.roll(krb, HALF_ROPE, 1) * tab(3)
           + pltpu.roll(krb, LANES - HALF_ROPE, 1) * tab(4))
    kr_ref[...] = krr[:, :MLA_ROPE]
    krp_ref[...] = krr
    if not decode:
        kcat = jnp.concatenate([ckvn, krr], axis=1).astype(BF16)
        oa_ref[...] = _dot(kcat, wa_ref[...]).astype(BF16)
        vv = _dot(kcat[:, :MLA_KV_LORA], wb_ref[...])
        lane = lax.broadcasted_iota(jnp.int32, vv.shape, 1) % LANES
        ob_ref[...] = jnp.where(lane == MLA_V, 1.0, vv).astype(BF16)

    r = _dot(xb, wret_ref[...])
    cos_r, sin_r = tab(0), tab(1)
    nq = RET_HEADS * RET_DK
    for h in range(RET_HEADS):
        sl = slice(h * RET_DK, (h + 1) * RET_DK)
        a = r[:, sl]
        rq_ref[:, sl] = (a * cos_r + pltpu.roll(a, RET_DK // 2, 1) * sin_r).astype(rq_ref.dtype)
        b = r[:, nq + h * RET_DK:nq + (h + 1) * RET_DK]
        rk_ref[:, sl] = ((b * cos_r + pltpu.roll(b, RET_DK // 2, 1) * sin_r)
                         * (RET_DK ** -0.5)).astype(rk_ref.dtype)
    rv_ref[...] = r[:, 2 * nq:2 * nq + RET_HEADS * RET_DV].astype(rv_ref.dtype)
    rg = r[:, 2 * nq + RET_HEADS * RET_DV:]
    rg_ref[...] = (rg * _sigmoid(rg)).astype(rg_ref.dtype)

    g = _dot(xb, wg_ref[...])
    d = ga_ref.shape[1]
    ga_ref[...] = _sigmoid(g[:, :d]).astype(ga_ref.dtype)
    gb_ref[...] = _sigmoid(g[:, d:]).astype(gb_ref.dtype)


def _proj(x2d, tab, weights, q_norm_g, kv_norm_g, *, tm, ret_dtype, decode):
    T, D = x2d.shape
    w_small, w_ret, w_gate, w_q, w_a, w_b = weights
    n_tab = tab.shape[0] // tm
    nr = RET_HEADS * RET_DK
    hp = MLA_HEADS * LANES

    def row(i):
        return (i, 0)

    def full(a):
        return pl.BlockSpec(a.shape, lambda i: (0,) * a.ndim)

    out_shapes = [
        jax.ShapeDtypeStruct((T, hp), BF16),
        jax.ShapeDtypeStruct((T, MLA_KV_LORA), F32),
        jax.ShapeDtypeStruct((T, MLA_ROPE), F32),
        jax.ShapeDtypeStruct((T, LANES), F32),
        jax.ShapeDtypeStruct((T, nr), ret_dtype),
        jax.ShapeDtypeStruct((T, nr), ret_dtype),
        jax.ShapeDtypeStruct((T, nr), ret_dtype),
        jax.ShapeDtypeStruct((T, nr), BF16),
        jax.ShapeDtypeStruct((T, D), BF16),
        jax.ShapeDtypeStruct((T, D), BF16),
        jax.ShapeDtypeStruct((T, hp), BF16),
        jax.ShapeDtypeStruct((T, hp), BF16),
    ]
    out_specs = [pl.BlockSpec((tm, s.shape[1]), row) for s in out_shapes]
    qg = q_norm_g.reshape(1, -1)
    kvg = kv_norm_g.reshape(1, -1)
    in_specs = [pl.BlockSpec((tm, D), row),
                pl.BlockSpec((tm, N_TAB * LANES), lambda i: (i % n_tab, 0)),
                full(w_small), full(w_ret), full(w_gate), full(qg), full(w_q), full(kvg), full(w_a), full(w_b)]
    return pl.pallas_call(
        functools.partial(_proj_kernel, decode=decode),
        grid=(T // tm,),
        in_specs=in_specs,
        out_specs=out_specs,
        out_shape=out_shapes,
        compiler_params=_cparams(("parallel",)),
        name="proj",
    )(x2d, tab, w_small, w_ret, w_gate, qg, w_q, kvg, w_a, w_b)


def _attn_p_kernel(q_ref, k_ref, v_ref, o_ref, m_ref, acc_ref, *, tq, tk, hps):
    i = pl.program_id(2)
    m_ref[...] = jnp.full(m_ref.shape, NEG_INF, F32)
    acc_ref[...] = jnp.zeros(acc_ref.shape, F32)

    def step(j, masked):
        for hh in range(hps):
            cols = slice(hh * LANES, (hh + 1) * LANES)
            k = k_ref[pl.ds(j * tk, tk), cols]
            v = v_ref[pl.ds(j * tk, tk), cols]
            s = _dot_nt(q_ref[:, cols], k)
            if masked:
                row = lax.broadcasted_iota(jnp.int32, (tq, tk), 0) + i * tq
                col = lax.broadcasted_iota(jnp.int32, (tq, tk), 1) + j * tk
                s = jnp.where(col <= row, s, NEG_INF)
            m_prev = m_ref[hh]
            m_new = jnp.maximum(m_prev, jnp.max(s, axis=1, keepdims=True))
            p = jnp.concatenate([jnp.exp2(s[:, c * LANES:(c + 1) * LANES] - m_new)
                                 for c in range(tk // LANES)], axis=1)
            acc_ref[hh] = jnp.exp2(m_prev - m_new) * acc_ref[hh] + _dot(p.astype(BF16), v)
            m_ref[hh] = m_new

    n_full = (i * tq) // tk

    def body(j, c):
        step(j, False)
        return c

    lax.fori_loop(0, n_full, body, 0)
    for jj in range(tq // tk):
        step(n_full + jj, True)
    for hh in range(hps):
        acc = acc_ref[hh]
        o_ref[:, hh * LANES:(hh + 1) * LANES] = (acc / acc[:, MLA_V:MLA_V + 1]).astype(o_ref.dtype)


def _attn_p(qh, kh, vh, B, S, *, tq, tk, hps):
    assert tq % tk == 0 and MLA_HEADS % hps == 0
    nq = S // tq
    hp = MLA_HEADS * LANES
    kh3 = kh.reshape(B, S, hp)
    vh3 = vh.reshape(B, S, hp)
    return pl.pallas_call(
        functools.partial(_attn_p_kernel, tq=tq, tk=tk, hps=hps),
        grid=(B, MLA_HEADS // hps, nq),
        in_specs=[pl.BlockSpec((tq, hps * LANES), lambda b, h, i: (b * nq + i, h)),
                  pl.BlockSpec((None, S, hps * LANES), lambda b, h, i: (b, 0, h)),
                  pl.BlockSpec((None, S, hps * LANES), lambda b, h, i: (b, 0, h))],
        out_specs=pl.BlockSpec((tq, hps * LANES), lambda b, h, i: (b * nq + i, h)),
        out_shape=jax.ShapeDtypeStruct((B * S, hp), BF16),
        scratch_shapes=[pltpu.VMEM((hps, tq, LANES), F32), pltpu.VMEM((hps, tq, LANES), F32)],
        compiler_params=_cparams(("parallel", "parallel", "arbitrary")),
        name="attn_p",
    )(qh, kh3, vh3)


def _ret_consts(C):
    lg = jnp.log(1.0 - 2.0 ** (-5.0 - jnp.arange(RET_HEADS, dtype=F32)))
    idx = jnp.arange(C, dtype=F32)
    diff = idx[:, None] - idx[None, :]
    dmask = jnp.where(diff >= 0, jnp.exp(jnp.maximum(diff, 0.0)[None] * lg[:, None, None]), 0.0)
    q_dec = jnp.exp((idx[None, :] + 1.0) * lg[:, None])[:, :, None]
    k_dec = jnp.exp((C - 1.0 - idx)[None, :] * lg[:, None])[:, :, None]
    s_dec = jnp.exp(C * lg)
    return dmask, q_dec, k_dec, s_dec


def _head_norm_gate(o, gate, gn):
    mu = jnp.mean(o, axis=-1, keepdims=True)
    oc = o - mu
    var = jnp.mean(oc * oc, axis=-1, keepdims=True)
    return gate * (oc * lax.rsqrt(var + LN_EPS) * gn)


def _ret_p_kernel(sdec_ref, q_ref, k_ref, v_ref, g_ref, dm_ref, qd_ref, kd_ref, gn_ref,
                  o_ref, s_ref, *, nb, cb):
    C = RET_CHUNK

    @pl.when(pl.program_id(0) == 0)
    def _():
        s_ref[...] = jnp.zeros(s_ref.shape, F32)

    for c in range(cb):
        rows = slice(c * C, (c + 1) * C)
        for b in range(nb):
            for h in range(RET_HEADS):
                cols = slice(h * RET_DK, (h + 1) * RET_DK)
                q = q_ref[b, rows, cols]
                k = k_ref[b, rows, cols]
                v = v_ref[b, rows, cols]
                state = s_ref[b, h]
                att = _dot_nt(q, k) * dm_ref[h]
                o = _dot(att.astype(BF16), v) + _dot(q, state.astype(BF16)) * qd_ref[h]
                kd = (k.astype(F32) * kd_ref[h]).astype(BF16)
                s_ref[b, h] = state * sdec_ref[h] + _dot_tn(kd, v)
                gate = g_ref[b, rows, cols].astype(F32)
                o_ref[b, rows, cols] = _head_norm_gate(o, gate, gn_ref[:, cols]).astype(o_ref.dtype)


def _ret_p(rq, rk, rv, rg, ret_gn_g, B, S, *, cb):
    C = RET_CHUNK
    nr = RET_HEADS * RET_DK
    dmask, q_dec, k_dec, s_dec = _ret_consts(C)
    blk = pl.BlockSpec((B, cb * C, nr), lambda g: (0, g, 0))

    def full(a):
        return pl.BlockSpec(a.shape, lambda g: (0,) * a.ndim)

    gn = ret_gn_g.reshape(1, nr)
    args = [a.reshape(B, S, nr) for a in (rq, rk, rv, rg)]
    ret, state = pl.pallas_call(
        functools.partial(_ret_p_kernel, nb=B, cb=cb),
        grid=(S // (cb * C),),
        in_specs=[pl.BlockSpec(memory_space=pltpu.SMEM), blk, blk, blk, blk,
                  full(dmask), full(q_dec), full(k_dec), full(gn)],
        out_specs=[blk, pl.BlockSpec((B, RET_HEADS, RET_DK, RET_DV), lambda g: (0, 0, 0, 0))],
        out_shape=[jax.ShapeDtypeStruct((B, S, nr), BF16),
                   jax.ShapeDtypeStruct((B, RET_HEADS, RET_DK, RET_DV), F32)],
        compiler_params=_cparams(("arbitrary",)),
        name="ret_p",
    )(s_dec, *args, dmask, q_dec, k_dec, gn)
    return ret.reshape(B * S, nr), state


def _ret_s_kernel(sdec_ref, q_ref, k_ref, v_ref, g_ref, s0_ref, dm_ref, qd_ref, kd_ref, gn_ref,
                  o_ref, s_ref, *, gb, q_len):
    rows = gb * q_len
    row_b = lax.broadcasted_iota(jnp.int32, (rows, RET_DV), 0) // q_len
    for h in range(RET_HEADS):
        cols = slice(h * RET_DK, (h + 1) * RET_DK)
        q = q_ref[:, cols].astype(BF16)
        k = k_ref[:, cols]
        v = v_ref[:, cols].astype(BF16)
        att = _dot_nt(q, k.astype(BF16)) * dm_ref[h]
        o = _dot(att.astype(BF16), v)
        kd = k * kd_ref[h]
        inter = jnp.zeros((rows, RET_DV), F32)
        for b in range(gb):
            state = s0_ref[b, h]
            inter = jnp.where(row_b == b, _dot(q, state.astype(BF16)), inter)
            kd_b = jnp.where(row_b == b, kd, 0.0).astype(BF16)
            s_ref[b, h] = state * sdec_ref[h] + _dot_tn(kd_b, v)
        o = o + inter * qd_ref[h]
        gate = g_ref[:, cols].astype(F32)
        o_ref[:, cols] = _head_norm_gate(o, gate, gn_ref[:, cols]).astype(o_ref.dtype)


def _ret_s(rq, rk, rv, rg, state, ret_gn_g, *, gb):
    DB = state.shape[0]
    q_len = rq.shape[0] // DB
    nr = RET_HEADS * RET_DK
    rows = gb * q_len
    dmask, q_dec, k_dec, s_dec = _ret_consts(q_len)
    same = (jnp.arange(rows)[:, None] // q_len) == (jnp.arange(rows)[None, :] // q_len)
    dm = jnp.where(same[None], jnp.tile(dmask, (1, gb, gb)), 0.0)
    qd = jnp.tile(q_dec, (1, gb, 1))
    kd = jnp.tile(k_dec, (1, gb, 1))
    gn = ret_gn_g.reshape(1, nr)
    blk = pl.BlockSpec((rows, nr), lambda g: (g, 0))
    sblk = pl.BlockSpec((gb, RET_HEADS, RET_DK, RET_DV), lambda g: (g, 0, 0, 0))

    def full(a):
        return pl.BlockSpec(a.shape, lambda g: (0,) * a.ndim)

    return pl.pallas_call(
        functools.partial(_ret_s_kernel, gb=gb, q_len=q_len),
        grid=(DB // gb,),
        in_specs=[pl.BlockSpec(memory_space=pltpu.SMEM), blk, blk, blk, blk, sblk,
                  full(dm), full(qd), full(kd), full(gn)],
        out_specs=[blk, sblk],
        out_shape=[jax.ShapeDtypeStruct((DB * q_len, nr), BF16),
                   jax.ShapeDtypeStruct(state.shape, F32)],
        compiler_params=_cparams(("parallel",)),
        name="ret_s",
    )(s_dec, rq, rk, rv, rg, state, dm, qd, kd, gn)


SLAB = 16


def _attn_s_kernel(pt_ref, ql_ref, qr_ref, cn_ref, kn_ref, ckv_hbm, krt_hbm, o_ref, ckv_buf, krt_buf, sem,
                   *, n_pages, page, q_len, tk):
    b = pl.program_id(0)
    nb = pl.num_programs(0)
    slot = b % 2
    mine = b % (SLAB // q_len)
    P = n_pages * page
    R = MLA_HEADS * q_len

    def page_copies(bi, s):
        out = []
        for p in range(n_pages):
            pg = pt_ref[bi, p]
            out.append(pltpu.make_async_copy(ckv_hbm.at[pg], ckv_buf.at[s, pl.ds(p * page, page), :], sem.at[0, s]))
            out.append(pltpu.make_async_copy(krt_hbm.at[pg], krt_buf.at[s, pl.ds(p * MLA_ROPE, MLA_ROPE), :],
                                             sem.at[1, s]))
        return out

    @pl.when(b == 0)
    def _():
        for cp in page_copies(0, 0):
            cp.start()

    @pl.when(b + 1 < nb)
    def _():
        for cp in page_copies(b + 1, 1 - slot):
            cp.start()

    ql = ql_ref[...]
    qr = qr_ref[:, :MLA_ROPE]
    for cp in page_copies(b, slot):
        cp.wait()

    ppc = tk // page

    def keys(j):
        return ckv_buf[slot, j * tk:(j + 1) * tk, :].astype(BF16)

    scores = []
    for j in range(P // tk):
        krt = jnp.concatenate([krt_buf[slot, (j * ppc + pp) * MLA_ROPE:(j * ppc + pp + 1) * MLA_ROPE, :]
                               for pp in range(ppc)], axis=1).astype(BF16)
        scores.append(_dot_nt(ql, keys(j)) + _dot(qr, krt))
    kn = cn_ref[...].astype(BF16)
    s_new = _dot_nt(ql, kn) + _dot_nt(qr_ref[...], kn_ref[...].astype(BF16))
    row_t = lax.broadcasted_iota(jnp.int32, (R, SLAB), 0)
    col_t = lax.broadcasted_iota(jnp.int32, (R, SLAB), 1)
    ok = (col_t // q_len == mine) & (col_t % q_len <= row_t % q_len)
    s_new = jnp.where(ok, s_new, NEG_INF)
    m = functools.reduce(jnp.maximum, [jnp.max(s, axis=1, keepdims=True) for s in scores + [s_new]])
    p_new = jnp.exp2(s_new - m)
    l = jnp.sum(p_new, axis=1, keepdims=True)
    acc = _dot(p_new.astype(BF16), kn)
    for j, s in enumerate(scores):
        p = jnp.exp2(s - m)
        l = l + jnp.sum(p, axis=1, keepdims=True)
        acc = acc + _dot(p.astype(BF16), keys(j))
    o_ref[...] = (acc / l).astype(o_ref.dtype)


def _attn_s(page_table, ql, qr, ckv_new, krp_new, cache_ckv, cache_krope, *, tk):
    DB, n_pages = page_table.shape
    page = cache_ckv.shape[1]
    R = ql.shape[1]
    q_len = R // MLA_HEADS
    per_slab = SLAB // q_len
    P = n_pages * page
    assert tk % page == 0 and P % tk == 0 and page == LANES
    krt = jnp.swapaxes(cache_krope, 1, 2)

    def slab(b, pt):
        return (b // per_slab, 0)

    def seq(b, pt):
        return (b, 0, 0)

    grid_spec = pltpu.PrefetchScalarGridSpec(
        num_scalar_prefetch=1,
        grid=(DB,),
        in_specs=[pl.BlockSpec((None, R, LANES), seq),
                  pl.BlockSpec((None, R, LANES), seq),
                  pl.BlockSpec((SLAB, MLA_KV_LORA), slab),
                  pl.BlockSpec((SLAB, LANES), slab),
                  pl.BlockSpec(memory_space=pl.ANY),
                  pl.BlockSpec(memory_space=pl.ANY)],
        out_specs=pl.BlockSpec((None, R, LANES), seq),
        scratch_shapes=[pltpu.VMEM((2, P, MLA_KV_LORA), F32),
                        pltpu.VMEM((2, n_pages * MLA_ROPE, page), F32),
                        pltpu.SemaphoreType.DMA((2, 2))],
    )
    return pl.pallas_call(
        functools.partial(_attn_s_kernel, n_pages=n_pages, page=page, q_len=q_len, tk=tk),
        grid_spec=grid_spec,
        out_shape=jax.ShapeDtypeStruct((DB, R, LANES), BF16),
        compiler_params=_cparams(("arbitrary",)),
        name="attn_s",
    )(page_table, ql, qr, ckv_new, krp_new, cache_ckv, krt)


def _absorb_weights(w_uk, w_uv):
    wabs = jnp.transpose(w_uk.reshape(MLA_KV_LORA, MLA_HEADS, MLA_NOPE), (1, 2, 0))
    wabs = jnp.pad(wabs, ((0, 0), (0, LANES - MLA_NOPE), (0, 0))).astype(BF16)
    idx = jnp.arange(MLA_ROPE)
    wsel = jnp.zeros((LANES, LANES), F32).at[MLA_NOPE + idx, idx].set(1.0).astype(BF16)
    wuv = jnp.pad(jnp.transpose(w_uv.reshape(MLA_KV_LORA, MLA_HEADS, MLA_V), (1, 0, 2)),
                  ((0, 0), (0, 0), (0, LANES - MLA_V))).astype(BF16)
    return wabs, wsel, wuv


def _lat_up_kernel(o_ref, wuv_ref, out_ref):
    for h in range(MLA_HEADS):
        cols = slice(h * LANES, (h + 1) * LANES)
        out_ref[:, cols] = _dot(o_ref[:, cols], wuv_ref[h]).astype(out_ref.dtype)


def _lat_up(o_lat_tok, wuv):
    return pl.pallas_call(
        _lat_up_kernel,
        grid=(1,),
        in_specs=[pl.BlockSpec(o_lat_tok.shape, lambda i: (0, 0)), pl.BlockSpec(wuv.shape, lambda i: (0, 0, 0))],
        out_specs=pl.BlockSpec(o_lat_tok.shape, lambda i: (0, 0)),
        out_shape=jax.ShapeDtypeStruct(o_lat_tok.shape, BF16),
        compiler_params=_cparams(("arbitrary",)),
        name="lat_up",
    )(o_lat_tok, wuv)


ROUTE_ROWS = 8


def _first_index(hit, idx, big):
    return jnp.min(jnp.where(hit, idx, big), axis=0, keepdims=True)


def _route(scores, sel):
    tm = scores.shape[1]
    gsz = N_EXPERTS // N_GROUPS
    sub = lax.broadcasted_iota(jnp.int32, (gsz, tm), 0)
    grp_rows = lax.broadcasted_iota(jnp.int32, (N_GROUPS, tm), 0)
    groups = [sel[g * gsz:(g + 1) * gsz, :] for g in range(N_GROUPS)]
    gscore = jnp.zeros((N_GROUPS, tm), F32)
    for g, x in enumerate(groups):
        m1 = jnp.max(x, axis=0, keepdims=True)
        first = _first_index(x == m1, sub, gsz)
        m2 = jnp.max(jnp.where(sub == first, NEG_INF, x), axis=0, keepdims=True)
        gscore = jnp.where(grp_rows == g, m1 + m2, gscore)
    chosen = jnp.zeros((N_GROUPS, tm), jnp.bool_)
    y = gscore
    for _ in range(TOPK_GROUPS):
        m = jnp.max(y, axis=0, keepdims=True)
        hit = grp_rows == _first_index(y == m, grp_rows, N_GROUPS)
        chosen = chosen | hit
        y = jnp.where(hit, NEG_INF, y)
    cand = [jnp.where(chosen[g:g + 1, :], x, NEG_INF) for g, x in enumerate(groups)]
    eids = [sub + g * gsz for g in range(N_GROUPS)]
    out_rows = lax.broadcasted_iota(jnp.int32, (ROUTE_ROWS, tm), 0)
    eidx = jnp.zeros((ROUTE_ROWS, tm), jnp.int32)
    wsel = jnp.zeros((ROUTE_ROWS, tm), F32)
    hits = []
    for k in range(TOP_K):
        m = functools.reduce(jnp.maximum, [jnp.max(c, axis=0, keepdims=True) for c in cand])
        first = functools.reduce(jnp.minimum, [_first_index(c == m, e, N_EXPERTS) for c, e in zip(cand, eids)])
        wk = jnp.zeros((1, tm), F32)
        hit_k = []
        for g in range(N_GROUPS):
            hit = eids[g] == first
            hit_k.append(jnp.where(hit, 1.0, 0.0))
            wk = wk + jnp.sum(jnp.where(hit, scores[g * gsz:(g + 1) * gsz, :], 0.0), axis=0, keepdims=True)
            cand[g] = jnp.where(hit, NEG_INF, cand[g])
        hits.append(jnp.concatenate(hit_k, axis=0))
        eidx = jnp.where(out_rows == k, first, eidx)
        wsel = jnp.where(out_rows == k, wk, wsel)
    total = jnp.sum(wsel, axis=0, keepdims=True)
    return eidx, wsel / total * ROUTED_SCALE, hits


SEG_CAP = 16


def _sorted_rows(tm):
    return TOP_K * tm + N_EXPERTS * (SEG_CAP - 1) + (-(TOP_K * tm + N_EXPERTS * (SEG_CAP - 1))) % SEG_CAP


def _tile_positions(hits):
    tm = hits[0].shape[1]
    sel = functools.reduce(jnp.add, hits)
    before = (lax.broadcasted_iota(jnp.int32, (tm, tm), 0) < lax.broadcasted_iota(jnp.int32, (tm, tm), 1))
    local = _dot(sel.astype(BF16), jnp.where(before, 1.0, 0.0).astype(BF16))
    cnt = jnp.sum(sel, axis=1, keepdims=True)
    nchunk = jnp.floor((cnt + (SEG_CAP - 1)) * (1.0 / SEG_CAP))
    nchunk_rep = jnp.broadcast_to(nchunk, (N_EXPERTS, LANES))
    lower = (lax.broadcasted_iota(jnp.int32, (N_EXPERTS, N_EXPERTS), 1)
             < lax.broadcasted_iota(jnp.int32, (N_EXPERTS, N_EXPERTS), 0))
    loff_rep = _dot(jnp.where(lower, 1.0, 0.0).astype(BF16), nchunk_rep.astype(BF16)) * SEG_CAP
    where = local + loff_rep[:, :1]
    out_rows = lax.broadcasted_iota(jnp.int32, (ROUTE_ROWS, tm), 0)
    lpos = jnp.zeros((ROUTE_ROWS, tm), F32)
    for k, hit in enumerate(hits):
        lpos = jnp.where(out_rows == k, jnp.sum(hit * where, axis=0, keepdims=True), lpos)
    return lpos.astype(jnp.int32), nchunk_rep, loff_rep, cnt


def _split_hi_lo(a):
    hi = a.astype(BF16)
    lo = (a - hi.astype(F32)).astype(BF16)
    return hi, lo


def _mix_kernel(x_ref, om_ref, ret_ref, ga_ref, gb_ref, wo_ref, wr_ref, wout_ref, g1_ref, b1_ref,
                wrt_hi_ref, wrt_lo_ref, rb_ref, run0_ref,
                h_ref, gw_ref, lpos_ref, nch_ref, loff_ref, runb_ref, run_ref):
    @pl.when(pl.program_id(0) == 0)
    def _():
        run_ref[...] = run0_ref[...]

    y_a = _dot(om_ref[...], wo_ref[...])
    y_b = _dot(ret_ref[...], wr_ref[...])
    mixed_in = ga_ref[...].astype(F32) * y_a + gb_ref[...].astype(F32) * y_b
    mixed = _dot(mixed_in.astype(BF16), wout_ref[...])
    h = _layer_norm(DEEPNORM_ALPHA * x_ref[...] + mixed, g1_ref[...], b1_ref[...])
    h_ref[...] = h
    h_hi, h_lo = _split_hi_lo(h)
    logits = _dot_nt(wrt_hi_ref[...], h_hi) + (_dot_nt(wrt_hi_ref[...], h_lo) + _dot_nt(wrt_lo_ref[...], h_hi))
    scores = _sigmoid(logits)
    _, gw, hits = _route(scores, scores + rb_ref[...])
    lpos, nchunk, loff, cnt = _tile_positions(hits)
    gw_ref[...] = gw
    lpos_ref[...] = lpos
    nch_ref[...] = nchunk
    loff_ref[...] = loff
    run = run_ref[...]
    runb_ref[...] = jnp.broadcast_to(run, (N_EXPERTS, LANES))
    run_ref[...] = run + cnt


def _mix(x2d, o_mla, ret, ga, gb, w_o, w_ret_o, w_out, ln1_g, ln1_b, w_router, router_bias, run0, *, tm):
    T, D = x2d.shape
    assert D == ROW_TILE * LANES
    wrt = w_router.T
    wrt_hi, wrt_lo = _split_hi_lo(wrt)
    rb = router_bias.reshape(N_EXPERTS, 1).astype(F32)
    g1, b1 = ln1_g.reshape(1, D), ln1_b.reshape(1, D)
    wr = w_ret_o.astype(BF16)
    wout = w_out.astype(BF16)

    def row(i):
        return (i, 0)

    def full(a):
        return pl.BlockSpec(a.shape, lambda i: (0,) * a.ndim)

    route_spec = pl.BlockSpec((ROUTE_ROWS, tm), lambda i: (0, i))
    return pl.pallas_call(
        _mix_kernel,
        grid=(T // tm,),
        in_specs=[pl.BlockSpec((tm, D), row), pl.BlockSpec((tm, o_mla.shape[1]), row),
                  pl.BlockSpec((tm, ret.shape[1]), row), pl.BlockSpec((tm, D), row), pl.BlockSpec((tm, D), row),
                  full(w_o), full(wr), full(wout), full(g1), full(b1), full(wrt_hi), full(wrt_lo), full(rb),
                  full(run0)],
        out_specs=[pl.BlockSpec((tm, D), row), route_spec, route_spec,
                   pl.BlockSpec((N_EXPERTS, LANES), row), pl.BlockSpec((N_EXPERTS, LANES), row),
                   pl.BlockSpec((N_EXPERTS, LANES), row), pl.BlockSpec((N_EXPERTS, 1), lambda i: (0, 0))],
        out_shape=[jax.ShapeDtypeStruct((T, D), F32),
                   jax.ShapeDtypeStruct((ROUTE_ROWS, T), F32),
                   jax.ShapeDtypeStruct((ROUTE_ROWS, T), jnp.int32),
                   jax.ShapeDtypeStruct((T // tm * N_EXPERTS, LANES), F32),
                   jax.ShapeDtypeStruct((T // tm * N_EXPERTS, LANES), F32),
                   jax.ShapeDtypeStruct((T // tm * N_EXPERTS, LANES), F32),
                   jax.ShapeDtypeStruct((N_EXPERTS, 1), F32)],
        compiler_params=_cparams(("arbitrary",)),
        name="mix",
    )(x2d, o_mla, ret, ga, gb, w_o, wr, wout, g1, b1, wrt_hi, wrt_lo, rb, run0)


def _n_row_blocks(n_tokens):
    return (n_tokens * TOP_K + N_EXPERTS * (MOE_BLOCK + SEG_CAP - 1) + MOE_BLOCK - 1) // MOE_BLOCK


def _block_plan(counts, n_blocks):
    counts = counts.reshape(N_EXPERTS).astype(jnp.int32)
    pad_len = jnp.where(counts > 0, (counts + SEG_CAP + MOE_BLOCK - 1) // MOE_BLOCK * MOE_BLOCK, 0)
    pad_end = jnp.cumsum(pad_len)
    pad_start = pad_end - pad_len
    first_row = jnp.arange(n_blocks, dtype=jnp.int32) * MOE_BLOCK
    blk_e = jnp.minimum(jnp.sum((pad_end[None, :] <= first_row[:, None]).astype(jnp.int32), axis=1), N_EXPERTS - 1)
    n_used = (pad_end[-1:] // MOE_BLOCK).astype(jnp.int32)
    return pad_start.astype(jnp.int32), pad_len.astype(jnp.int32), blk_e, n_used


def _segment_tables(pad_start, tables, n_tiles):
    nch, loff, runb = (t[:, 0].reshape(n_tiles, 1, N_EXPERTS).astype(jnp.int32) for t in tables)
    return nch, loff, pad_start[None, None, :] + runb


def _chunk_copies(nch_ref, lo_ref, dst_ref, local_buf, rows_hbm, sem, *, to_hbm, fn):
    rows = SEG_CAP * ROW_TILE

    def per_expert(e, c):
        lo, dst = lo_ref[0, 0, e], dst_ref[0, 0, e]

        def per_chunk(j, c2):
            loc = local_buf.at[pl.ds((lo + j * SEG_CAP) * ROW_TILE, rows), :]
            far = rows_hbm.at[pl.ds((dst + j * SEG_CAP) * ROW_TILE, rows), :]
            fn(pltpu.make_async_copy(loc, far, sem) if to_hbm else pltpu.make_async_copy(far, loc, sem))
            return c2

        lax.fori_loop(0, nch_ref[0, 0, e], per_chunk, 0)
        return c

    lax.fori_loop(0, N_EXPERTS, per_expert, 0)


def _two_group_specs(tm, width, n_main):
    return (pl.BlockSpec((tm, width), lambda i: (jnp.minimum(i, n_main - 1), 0)),
            pl.BlockSpec((tm, width), lambda i: (jnp.maximum(i - n_main, 0), 0)))


def _dispatch_kernel(ps_ref, pl_ref, cnt_ref, nu_ref, nch_ref, lo_ref, dst_ref, lpos_ref, hp_ref, hs_ref,
                     xs_hbm, sbuf, zbuf, sem, zsem, *, tm, n_main, n_blocks):
    i = pl.program_id(0)
    blk_rows = MOE_BLOCK * ROW_TILE
    rl = sbuf.shape[0] // ROW_TILE

    @pl.when(i == 0)
    def _():
        zbuf[...] = jnp.zeros(zbuf.shape, F32)

        def zcopy(block_row):
            return pltpu.make_async_copy(zbuf, xs_hbm.at[pl.ds(block_row * ROW_TILE, blk_rows), :], zsem)

        def each_expert(fn):
            def body(e, c):
                end = ps_ref[e] + pl_ref[e]

                @pl.when(pl_ref[e] > 0)
                def _():
                    fn(zcopy(end - MOE_BLOCK))

                @pl.when(pl_ref[e] - cnt_ref[e] > MOE_BLOCK)
                def _():
                    fn(zcopy(end - 2 * MOE_BLOCK))
                return c
            lax.fori_loop(0, N_EXPERTS, body, 0)

        def each_tail(fn):
            def body(j, c):
                fn(zcopy(j * MOE_BLOCK))
                return c
            lax.fori_loop(nu_ref[0], n_blocks, body, 0)

        each_expert(lambda cp: cp.start())
        each_tail(lambda cp: cp.start())
        each_expert(lambda cp: cp.wait())
        each_tail(lambda cp: cp.wait())

    h = jnp.where(i < n_main, hp_ref[...], hs_ref[...]).astype(BF16)
    rows = lax.broadcasted_iota(jnp.int32, (rl, tm), 0)
    perm = jnp.zeros((rl, tm), F32)
    for k in range(TOP_K):
        perm = perm + jnp.where(rows == lpos_ref[k:k + 1, :], 1.0, 0.0)
    xsort = _dot(perm.astype(BF16), h)
    for s in range(ROW_TILE):
        sbuf[pl.ds(s, rl, stride=ROW_TILE), :] = xsort[:, s * LANES:(s + 1) * LANES]
    copies = functools.partial(_chunk_copies, nch_ref, lo_ref, dst_ref, sbuf, xs_hbm, sem, to_hbm=True)
    copies(fn=lambda cp: cp.start())
    copies(fn=lambda cp: cp.wait())


def _dispatch(plan, counts, seg, lpos, h_p, h_s, n_blocks, *, tm):
    pad_start, pad_len, _, n_used = plan
    nch, loff, dst = seg
    n_tiles = nch.shape[0]
    n_main = h_p.shape[0] // tm
    D = h_p.shape[1]
    rl = _sorted_rows(tm)
    smem = pl.BlockSpec(memory_space=pltpu.SMEM)
    seg_spec = pl.BlockSpec((1, 1, N_EXPERTS), lambda i: (i, 0, 0), memory_space=pltpu.SMEM)
    return pl.pallas_call(
        functools.partial(_dispatch_kernel, tm=tm, n_main=n_main, n_blocks=n_blocks),
        grid=(n_tiles,),
        in_specs=[smem, smem, smem, smem, seg_spec, seg_spec, seg_spec,
                  pl.BlockSpec((ROUTE_ROWS, tm), lambda i: (0, i)), *_two_group_specs(tm, D, n_main)],
        out_specs=pl.BlockSpec(memory_space=pl.ANY),
        out_shape=jax.ShapeDtypeStruct((n_blocks * MOE_BLOCK * ROW_TILE, LANES), F32),
        scratch_shapes=[pltpu.VMEM((rl * ROW_TILE, LANES), F32),
                        pltpu.VMEM((MOE_BLOCK * ROW_TILE, LANES), F32),
                        pltpu.SemaphoreType.DMA(()), pltpu.SemaphoreType.DMA(())],
        compiler_params=_cparams(("arbitrary",)),
        name="dispatch",
    )(pad_start, pad_len, counts, n_used, nch, loff, dst, lpos, h_p, h_s)


def _from_row_tiles(ref, n_rows, base=0):
    return jnp.concatenate([ref[pl.ds(base * ROW_TILE + s, n_rows, stride=ROW_TILE), :]
                            for s in range(ROW_TILE)], axis=1)


def _experts_kernel(be_ref, nu_ref, x_ref, wg_ref, wu_ref, wd_ref, y_ref, wgb, wub, wdb):
    j = pl.program_id(0)
    used = j < nu_ref[0]

    @pl.when(used & ((j == 0) | (be_ref[j] != be_ref[jnp.maximum(j - 1, 0)])))
    def _():
        wgb[...] = wg_ref[...].astype(BF16)
        wub[...] = wu_ref[...].astype(BF16)
        wdb[...] = wd_ref[...].astype(BF16)

    @pl.when(used)
    def _():
        x = _from_row_tiles(x_ref, MOE_BLOCK).astype(BF16)
        gate = _dot(x, wgb[...])
        up = _dot(x, wub[...])
        hid = (gate * _sigmoid(gate) * up).astype(BF16)
        y = _dot(hid, wdb[...])
        for s in range(ROW_TILE):
            y_ref[pl.ds(s, MOE_BLOCK, stride=ROW_TILE), :] = y[:, s * LANES:(s + 1) * LANES]

    @pl.when(pl.program_id(0) >= nu_ref[0])
    def _():
        y_ref[...] = jnp.zeros(y_ref.shape, F32)


def _experts(xs, blk_e, n_used, w_gate, w_up, w_down):
    n_blocks = blk_e.shape[0]
    D = w_gate.shape[1]

    def blk(j, be, nu):
        return (jnp.minimum(j, nu[0] - 1), 0)

    def out_blk(j, be, nu):
        return (j, 0)

    def wsel(j, be, nu):
        return (be[jnp.minimum(j, nu[0] - 1)], 0, 0)

    rows = MOE_BLOCK * ROW_TILE
    grid_spec = pltpu.PrefetchScalarGridSpec(
        num_scalar_prefetch=2,
        grid=(n_blocks,),
        in_specs=[pl.BlockSpec((rows, LANES), blk),
                  pl.BlockSpec((None, D, EXPERT_FF), wsel),
                  pl.BlockSpec((None, D, EXPERT_FF), wsel),
                  pl.BlockSpec((None, EXPERT_FF, D), wsel)],
        out_specs=pl.BlockSpec((rows, LANES), out_blk),
        scratch_shapes=[pltpu.VMEM((D, EXPERT_FF), BF16), pltpu.VMEM((D, EXPERT_FF), BF16),
                        pltpu.VMEM((EXPERT_FF, D), BF16)],
    )
    return pl.pallas_call(
        _experts_kernel,
        grid_spec=grid_spec,
        out_shape=jax.ShapeDtypeStruct(xs.shape, F32),
        compiler_params=_cparams(("arbitrary",)),
        name="experts",
    )(blk_e, n_used, xs, w_gate, w_up, w_down)


def _combine_kernel(nch_ref, lo_ref, dst_ref, lpos_ref, gw_ref, hp_ref, hs_ref, y_hbm,
                    wsg_ref, wsu_ref, wsd_ref, g2_ref, b2_ref, op_ref, os_ref, gbuf, sem, *, tm, n_main):
    i = pl.program_id(0)
    rl = gbuf.shape[0] // ROW_TILE

    @pl.when(i == 0)
    def _():
        gbuf[...] = jnp.zeros(gbuf.shape, F32)

    copies = functools.partial(_chunk_copies, nch_ref, lo_ref, dst_ref, gbuf, y_hbm, sem, to_hbm=False)
    copies(fn=lambda cp: cp.start())
    h = jnp.where(i < n_main, hp_ref[...], hs_ref[...])
    hb = h.astype(BF16)
    gate = _dot(hb, wsg_ref[...])
    up = _dot(hb, wsu_ref[...])
    ffn = _dot((gate * _sigmoid(gate) * up).astype(BF16), wsd_ref[...])
    cols = lax.broadcasted_iota(jnp.int32, (tm, rl), 1)
    place = jnp.zeros((tm, rl), F32)
    for k in range(TOP_K):
        place = place + jnp.where(cols == lpos_ref[:, k:k + 1], gw_ref[:, k:k + 1], 0.0)
    copies(fn=lambda cp: cp.wait())
    ffn = ffn + _dot(place.astype(BF16), _from_row_tiles(gbuf, rl).astype(BF16))
    out = _layer_norm(DEEPNORM_ALPHA * h + ffn, g2_ref[...], b2_ref[...])

    @pl.when(i < n_main)
    def _():
        op_ref[...] = out

    @pl.when(i >= n_main)
    def _():
        os_ref[...] = out


def _combine(seg, lpos_t, gw_t, h_p, h_s, ys, w_sh_gate, w_sh_up, w_sh_down, ln2_g, ln2_b, *, tm):
    nch, loff, dst = seg
    n_tiles = nch.shape[0]
    D = h_p.shape[1]
    n_main = h_p.shape[0] // tm
    rl = _sorted_rows(tm)
    wsg, wsu, wsd = w_sh_gate.astype(BF16), w_sh_up.astype(BF16), w_sh_down.astype(BF16)
    g2, b2 = ln2_g.reshape(1, D), ln2_b.reshape(1, D)

    def full(a):
        return pl.BlockSpec(a.shape, lambda i: (0,) * a.ndim)

    seg_spec = pl.BlockSpec((1, 1, N_EXPERTS), lambda i: (i, 0, 0), memory_space=pltpu.SMEM)
    tok = pl.BlockSpec((tm, ROUTE_ROWS), lambda i: (i, 0))
    groups = _two_group_specs(tm, D, n_main)
    return pl.pallas_call(
        functools.partial(_combine_kernel, tm=tm, n_main=n_main),
        grid=(n_tiles,),
        in_specs=[seg_spec, seg_spec, seg_spec, tok, tok, *groups, pl.BlockSpec(memory_space=pl.ANY),
                  full(wsg), full(wsu), full(wsd), full(g2), full(b2)],
        out_specs=list(groups),
        out_shape=[jax.ShapeDtypeStruct(h_p.shape, F32), jax.ShapeDtypeStruct(h_s.shape, F32)],
        scratch_shapes=[pltpu.VMEM((rl * ROW_TILE, LANES), F32), pltpu.SemaphoreType.DMA(())],
        compiler_params=_cparams(("arbitrary",)),
        name="combine",
    )(nch, loff, dst, lpos_t, gw_t, h_p, h_s, ys, wsg, wsu, wsd, g2, b2)


def kernel(x_prompt, x_sample, cache_ckv, cache_krope, state_ret, page_table, w_in, q_norm_g, w_q_up, kv_norm_g,
           w_uk, w_uv, ret_gn_g, w_mla_o, w_ret_o, w_out, ln1_g, ln1_b, w_router, router_bias,
           w_exp_gate, w_exp_up, w_exp_down, w_sh_gate, w_sh_up, w_sh_down, ln2_g, ln2_b):
    B, S, D = x_prompt.shape
    DB, Q, _ = x_sample.shape
    Tp, Ts = B * S, DB * Q
    past_len = page_table.shape[1] * cache_ckv.shape[1]
    w_small, w_ret, w_gate, w_q, w_k, w_v, w_o = _prep_weights(w_in, w_q_up, w_uk, w_uv, w_mla_o)
    pw = (w_small, w_ret, w_gate, w_q, w_k, w_v)

    tab_p = _rope_tables(jnp.arange(S))
    (qh, ckv_p, kr_p, _, rq, rk, rv, rg, ga, gb, kh, vh) = _proj(
        x_prompt.reshape(Tp, D), tab_p, pw, q_norm_g, kv_norm_g, tm=512, ret_dtype=BF16, decode=False)
    o_mla = _attn_p(qh, kh, vh, B, S, tq=512, tk=512, hps=4)
    ret, ret_state_p = _ret_p(rq, rk, rv, rg, ret_gn_g, B, S, cb=4)
    mix_w = (w_o, w_ret_o, w_out, ln1_g, ln1_b, w_router, router_bias)
    tmoe = min(256, Ts)
    h_p, gw_p, lpos_p, *seg_p, cnt_p = _mix(x_prompt.reshape(Tp, D), o_mla, ret, ga, gb, *mix_w,
                                            jnp.zeros((N_EXPERTS, 1), F32), tm=tmoe)

    tab_s = _rope_tables(jnp.tile(past_len + jnp.arange(Q), DB))
    wabs, wsel, wuv = _absorb_weights(w_uk, w_uv)
    (_, ckv_s, kr_s, krp_s, rq_s, rk_s, rv_s, rg_s, ga_s, gb_s, ql_s, qr_s) = _proj(
        x_sample.reshape(Ts, D), tab_s, pw[:4] + (wabs, wsel), q_norm_g, kv_norm_g, tm=Ts, ret_dtype=F32,
        decode=True)

    def rows_by_head(a):
        return a.reshape(DB, Q, MLA_HEADS, LANES).transpose(0, 2, 1, 3).reshape(DB, MLA_HEADS * Q, LANES)

    o_lat = _attn_s(page_table, rows_by_head(ql_s), rows_by_head(qr_s), ckv_s, krp_s, cache_ckv, cache_krope,
                    tk=min(1024, past_len))
    o_lat_tok = o_lat.reshape(DB, MLA_HEADS, Q, LANES).transpose(0, 2, 1, 3).reshape(Ts, MLA_HEADS * LANES)
    o_mla_s = _lat_up(o_lat_tok, wuv)
    ret_s, ret_state_s = _ret_s(rq_s, rk_s, rv_s, rg_s, state_ret, ret_gn_g, gb=min(16, DB))
    h_s, gw_s, lpos_s, *seg_s, cnt = _mix(x_sample.reshape(Ts, D), o_mla_s, ret_s, ga_s, gb_s, *mix_w,
                                          cnt_p, tm=tmoe)

    n_blocks = _n_row_blocks(Tp + Ts)
    plan = _block_plan(cnt, n_blocks)
    n_tiles = (Tp + Ts) // tmoe
    seg = _segment_tables(plan[0], [jnp.concatenate(t, axis=0) for t in zip(seg_p, seg_s)], n_tiles)
    lpos = jnp.concatenate([lpos_p, lpos_s], axis=1)
    gw = jnp.concatenate([gw_p, gw_s], axis=1)
    xs = _dispatch(plan, cnt.reshape(N_EXPERTS).astype(jnp.int32), seg, lpos, h_p, h_s, n_blocks, tm=tmoe)
    ys = _experts(xs, plan[2], plan[3], w_exp_gate, w_exp_up, w_exp_down)
    y_p, y_s = _combine(seg, lpos.T, gw.T, h_p, h_s, ys, w_sh_gate, w_sh_up, w_sh_down, ln2_g, ln2_b, tm=tmoe)

    return (y_p.reshape(B, S, D), y_s.reshape(DB, Q, D),
            ckv_p.reshape(B, S, -1), kr_p.reshape(B, S, -1), ret_state_p,
            ckv_s.reshape(DB, Q, -1), kr_s.reshape(DB, Q, -1), ret_state_s)
```

```python
import functools
import math

import numpy as np
import jax
import jax.numpy as jnp
from jax import lax
from jax.experimental import pallas as pl
from jax.experimental.pallas import tpu as pltpu

F32 = jnp.float32
BF16 = jnp.bfloat16

MLA_HEADS = 8
MLA_NOPE = 64
MLA_ROPE = 32
MLA_V = 64
MLA_Q_LORA = 256
MLA_KV_LORA = 128
MLA_SCALE = (MLA_NOPE + MLA_ROPE) ** -0.5
Q_SCALE = MLA_SCALE * math.log2(math.e)
RET_HEADS = 4
RET_DK = 128
RET_DV = 128
RET_CHUNK = 128
N_EXPERTS = 64
N_GROUPS = 8
TOPK_GROUPS = 4
TOP_K = 6
EXPERT_FF = 256
SHARED_FF = 256
ROUTED_SCALE = 2.5
MOE_BLOCK = 512
ROPE_BASE = 10000.0
LN_EPS = 1e-5
RMS_EPS = 1e-6
DEPTH = 1
DEEPNORM_ALPHA = (2 * DEPTH) ** 0.25

LANES = 128
ROW_TILE = 8
HALF_ROPE = MLA_ROPE // 2
VMEM_LIMIT = 56 * 1024 * 1024
NEG_INF = float("-inf")


def _cparams(sem):
    return pltpu.CompilerParams(dimension_semantics=sem, vmem_limit_bytes=VMEM_LIMIT)


def _dot(a, b):
    return jnp.dot(a, b, preferred_element_type=F32)


def _dot_nt(a, b):
    return lax.dot_general(a, b, (((1,), (1,)), ((), ())), preferred_element_type=F32)


def _dot_tn(a, b):
    return lax.dot_general(a, b, (((0,), (0,)), ((), ())), preferred_element_type=F32)


N_TAB = 8


def _rope_tables(pos):
    L = pos.shape[0]
    posf = pos.astype(F32)[:, None]
    half_r = RET_DK // 2
    ang_r = posf * (ROPE_BASE ** (-jnp.arange(half_r, dtype=F32) / half_r))[None, :]
    cos_r = jnp.concatenate([jnp.cos(ang_r), jnp.cos(ang_r)], axis=1)
    sin_r = jnp.concatenate([-jnp.sin(ang_r), jnp.sin(ang_r)], axis=1)
    ang_m = posf * (ROPE_BASE ** (-jnp.arange(HALF_ROPE, dtype=F32) / HALF_ROPE))[None, :]
    c, s = jnp.cos(ang_m), jnp.sin(ang_m)
    z16 = jnp.zeros((L, HALF_ROPE), F32)

    def place(parts, offset):
        body = jnp.concatenate(parts, axis=1)
        return jnp.concatenate([jnp.zeros((L, offset), F32), body,
                                jnp.zeros((L, LANES - offset - body.shape[1]), F32)], axis=1)

    cos_k = place([c, c], 0)
    sinp_k = place([z16, s], 0)
    sinm_k = place([-s, z16], 0)
    ones = jnp.ones((L, MLA_NOPE), F32)
    cos_q = jnp.concatenate([ones, c, c, jnp.zeros((L, LANES - MLA_NOPE - MLA_ROPE), F32)], axis=1) * Q_SCALE
    sinp_q = place([z16, s], MLA_NOPE) * Q_SCALE
    sinm_q = place([-s, z16], MLA_NOPE) * Q_SCALE
    return jnp.concatenate([cos_r, sin_r, cos_k, sinp_k, sinm_k, cos_q, sinp_q, sinm_q], axis=1)


def _prep_weights(w_in, w_q_up, w_uk, w_uv, w_mla_o):
    d = w_in.shape[0]
    c_q, c_kv, c_kr = MLA_Q_LORA, MLA_KV_LORA, MLA_ROPE
    o_ret = c_q + c_kv + c_kr
    n_ret = 2 * RET_HEADS * RET_DK + 2 * RET_HEADS * RET_DV
    w_small = jnp.concatenate([w_in[:, :o_ret], jnp.zeros((d, LANES - c_kr), F32)], axis=1).astype(BF16)
    w_ret = w_in[:, o_ret:o_ret + n_ret].astype(BF16)
    w_gate = w_in[:, o_ret + n_ret:].astype(BF16)
    hd = MLA_NOPE + MLA_ROPE
    w_q = jnp.pad(w_q_up.reshape(c_q, MLA_HEADS, hd), ((0, 0), (0, 0), (0, LANES - hd)))
    w_q = w_q.reshape(c_q, MLA_HEADS * LANES).astype(BF16)
    wk_top = jnp.pad(w_uk.reshape(c_kv, MLA_HEADS, MLA_NOPE), ((0, 0), (0, 0), (0, LANES - MLA_NOPE)))
    place = jnp.zeros((LANES, MLA_HEADS, LANES), F32)
    idx = jnp.arange(MLA_ROPE)
    place = place.at[idx, :, MLA_NOPE + idx].set(1.0)
    w_k = jnp.concatenate([wk_top.reshape(c_kv, -1), place.reshape(LANES, -1)], axis=0).astype(BF16)
    w_v = jnp.pad(w_uv.reshape(c_kv, MLA_HEADS, MLA_V), ((0, 0), (0, 0), (0, LANES - MLA_V)))
    w_v = w_v.reshape(c_kv, MLA_HEADS * LANES).astype(BF16)
    w_o = jnp.pad(w_mla_o.reshape(MLA_HEADS, MLA_V, -1), ((0, 0), (0, LANES - MLA_V), (0, 0)))
    w_o = w_o.reshape(MLA_HEADS * LANES, -1).astype(BF16)
    return w_small, w_ret, w_gate, w_q, w_k, w_v, w_o


def _rms_norm(x, g):
    inv = lax.rsqrt(jnp.mean(x * x, axis=-1, keepdims=True) + RMS_EPS)
    return x * inv * g


def _layer_norm(x, g, b):
    mu = jnp.mean(x, axis=-1, keepdims=True)
    xc = x - mu
    var = jnp.mean(xc * xc, axis=-1, keepdims=True)
    return xc * lax.rsqrt(var + LN_EPS) * g + b


def _sigmoid(x):
    return 1.0 / (1.0 + jnp.exp(-x))


def _proj_kernel(x_ref, tab_ref, wsm_ref, wret_ref, wg_ref, qg_ref, wq_ref, kvg_ref, wa_ref, wb_ref,
                 qh_ref, ckv_ref, kr_ref, krp_ref, rq_ref, rk_ref, rv_ref, rg_ref, ga_ref, gb_ref,
                 oa_ref, ob_ref, *, decode):
    xb = x_ref[...].astype(BF16)

    def tab(i):
        return tab_ref[:, i * LANES:(i + 1) * LANES]

    small = _dot(xb, wsm_ref[...])
    cq = small[:, :MLA_Q_LORA]
    ckv = small[:, MLA_Q_LORA:MLA_Q_LORA + MLA_KV_LORA]
    krb = small[:, MLA_Q_LORA + MLA_KV_LORA:]
    q = _dot(_rms_norm(cq, qg_ref[...]).astype(BF16), wq_ref[...])
    cos_q, sinp_q, sinm_q = tab(5), tab(6), tab(7)
    for h in range(MLA_HEADS):
        blk = q[:, h * LANES:(h + 1) * LANES]
        rot = (blk * cos_q + pltpu.roll(blk, HALF_ROPE, 1) * sinp_q
               + pltpu.roll(blk, LANES - HALF_ROPE, 1) * sinm_q)
        rot = rot.astype(BF16)
        qh_ref[:, h * LANES:(h + 1) * LANES] = rot
        if decode:
            oa_ref[:, h * LANES:(h + 1) * LANES] = _dot(rot, wa_ref[h]).astype(BF16)
            ob_ref[:, h * LANES:(h + 1) * LANES] = _dot(rot, wb_ref[...]).astype(BF16)
    ckvn = _rms_norm(ckv, kvg_ref[...])
    ckv_ref[...] = ckvn
    krr = (krb * tab(2) + pltpu.roll(krb, HALF_ROPE, 1) * tab(3)
           + pltpu.roll(krb, LANES - HALF_ROPE, 1) * tab(4))
    kr_ref[...] = krr[:, :MLA_ROPE]
    krp_ref[...] = krr
    if not decode:
        kcat = jnp.concatenate([ckvn, krr], axis=1).astype(BF16)
        oa_ref[...] = _dot(kcat, wa_ref[...]).astype(BF16)
        vv = _dot(kcat[:, :MLA_KV_LORA], wb_ref[...])
        lane = lax.broadcasted_iota(jnp.int32, vv.shape, 1) % LANES
        ob_ref[...] = jnp.where(lane == MLA_V, 1.0, vv).astype(BF16)

    r = _dot(xb, wret_ref[...])
    cos_r, sin_r = tab(0), tab(1)
    nq = RET_HEADS * RET_DK
    for h in range(RET_HEADS):
        sl = slice(h * RET_DK, (h + 1) * RET_DK)
        a = r[:, sl]
        rq_ref[:, sl] = (a * cos_r + pltpu.roll(a, RET_DK // 2, 1) * sin_r).astype(rq_ref.dtype)
        b = r[:, nq + h * RET_DK:nq + (h + 1) * RET_DK]
        rk_ref[:, sl] = ((b * cos_r + pltpu.roll(b, RET_DK // 2, 1) * sin_r)
                         * (RET_DK ** -0.5)).astype(rk_ref.dtype)
    rv_ref[...] = r[:, 2 * nq:2 * nq + RET_HEADS * RET_DV].astype(rv_ref.dtype)
    rg = r[:, 2 * nq + RET_HEADS * RET_DV:]
    rg_ref[...] = (rg * _sigmoid(rg)).astype(rg_ref.dtype)

    g = _dot(xb, wg_ref[...])
    d = ga_ref.shape[1]
    ga_ref[...] = _sigmoid(g[:, :d]).astype(ga_ref.dtype)
    gb_ref[...] = _sigmoid(g[:, d:]).astype(gb_ref.dtype)


def _proj(x2d, tab, weights, q_norm_g, kv_norm_g, *, tm, ret_dtype, decode):
    T, D = x2d.shape
    w_small, w_ret, w_gate, w_q, w_a, w_b = weights
    n_tab = tab.shape[0] // tm
    nr = RET_HEADS * RET_DK
    hp = MLA_HEADS * LANES

    def row(i):
        return (i, 0)

    def full(a):
        return pl.BlockSpec(a.shape, lambda i: (0,) * a.ndim)

    out_shapes = [
        jax.ShapeDtypeStruct((T, hp), BF16),
        jax.ShapeDtypeStruct((T, MLA_KV_LORA), F32),
        jax.ShapeDtypeStruct((T, MLA_ROPE), F32),
        jax.ShapeDtypeStruct((T, LANES), F32),
        jax.ShapeDtypeStruct((T, nr), ret_dtype),
        jax.ShapeDtypeStruct((T, nr), ret_dtype),
        jax.ShapeDtypeStruct((T, nr), ret_dtype),
        jax.ShapeDtypeStruct((T, nr), BF16),
        jax.ShapeDtypeStruct((T, D), BF16),
        jax.ShapeDtypeStruct((T, D), BF16),
        jax.ShapeDtypeStruct((T, hp), BF16),
        jax.ShapeDtypeStruct((T, hp), BF16),
    ]
    out_specs = [pl.BlockSpec((tm, s.shape[1]), row) for s in out_shapes]
    qg = q_norm_g.reshape(1, -1)
    kvg = kv_norm_g.reshape(1, -1)
    in_specs = [pl.BlockSpec((tm, D), row),
                pl.BlockSpec((tm, N_TAB * LANES), lambda i: (i % n_tab, 0)),
                full(w_small), full(w_ret), full(w_gate), full(qg), full(w_q), full(kvg), full(w_a), full(w_b)]
    return pl.pallas_call(
        functools.partial(_proj_kernel, decode=decode),
        grid=(T // tm,),
        in_specs=in_specs,
        out_specs=out_specs,
        out_shape=out_shapes,
        compiler_params=_cparams(("parallel",)),
        name="proj",
    )(x2d, tab, w_small, w_ret, w_gate, qg, w_q, kvg, w_a, w_b)


def _attn_p_kernel(q_ref, k_ref, v_ref, o_ref, m_ref, acc_ref, *, tq, tk, hps):
    i = pl.program_id(2)
    m_ref[...] = jnp.full(m_ref.shape, NEG_INF, F32)
    acc_ref[...] = jnp.zeros(acc_ref.shape, F32)

    def step(j, masked):
        for hh in range(hps):
            cols = slice(hh * LANES, (hh + 1) * LANES)
            k = k_ref[pl.ds(j * tk, tk), cols]
            v = v_ref[pl.ds(j * tk, tk), cols]
            s = _dot_nt(q_ref[:, cols], k)
            if masked:
                row = lax.broadcasted_iota(jnp.int32, (tq, tk), 0) + i * tq
                col = lax.broadcasted_iota(jnp.int32, (tq, tk), 1) + j * tk
                s = jnp.where(col <= row, s, NEG_INF)
            m_prev = m_ref[hh]
            m_new = jnp.maximum(m_prev, jnp.max(s, axis=1, keepdims=True))
            p = jnp.concatenate([jnp.exp2(s[:, c * LANES:(c + 1) * LANES] - m_new)
                                 for c in range(tk // LANES)], axis=1)
            acc_ref[hh] = jnp.exp2(m_prev - m_new) * acc_ref[hh] + _dot(p.astype(BF16), v)
            m_ref[hh] = m_new

    n_full = (i * tq) // tk

    def body(j, c):
        step(j, False)
        return c

    lax.fori_loop(0, n_full, body, 0)
    for jj in range(tq // tk):
        step(n_full + jj, True)
    for hh in range(hps):
        acc = acc_ref[hh]
        o_ref[:, hh * LANES:(hh + 1) * LANES] = (acc / acc[:, MLA_V:MLA_V + 1]).astype(o_ref.dtype)


def _attn_p(qh, kh, vh, B, S, *, tq, tk, hps):
    assert tq % tk == 0 and MLA_HEADS % hps == 0
    nq = S // tq
    hp = MLA_HEADS * LANES
    kh3 = kh.reshape(B, S, hp)
    vh3 = vh.reshape(B, S, hp)
    return pl.pallas_call(
        functools.partial(_attn_p_kernel, tq=tq, tk=tk, hps=hps),
        grid=(B, MLA_HEADS // hps, nq),
        in_specs=[pl.BlockSpec((tq, hps * LANES), lambda b, h, i: (b * nq + i, h)),
                  pl.BlockSpec((None, S, hps * LANES), lambda b, h, i: (b, 0, h)),
                  pl.BlockSpec((None, S, hps * LANES), lambda b, h, i: (b, 0, h))],
        out_specs=pl.BlockSpec((tq, hps * LANES), lambda b, h, i: (b * nq + i, h)),
        out_shape=jax.ShapeDtypeStruct((B * S, hp), BF16),
        scratch_shapes=[pltpu.VMEM((hps, tq, LANES), F32), pltpu.VMEM((hps, tq, LANES), F32)],
        compiler_params=_cparams(("parallel", "parallel", "arbitrary")),
        name="attn_p",
    )(qh, kh3, vh3)


def _ret_consts(C):
    lg = jnp.log(1.0 - 2.0 ** (-5.0 - jnp.arange(RET_HEADS, dtype=F32)))
    idx = jnp.arange(C, dtype=F32)
    diff = idx[:, None] - idx[None, :]
    dmask = jnp.where(diff >= 0, jnp.exp(jnp.maximum(diff, 0.0)[None] * lg[:, None, None]), 0.0)
    q_dec = jnp.exp((idx[None, :] + 1.0) * lg[:, None])[:, :, None]
    k_dec = jnp.exp((C - 1.0 - idx)[None, :] * lg[:, None])[:, :, None]
    s_dec = jnp.exp(C * lg)
    return dmask, q_dec, k_dec, s_dec


def _head_norm_gate(o, gate, gn):
    mu = jnp.mean(o, axis=-1, keepdims=True)
    oc = o - mu
    var = jnp.mean(oc * oc, axis=-1, keepdims=True)
    return gate * (oc * lax.rsqrt(var + LN_EPS) * gn)


def _ret_p_kernel(sdec_ref, q_ref, k_ref, v_ref, g_ref, dm_ref, qd_ref, kd_ref, gn_ref,
                  o_ref, s_ref, *, nb, cb):
    C = RET_CHUNK

    @pl.when(pl.program_id(0) == 0)
    def _():
        s_ref[...] = jnp.zeros(s_ref.shape, F32)

    for c in range(cb):
        rows = slice(c * C, (c + 1) * C)
        for b in range(nb):
            for h in range(RET_HEADS):
                cols = slice(h * RET_DK, (h + 1) * RET_DK)
                q = q_ref[b, rows, cols]
                k = k_ref[b, rows, cols]
                v = v_ref[b, rows, cols]
                state = s_ref[b, h]
                att = _dot_nt(q, k) * dm_ref[h]
                o = _dot(att.astype(BF16), v) + _dot(q, state.astype(BF16)) * qd_ref[h]
                kd = (k.astype(F32) * kd_ref[h]).astype(BF16)
                s_ref[b, h] = state * sdec_ref[h] + _dot_tn(kd, v)
                gate = g_ref[b, rows, cols].astype(F32)
                o_ref[b, rows, cols] = _head_norm_gate(o, gate, gn_ref[:, cols]).astype(o_ref.dtype)


def _ret_p(rq, rk, rv, rg, ret_gn_g, B, S, *, cb):
    C = RET_CHUNK
    nr = RET_HEADS * RET_DK
    dmask, q_dec, k_dec, s_dec = _ret_consts(C)
    blk = pl.BlockSpec((B, cb * C, nr), lambda g: (0, g, 0))

    def full(a):
        return pl.BlockSpec(a.shape, lambda g: (0,) * a.ndim)

    gn = ret_gn_g.reshape(1, nr)
    args = [a.reshape(B, S, nr) for a in (rq, rk, rv, rg)]
    ret, state = pl.pallas_call(
        functools.partial(_ret_p_kernel, nb=B, cb=cb),
        grid=(S // (cb * C),),
        in_specs=[pl.BlockSpec(memory_space=pltpu.SMEM), blk, blk, blk, blk,
                  full(dmask), full(q_dec), full(k_dec), full(gn)],
        out_specs=[blk, pl.BlockSpec((B, RET_HEADS, RET_DK, RET_DV), lambda g: (0, 0, 0, 0))],
        out_shape=[jax.ShapeDtypeStruct((B, S, nr), BF16),
                   jax.ShapeDtypeStruct((B, RET_HEADS, RET_DK, RET_DV), F32)],
        compiler_params=_cparams(("arbitrary",)),
        name="ret_p",
    )(s_dec, *args, dmask, q_dec, k_dec, gn)
    return ret.reshape(B * S, nr), state


def _ret_s_kernel(sdec_ref, q_ref, k_ref, v_ref, g_ref, s0_ref, dm_ref, qd_ref, kd_ref, gn_ref,
                  o_ref, s_ref, *, gb, q_len):
    rows = gb * q_len
    row_b = lax.broadcasted_iota(jnp.int32, (rows, RET_DV), 0) // q_len
    for h in range(RET_HEADS):
        cols = slice(h * RET_DK, (h + 1) * RET_DK)
        q = q_ref[:, cols].astype(BF16)
        k = k_ref[:, cols]
        v = v_ref[:, cols].astype(BF16)
        att = _dot_nt(q, k.astype(BF16)) * dm_ref[h]
        o = _dot(att.astype(BF16), v)
        kd = k * kd_ref[h]
        inter = jnp.zeros((rows, RET_DV), F32)
        for b in range(gb):
            state = s0_ref[b, h]
            inter = jnp.where(row_b == b, _dot(q, state.astype(BF16)), inter)
            kd_b = jnp.where(row_b == b, kd, 0.0).astype(BF16)
            s_ref[b, h] = state * sdec_ref[h] + _dot_tn(kd_b, v)
        o = o + inter * qd_ref[h]
        gate = g_ref[:, cols].astype(F32)
        o_ref[:, cols] = _head_norm_gate(o, gate, gn_ref[:, cols]).astype(o_ref.dtype)


def _ret_s(rq, rk, rv, rg, state, ret_gn_g, *, gb):
    DB = state.shape[0]
    q_len = rq.shape[0] // DB
    nr = RET_HEADS * RET_DK
    rows = gb * q_len
    dmask, q_dec, k_dec, s_dec = _ret_consts(q_len)
    same = (jnp.arange(rows)[:, None] // q_len) == (jnp.arange(rows)[None, :] // q_len)
    dm = jnp.where(same[None], jnp.tile(dmask, (1, gb, gb)), 0.0)
    qd = jnp.tile(q_dec, (1, gb, 1))
    kd = jnp.tile(k_dec, (1, gb, 1))
    gn = ret_gn_g.reshape(1, nr)
    blk = pl.BlockSpec((rows, nr), lambda g: (g, 0))
    sblk = pl.BlockSpec((gb, RET_HEADS, RET_DK, RET_DV), lambda g: (g, 0, 0, 0))

    def full(a):
        return pl.BlockSpec(a.shape, lambda g: (0,) * a.ndim)

    return pl.pallas_call(
        functools.partial(_ret_s_kernel, gb=gb, q_len=q_len),
        grid=(DB // gb,),
        in_specs=[pl.BlockSpec(memory_space=pltpu.SMEM), blk, blk, blk, blk, sblk,
                  full(dm), full(qd), full(kd), full(gn)],
        out_specs=[blk, sblk],
        out_shape=[jax.ShapeDtypeStruct((DB * q_len, nr), BF16),
                   jax.ShapeDtypeStruct(state.shape, F32)],
        compiler_params=_cparams(("parallel",)),
        name="ret_s",
    )(s_dec, rq, rk, rv, rg, state, dm, qd, kd, gn)


SLAB = 16


def _attn_s_kernel(pt_ref, ql_ref, qr_ref, cn_ref, kn_ref, ckv_hbm, krt_hbm, o_ref, ckv_buf, krt_buf, sem,
                   *, n_pages, page, q_len, tk):
    b = pl.program_id(0)
    nb = pl.num_programs(0)
    slot = b % 2
    mine = b % (SLAB // q_len)
    P = n_pages * page
    R = MLA_HEADS * q_len

    def page_copies(bi, s):
        out = []
        for p in range(n_pages):
            pg = pt_ref[bi, p]
            out.append(pltpu.make_async_copy(ckv_hbm.at[pg], ckv_buf.at[s, pl.ds(p * page, page), :], sem.at[0, s]))
            out.append(pltpu.make_async_copy(krt_hbm.at[pg], krt_buf.at[s, pl.ds(p * MLA_ROPE, MLA_ROPE), :],
                                             sem.at[1, s]))
        return out

    @pl.when(b == 0)
    def _():
        for cp in page_copies(0, 0):
            cp.start()

    @pl.when(b + 1 < nb)
    def _():
        for cp in page_copies(b + 1, 1 - slot):
            cp.start()

    ql = ql_ref[...]
    qr = qr_ref[:, :MLA_ROPE]
    for cp in page_copies(b, slot):
        cp.wait()

    ppc = tk // page

    def keys(j):
        return ckv_buf[slot, j * tk:(j + 1) * tk, :].astype(BF16)

    scores = []
    for j in range(P // tk):
        krt = jnp.concatenate([krt_buf[slot, (j * ppc + pp) * MLA_ROPE:(j * ppc + pp + 1) * MLA_ROPE, :]
                               for pp in range(ppc)], axis=1).astype(BF16)
        scores.append(_dot_nt(ql, keys(j)) + _dot(qr, krt))
    kn = cn_ref[...].astype(BF16)
    s_new = _dot_nt(ql, kn) + _dot_nt(qr_ref[...], kn_ref[...].astype(BF16))
    row_t = lax.broadcasted_iota(jnp.int32, (R, SLAB), 0)
    col_t = lax.broadcasted_iota(jnp.int32, (R, SLAB), 1)
    ok = (col_t // q_len == mine) & (col_t % q_len <= row_t % q_len)
    s_new = jnp.where(ok, s_new, NEG_INF)
    m = functools.reduce(jnp.maximum, [jnp.max(s, axis=1, keepdims=True) for s in scores + [s_new]])
    p_new = jnp.exp2(s_new - m)
    l = jnp.sum(p_new, axis=1, keepdims=True)
    acc = _dot(p_new.astype(BF16), kn)
    for j, s in enumerate(scores):
        p = jnp.exp2(s - m)
        l = l + jnp.sum(p, axis=1, keepdims=True)
        acc = acc + _dot(p.astype(BF16), keys(j))
    o_ref[...] = (acc / l).astype(o_ref.dtype)


def _attn_s(page_table, ql, qr, ckv_new, krp_new, cache_ckv, cache_krope, *, tk):
    DB, n_pages = page_table.shape
    page = cache_ckv.shape[1]
    R = ql.shape[1]
    q_len = R // MLA_HEADS
    per_slab = SLAB // q_len
    P = n_pages * page
    assert tk % page == 0 and P % tk == 0 and page == LANES
    krt = jnp.swapaxes(cache_krope, 1, 2)

    def slab(b, pt):
        return (b // per_slab, 0)

    def seq(b, pt):
        return (b, 0, 0)

    grid_spec = pltpu.PrefetchScalarGridSpec(
        num_scalar_prefetch=1,
        grid=(DB,),
        in_specs=[pl.BlockSpec((None, R, LANES), seq),
                  pl.BlockSpec((None, R, LANES), seq),
                  pl.BlockSpec((SLAB, MLA_KV_LORA), slab),
                  pl.BlockSpec((SLAB, LANES), slab),
                  pl.BlockSpec(memory_space=pl.ANY),
                  pl.BlockSpec(memory_space=pl.ANY)],
        out_specs=pl.BlockSpec((None, R, LANES), seq),
        scratch_shapes=[pltpu.VMEM((2, P, MLA_KV_LORA), F32),
                        pltpu.VMEM((2, n_pages * MLA_ROPE, page), F32),
                        pltpu.SemaphoreType.DMA((2, 2))],
    )
    return pl.pallas_call(
        functools.partial(_attn_s_kernel, n_pages=n_pages, page=page, q_len=q_len, tk=tk),
        grid_spec=grid_spec,
        out_shape=jax.ShapeDtypeStruct((DB, R, LANES), BF16),
        compiler_params=_cparams(("arbitrary",)),
        name="attn_s",
    )(page_table, ql, qr, ckv_new, krp_new, cache_ckv, krt)


def _absorb_weights(w_uk, w_uv):
    wabs = jnp.transpose(w_uk.reshape(MLA_KV_LORA, MLA_HEADS, MLA_NOPE), (1, 2, 0))
    wabs = jnp.pad(wabs, ((0, 0), (0, LANES - MLA_NOPE), (0, 0))).astype(BF16)
    idx = jnp.arange(MLA_ROPE)
    wsel = jnp.zeros((LANES, LANES), F32).at[MLA_NOPE + idx, idx].set(1.0).astype(BF16)
    wuv = jnp.pad(jnp.transpose(w_uv.reshape(MLA_KV_LORA, MLA_HEADS, MLA_V), (1, 0, 2)),
                  ((0, 0), (0, 0), (0, LANES - MLA_V))).astype(BF16)
    return wabs, wsel, wuv


def _lat_up_kernel(o_ref, wuv_ref, out_ref):
    for h in range(MLA_HEADS):
        cols = slice(h * LANES, (h + 1) * LANES)
        out_ref[:, cols] = _dot(o_ref[:, cols], wuv_ref[h]).astype(out_ref.dtype)


def _lat_up(o_lat_tok, wuv):
    return pl.pallas_call(
        _lat_up_kernel,
        grid=(1,),
        in_specs=[pl.BlockSpec(o_lat_tok.shape, lambda i: (0, 0)), pl.BlockSpec(wuv.shape, lambda i: (0, 0, 0))],
        out_specs=pl.BlockSpec(o_lat_tok.shape, lambda i: (0, 0)),
        out_shape=jax.ShapeDtypeStruct(o_lat_tok.shape, BF16),
        compiler_params=_cparams(("arbitrary",)),
        name="lat_up",
    )(o_lat_tok, wuv)


ROUTE_ROWS = 8


def _first_index(hit, idx, big):
    return jnp.min(jnp.where(hit, idx, big), axis=0, keepdims=True)


def _route(scores, sel):
    tm = scores.shape[1]
    gsz = N_EXPERTS // N_GROUPS
    sub = lax.broadcasted_iota(jnp.int32, (gsz, tm), 0)
    grp_rows = lax.broadcasted_iota(jnp.int32, (N_GROUPS, tm), 0)
    groups = [sel[g * gsz:(g + 1) * gsz, :] for g in range(N_GROUPS)]
    gscore = jnp.zeros((N_GROUPS, tm), F32)
    for g, x in enumerate(groups):
        m1 = jnp.max(x, axis=0, keepdims=True)
        first = _first_index(x == m1, sub, gsz)
        m2 = jnp.max(jnp.where(sub == first, NEG_INF, x), axis=0, keepdims=True)
        gscore = jnp.where(grp_rows == g, m1 + m2, gscore)
    chosen = jnp.zeros((N_GROUPS, tm), jnp.bool_)
    y = gscore
    for _ in range(TOPK_GROUPS):
        m = jnp.max(y, axis=0, keepdims=True)
        hit = grp_rows == _first_index(y == m, grp_rows, N_GROUPS)
        chosen = chosen | hit
        y = jnp.where(hit, NEG_INF, y)
    cand = [jnp.where(chosen[g:g + 1, :], x, NEG_INF) for g, x in enumerate(groups)]
    eids = [sub + g * gsz for g in range(N_GROUPS)]
    out_rows = lax.broadcasted_iota(jnp.int32, (ROUTE_ROWS, tm), 0)
    eidx = jnp.zeros((ROUTE_ROWS, tm), jnp.int32)
    wsel = jnp.zeros((ROUTE_ROWS, tm), F32)
    hits = []
    for k in range(TOP_K):
        m = functools.reduce(jnp.maximum, [jnp.max(c, axis=0, keepdims=True) for c in cand])
        first = functools.reduce(jnp.minimum, [_first_index(c == m, e, N_EXPERTS) for c, e in zip(cand, eids)])
        wk = jnp.zeros((1, tm), F32)
        hit_k = []
        for g in range(N_GROUPS):
            hit = eids[g] == first
            hit_k.append(jnp.where(hit, 1.0, 0.0))
            wk = wk + jnp.sum(jnp.where(hit, scores[g * gsz:(g + 1) * gsz, :], 0.0), axis=0, keepdims=True)
            cand[g] = jnp.where(hit, NEG_INF, cand[g])
        hits.append(jnp.concatenate(hit_k, axis=0))
        eidx = jnp.where(out_rows == k, first, eidx)
        wsel = jnp.where(out_rows == k, wk, wsel)
    total = jnp.sum(wsel, axis=0, keepdims=True)
    return eidx, wsel / total * ROUTED_SCALE, hits


SEG_CAP = 16


def _sorted_rows(tm):
    return TOP_K * tm + N_EXPERTS * (SEG_CAP - 1) + (-(TOP_K * tm + N_EXPERTS * (SEG_CAP - 1))) % SEG_CAP


def _tile_positions(hits):
    tm = hits[0].shape[1]
    sel = functools.reduce(jnp.add, hits)
    before = (lax.broadcasted_iota(jnp.int32, (tm, tm), 0) < lax.broadcasted_iota(jnp.int32, (tm, tm), 1))
    local = _dot(sel.astype(BF16), jnp.where(before, 1.0, 0.0).astype(BF16))
    cnt = jnp.sum(sel, axis=1, keepdims=True)
    nchunk = jnp.floor((cnt + (SEG_CAP - 1)) * (1.0 / SEG_CAP))
    nchunk_rep = jnp.broadcast_to(nchunk, (N_EXPERTS, LANES))
    lower = (lax.broadcasted_iota(jnp.int32, (N_EXPERTS, N_EXPERTS), 1)
             < lax.broadcasted_iota(jnp.int32, (N_EXPERTS, N_EXPERTS), 0))
    loff_rep = _dot(jnp.where(lower, 1.0, 0.0).astype(BF16), nchunk_rep.astype(BF16)) * SEG_CAP
    where = local + loff_rep[:, :1]
    out_rows = lax.broadcasted_iota(jnp.int32, (ROUTE_ROWS, tm), 0)
    lpos = jnp.zeros((ROUTE_ROWS, tm), F32)
    for k, hit in enumerate(hits):
        lpos = jnp.where(out_rows == k, jnp.sum(hit * where, axis=0, keepdims=True), lpos)
    return lpos.astype(jnp.int32), nchunk_rep, loff_rep, cnt


def _split_hi_lo(a):
    hi = a.astype(BF16)
    lo = (a - hi.astype(F32)).astype(BF16)
    return hi, lo


def _mix_kernel(x_ref, om_ref, ret_ref, ga_ref, gb_ref, wo_ref, wr_ref, wout_ref, g1_ref, b1_ref,
                wrt_hi_ref, wrt_lo_ref, rb_ref, run0_ref,
                h_ref, gw_ref, lpos_ref, nch_ref, loff_ref, runb_ref, run_ref):
    @pl.when(pl.program_id(0) == 0)
    def _():
        run_ref[...] = run0_ref[...]

    y_a = _dot(om_ref[...], wo_ref[...])
    y_b = _dot(ret_ref[...], wr_ref[...])
    mixed_in = ga_ref[...].astype(F32) * y_a + gb_ref[...].astype(F32) * y_b
    mixed = _dot(mixed_in.astype(BF16), wout_ref[...])
    h = _layer_norm(DEEPNORM_ALPHA * x_ref[...] + mixed, g1_ref[...], b1_ref[...])
    h_ref[...] = h
    h_hi, h_lo = _split_hi_lo(h)
    logits = _dot_nt(wrt_hi_ref[...], h_hi) + (_dot_nt(wrt_hi_ref[...], h_lo) + _dot_nt(wrt_lo_ref[...], h_hi))
    scores = _sigmoid(logits)
    _, gw, hits = _route(scores, scores + rb_ref[...])
    lpos, nchunk, loff, cnt = _tile_positions(hits)
    gw_ref[...] = gw
    lpos_ref[...] = lpos
    nch_ref[...] = nchunk
    loff_ref[...] = loff
    run = run_ref[...]
    runb_ref[...] = jnp.broadcast_to(run, (N_EXPERTS, LANES))
    run_ref[...] = run + cnt


def _mix(x2d, o_mla, ret, ga, gb, w_o, w_ret_o, w_out, ln1_g, ln1_b, w_router, router_bias, run0, *, tm):
    T, D = x2d.shape
    assert D == ROW_TILE * LANES
    wrt = w_router.T
    wrt_hi, wrt_lo = _split_hi_lo(wrt)
    rb = router_bias.reshape(N_EXPERTS, 1).astype(F32)
    g1, b1 = ln1_g.reshape(1, D), ln1_b.reshape(1, D)
    wr = w_ret_o.astype(BF16)
    wout = w_out.astype(BF16)

    def row(i):
        return (i, 0)

    def full(a):
        return pl.BlockSpec(a.shape, lambda i: (0,) * a.ndim)

    route_spec = pl.BlockSpec((ROUTE_ROWS, tm), lambda i: (0, i))
    return pl.pallas_call(
        _mix_kernel,
        grid=(T // tm,),
        in_specs=[pl.BlockSpec((tm, D), row), pl.BlockSpec((tm, o_mla.shape[1]), row),
                  pl.BlockSpec((tm, ret.shape[1]), row), pl.BlockSpec((tm, D), row), pl.BlockSpec((tm, D), row),
                  full(w_o), full(wr), full(wout), full(g1), full(b1), full(wrt_hi), full(wrt_lo), full(rb),
                  full(run0)],
        out_specs=[pl.BlockSpec((tm, D), row), route_spec, route_spec,
                   pl.BlockSpec((N_EXPERTS, LANES), row), pl.BlockSpec((N_EXPERTS, LANES), row),
                   pl.BlockSpec((N_EXPERTS, LANES), row), pl.BlockSpec((N_EXPERTS, 1), lambda i: (0, 0))],
        out_shape=[jax.ShapeDtypeStruct((T, D), F32),
                   jax.ShapeDtypeStruct((ROUTE_ROWS, T), F32),
                   jax.ShapeDtypeStruct((ROUTE_ROWS, T), jnp.int32),
                   jax.ShapeDtypeStruct((T // tm * N_EXPERTS, LANES), F32),
                   jax.ShapeDtypeStruct((T // tm * N_EXPERTS, LANES), F32),
                   jax.ShapeDtypeStruct((T // tm * N_EXPERTS, LANES), F32),
                   jax.ShapeDtypeStruct((N_EXPERTS, 1), F32)],
        compiler_params=_cparams(("arbitrary",)),
        name="mix",
    )(x2d, o_mla, ret, ga, gb, w_o, wr, wout, g1, b1, wrt_hi, wrt_lo, rb, run0)


def _n_row_blocks(n_tokens):
    return (n_tokens * TOP_K + N_EXPERTS * (MOE_BLOCK + SEG_CAP - 1) + MOE_BLOCK - 1) // MOE_BLOCK


def _block_plan(counts, n_blocks):
    counts = counts.reshape(N_EXPERTS).astype(jnp.int32)
    pad_len = jnp.where(counts > 0, (counts + SEG_CAP + MOE_BLOCK - 1) // MOE_BLOCK * MOE_BLOCK, 0)
    pad_end = jnp.cumsum(pad_len)
    pad_start = pad_end - pad_len
    first_row = jnp.arange(n_blocks, dtype=jnp.int32) * MOE_BLOCK
    blk_e = jnp.minimum(jnp.sum((pad_end[None, :] <= first_row[:, None]).astype(jnp.int32), axis=1), N_EXPERTS - 1)
    n_used = (pad_end[-1:] // MOE_BLOCK).astype(jnp.int32)
    return pad_start.astype(jnp.int32), pad_len.astype(jnp.int32), blk_e, n_used


def _segment_tables(pad_start, tables, n_tiles):
    nch, loff, runb = (t[:, 0].reshape(n_tiles, 1, N_EXPERTS).astype(jnp.int32) for t in tables)
    return nch, loff, pad_start[None, None, :] + runb


def _chunk_copies(nch_ref, lo_ref, dst_ref, local_buf, rows_hbm, sem, *, to_hbm, fn):
    rows = SEG_CAP * ROW_TILE

    def per_expert(e, c):
        lo, dst = lo_ref[0, 0, e], dst_ref[0, 0, e]

        def per_chunk(j, c2):
            loc = local_buf.at[pl.ds((lo + j * SEG_CAP) * ROW_TILE, rows), :]
            far = rows_hbm.at[pl.ds((dst + j * SEG_CAP) * ROW_TILE, rows), :]
            fn(pltpu.make_async_copy(loc, far, sem) if to_hbm else pltpu.make_async_copy(far, loc, sem))
            return c2

        lax.fori_loop(0, nch_ref[0, 0, e], per_chunk, 0)
        return c

    lax.fori_loop(0, N_EXPERTS, per_expert, 0)


def _two_group_specs(tm, width, n_main):
    return (pl.BlockSpec((tm, width), lambda i: (jnp.minimum(i, n_main - 1), 0)),
            pl.BlockSpec((tm, width), lambda i: (jnp.maximum(i - n_main, 0), 0)))


def _dispatch_kernel(ps_ref, pl_ref, cnt_ref, nu_ref, nch_ref, lo_ref, dst_ref, nchp_ref, lop_ref, dstp_ref,
                     lpos_ref, hp_ref, hs_ref, xs_hbm, sbuf, zbuf, sem, zsem, *, tm, n_main, n_blocks):
    i = pl.program_id(0)
    blk_rows = MOE_BLOCK * ROW_TILE
    rl = sbuf.shape[1] // ROW_TILE

    @pl.when(i == 0)
    def _():
        zbuf[...] = jnp.zeros(zbuf.shape, F32)

        def zcopy(block_row):
            return pltpu.make_async_copy(zbuf, xs_hbm.at[pl.ds(block_row * ROW_TILE, blk_rows), :], zsem)

        def each_expert(fn):
            def body(e, c):
                end = ps_ref[e] + pl_ref[e]

                @pl.when(pl_ref[e] > 0)
                def _():
                    fn(zcopy(end - MOE_BLOCK))

                @pl.when(pl_ref[e] - cnt_ref[e] > MOE_BLOCK)
                def _():
                    fn(zcopy(end - 2 * MOE_BLOCK))
                return c
            lax.fori_loop(0, N_EXPERTS, body, 0)

        def each_tail(fn):
            def body(j, c):
                fn(zcopy(j * MOE_BLOCK))
                return c
            lax.fori_loop(nu_ref[0], n_blocks, body, 0)

        each_expert(lambda cp: cp.start())
        each_tail(lambda cp: cp.start())
        each_expert(lambda cp: cp.wait())
        each_tail(lambda cp: cp.wait())

    slot = i % 2
    h = jnp.where(i < n_main, hp_ref[...], hs_ref[...]).astype(BF16)
    rows = lax.broadcasted_iota(jnp.int32, (rl, tm), 0)
    perm = jnp.zeros((rl, tm), F32)
    for k in range(TOP_K):
        perm = jnp.where(rows == lpos_ref[k:k + 1, :], 1.0, perm)
    xsort = _dot(perm.astype(BF16), h)
    for s in range(ROW_TILE):
        sbuf[slot, pl.ds(s, rl, stride=ROW_TILE), :] = xsort[:, s * LANES:(s + 1) * LANES]

    @pl.when(i > 0)
    def _():
        _chunk_copies(nchp_ref, lop_ref, dstp_ref, sbuf.at[1 - slot], xs_hbm, sem.at[1 - slot], to_hbm=True,
                      fn=lambda cp: cp.wait())

    copies = functools.partial(_chunk_copies, nch_ref, lo_ref, dst_ref, sbuf.at[slot], xs_hbm, sem.at[slot],
                               to_hbm=True)
    copies(fn=lambda cp: cp.start())

    @pl.when(i == pl.num_programs(0) - 1)
    def _():
        copies(fn=lambda cp: cp.wait())


def _dispatch(plan, counts, seg, lpos, h_p, h_s, n_blocks, *, tm):
    pad_start, pad_len, _, n_used = plan
    nch, loff, dst = seg
    n_tiles = nch.shape[0]
    n_main = h_p.shape[0] // tm
    D = h_p.shape[1]
    rl = _sorted_rows(tm)
    smem = pl.BlockSpec(memory_space=pltpu.SMEM)
    seg_spec = pl.BlockSpec((1, 1, N_EXPERTS), lambda i: (i, 0, 0), memory_space=pltpu.SMEM)
    prev_spec = pl.BlockSpec((1, 1, N_EXPERTS), lambda i: (jnp.maximum(i - 1, 0), 0, 0), memory_space=pltpu.SMEM)
    return pl.pallas_call(
        functools.partial(_dispatch_kernel, tm=tm, n_main=n_main, n_blocks=n_blocks),
        grid=(n_tiles,),
        in_specs=[smem, smem, smem, smem, seg_spec, seg_spec, seg_spec, prev_spec, prev_spec, prev_spec,
                  pl.BlockSpec((ROUTE_ROWS, tm), lambda i: (0, i)), *_two_group_specs(tm, D, n_main)],
        out_specs=pl.BlockSpec(memory_space=pl.ANY),
        out_shape=jax.ShapeDtypeStruct((n_blocks * MOE_BLOCK * ROW_TILE, LANES), F32),
        scratch_shapes=[pltpu.VMEM((2, rl * ROW_TILE, LANES), F32),
                        pltpu.VMEM((MOE_BLOCK * ROW_TILE, LANES), F32),
                        pltpu.SemaphoreType.DMA((2,)), pltpu.SemaphoreType.DMA(())],
        compiler_params=_cparams(("arbitrary",)),
        name="dispatch",
    )(pad_start, pad_len, counts, n_used, nch, loff, dst, nch, loff, dst, lpos, h_p, h_s)


def _from_row_tiles(ref, n_rows, base=0):
    return jnp.concatenate([ref[pl.ds(base * ROW_TILE + s, n_rows, stride=ROW_TILE), :]
                            for s in range(ROW_TILE)], axis=1)


def _experts_kernel(be_ref, nu_ref, x_ref, wg_ref, wu_ref, wd_ref, y_ref, wgb, wub, wdb):
    j = pl.program_id(0)
    used = j < nu_ref[0]

    @pl.when(used & ((j == 0) | (be_ref[j] != be_ref[jnp.maximum(j - 1, 0)])))
    def _():
        wgb[...] = wg_ref[...].astype(BF16)
        wub[...] = wu_ref[...].astype(BF16)
        wdb[...] = wd_ref[...].astype(BF16)

    @pl.when(used)
    def _():
        x = _from_row_tiles(x_ref, MOE_BLOCK).astype(BF16)
        gate = _dot(x, wgb[...])
        up = _dot(x, wub[...])
        hid = (gate * _sigmoid(gate) * up).astype(BF16)
        y = _dot(hid, wdb[...])
        for s in range(ROW_TILE):
            y_ref[pl.ds(s, MOE_BLOCK, stride=ROW_TILE), :] = y[:, s * LANES:(s + 1) * LANES]

    @pl.when(pl.program_id(0) >= nu_ref[0])
    def _():
        y_ref[...] = jnp.zeros(y_ref.shape, F32)


def _experts(xs, blk_e, n_used, w_gate, w_up, w_down):
    n_blocks = blk_e.shape[0]
    D = w_gate.shape[1]

    def blk(j, be, nu):
        return (jnp.minimum(j, nu[0] - 1), 0)

    def out_blk(j, be, nu):
        return (j, 0)

    def wsel(j, be, nu):
        return (be[jnp.minimum(j, nu[0] - 1)], 0, 0)

    rows = MOE_BLOCK * ROW_TILE
    grid_spec = pltpu.PrefetchScalarGridSpec(
        num_scalar_prefetch=2,
        grid=(n_blocks,),
        in_specs=[pl.BlockSpec((rows, LANES), blk),
                  pl.BlockSpec((None, D, EXPERT_FF), wsel),
                  pl.BlockSpec((None, D, EXPERT_FF), wsel),
                  pl.BlockSpec((None, EXPERT_FF, D), wsel)],
        out_specs=pl.BlockSpec((rows, LANES), out_blk),
        scratch_shapes=[pltpu.VMEM((D, EXPERT_FF), BF16), pltpu.VMEM((D, EXPERT_FF), BF16),
                        pltpu.VMEM((EXPERT_FF, D), BF16)],
    )
    return pl.pallas_call(
        _experts_kernel,
        grid_spec=grid_spec,
        out_shape=jax.ShapeDtypeStruct(xs.shape, F32),
        compiler_params=_cparams(("arbitrary",)),
        name="experts",
    )(blk_e, n_used, xs, w_gate, w_up, w_down)


def _combine_kernel(nch_ref, lo_ref, dst_ref, lpos_ref, gw_ref, hp_ref, hs_ref, y_hbm,
                    wsg_ref, wsu_ref, wsd_ref, g2_ref, b2_ref, op_ref, os_ref, gbuf, sem, *, tm, n_main):
    i = pl.program_id(0)
    rl = gbuf.shape[0] // ROW_TILE

    @pl.when(i == 0)
    def _():
        gbuf[...] = jnp.zeros(gbuf.shape, F32)

    copies = functools.partial(_chunk_copies, nch_ref, lo_ref, dst_ref, gbuf, y_hbm, sem, to_hbm=False)
    copies(fn=lambda cp: cp.start())
    h = jnp.where(i < n_main, hp_ref[...], hs_ref[...])
    hb = h.astype(BF16)
    gate = _dot(hb, wsg_ref[...])
    up = _dot(hb, wsu_ref[...])
    ffn = _dot((gate * _sigmoid(gate) * up).astype(BF16), wsd_ref[...])
    cols = lax.broadcasted_iota(jnp.int32, (tm, rl), 1)
    place = jnp.zeros((tm, rl), F32)
    for k in range(TOP_K):
        place = jnp.where(cols == lpos_ref[:, k:k + 1], gw_ref[:, k:k + 1], place)
    copies(fn=lambda cp: cp.wait())
    ffn = ffn + _dot(place.astype(BF16), _from_row_tiles(gbuf, rl).astype(BF16))
    out = _layer_norm(DEEPNORM_ALPHA * h + ffn, g2_ref[...], b2_ref[...])

    @pl.when(i < n_main)
    def _():
        op_ref[...] = out

    @pl.when(i >= n_main)
    def _():
        os_ref[...] = out


def _combine(seg, lpos_t, gw_t, h_p, h_s, ys, w_sh_gate, w_sh_up, w_sh_down, ln2_g, ln2_b, *, tm):
    nch, loff, dst = seg
    n_tiles = nch.shape[0]
    D = h_p.shape[1]
    n_main = h_p.shape[0] // tm
    rl = _sorted_rows(tm)
    wsg, wsu, wsd = w_sh_gate.astype(BF16), w_sh_up.astype(BF16), w_sh_down.astype(BF16)
    g2, b2 = ln2_g.reshape(1, D), ln2_b.reshape(1, D)

    def full(a):
        return pl.BlockSpec(a.shape, lambda i: (0,) * a.ndim)

    seg_spec = pl.BlockSpec((1, 1, N_EXPERTS), lambda i: (i, 0, 0), memory_space=pltpu.SMEM)
    tok = pl.BlockSpec((tm, ROUTE_ROWS), lambda i: (i, 0))
    groups = _two_group_specs(tm, D, n_main)
    return pl.pallas_call(
        functools.partial(_combine_kernel, tm=tm, n_main=n_main),
        grid=(n_tiles,),
        in_specs=[seg_spec, seg_spec, seg_spec, tok, tok, *groups, pl.BlockSpec(memory_space=pl.ANY),
                  full(wsg), full(wsu), full(wsd), full(g2), full(b2)],
        out_specs=list(groups),
        out_shape=[jax.ShapeDtypeStruct(h_p.shape, F32), jax.ShapeDtypeStruct(h_s.shape, F32)],
        scratch_shapes=[pltpu.VMEM((rl * ROW_TILE, LANES), F32), pltpu.SemaphoreType.DMA(())],
        compiler_params=_cparams(("arbitrary",)),
        name="combine",
    )(nch, loff, dst, lpos_t, gw_t, h_p, h_s, ys, wsg, wsu, wsd, g2, b2)


def kernel(x_prompt, x_sample, cache_ckv, cache_krope, state_ret, page_table, w_in, q_norm_g, w_q_up, kv_norm_g,
           w_uk, w_uv, ret_gn_g, w_mla_o, w_ret_o, w_out, ln1_g, ln1_b, w_router, router_bias,
           w_exp_gate, w_exp_up, w_exp_down, w_sh_gate, w_sh_up, w_sh_down, ln2_g, ln2_b):
    B, S, D = x_prompt.shape
    DB, Q, _ = x_sample.shape
    Tp, Ts = B * S, DB * Q
    past_len = page_table.shape[1] * cache_ckv.shape[1]
    w_small, w_ret, w_gate, w_q, w_k, w_v, w_o = _prep_weights(w_in, w_q_up, w_uk, w_uv, w_mla_o)
    pw = (w_small, w_ret, w_gate, w_q, w_k, w_v)

    tab_p = _rope_tables(jnp.arange(S))
    (qh, ckv_p, kr_p, _, rq, rk, rv, rg, ga, gb, kh, vh) = _proj(
        x_prompt.reshape(Tp, D), tab_p, pw, q_norm_g, kv_norm_g, tm=512, ret_dtype=BF16, decode=False)
    o_mla = _attn_p(qh, kh, vh, B, S, tq=512, tk=512, hps=4)
    ret, ret_state_p = _ret_p(rq, rk, rv, rg, ret_gn_g, B, S, cb=4)
    mix_w = (w_o, w_ret_o, w_out, ln1_g, ln1_b, w_router, router_bias)
    tmoe = min(256, Ts)
    h_p, gw_p, lpos_p, *seg_p, cnt_p = _mix(x_prompt.reshape(Tp, D), o_mla, ret, ga, gb, *mix_w,
                                            jnp.zeros((N_EXPERTS, 1), F32), tm=tmoe)

    tab_s = _rope_tables(jnp.tile(past_len + jnp.arange(Q), DB))
    wabs, wsel, wuv = _absorb_weights(w_uk, w_uv)
    (_, ckv_s, kr_s, krp_s, rq_s, rk_s, rv_s, rg_s, ga_s, gb_s, ql_s, qr_s) = _proj(
        x_sample.reshape(Ts, D), tab_s, pw[:4] + (wabs, wsel), q_norm_g, kv_norm_g, tm=Ts, ret_dtype=F32,
        decode=True)

    def rows_by_head(a):
        return a.reshape(DB, Q, MLA_HEADS, LANES).transpose(0, 2, 1, 3).reshape(DB, MLA_HEADS * Q, LANES)

    o_lat = _attn_s(page_table, rows_by_head(ql_s), rows_by_head(qr_s), ckv_s, krp_s, cache_ckv, cache_krope,
                    tk=min(1024, past_len))
    o_lat_tok = o_lat.reshape(DB, MLA_HEADS, Q, LANES).transpose(0, 2, 1, 3).reshape(Ts, MLA_HEADS * LANES)
    o_mla_s = _lat_up(o_lat_tok, wuv)
    ret_s, ret_state_s = _ret_s(rq_s, rk_s, rv_s, rg_s, state_ret, ret_gn_g, gb=min(16, DB))
    h_s, gw_s, lpos_s, *seg_s, cnt = _mix(x_sample.reshape(Ts, D), o_mla_s, ret_s, ga_s, gb_s, *mix_w,
                                          cnt_p, tm=tmoe)

    n_blocks = _n_row_blocks(Tp + Ts)
    plan = _block_plan(cnt, n_blocks)
    n_tiles = (Tp + Ts) // tmoe
    seg = _segment_tables(plan[0], [jnp.concatenate(t, axis=0) for t in zip(seg_p, seg_s)], n_tiles)
    lpos = jnp.concatenate([lpos_p, lpos_s], axis=1)
    gw = jnp.concatenate([gw_p, gw_s], axis=1)
    xs = _dispatch(plan, cnt.reshape(N_EXPERTS).astype(jnp.int32), seg, lpos, h_p, h_s, n_blocks, tm=tmoe)
    ys = _experts(xs, plan[2], plan[3], w_exp_gate, w_exp_up, w_exp_down)
    y_p, y_s = _combine(seg, lpos.T, gw.T, h_p, h_s, ys, w_sh_gate, w_sh_up, w_sh_down, ln2_g, ln2_b, tm=tmoe)

    return (y_p.reshape(B, S, D), y_s.reshape(DB, Q, D),
            ckv_p.reshape(B, S, -1), kr_p.reshape(B, S, -1), ret_state_p,
            ckv_s.reshape(DB, Q, -1), kr_s.reshape(DB, Q, -1), ret_state_s)
```

```python
import functools
import math

import numpy as np
import jax
import jax.numpy as jnp
from jax import lax
from jax.experimental import pallas as pl
from jax.experimental.pallas import tpu as pltpu

F32 = jnp.float32
BF16 = jnp.bfloat16

MLA_HEADS = 8
MLA_NOPE = 64
MLA_ROPE = 32
MLA_V = 64
MLA_Q_LORA = 256
MLA_KV_LORA = 128
MLA_SCALE = (MLA_NOPE + MLA_ROPE) ** -0.5
Q_SCALE = MLA_SCALE * math.log2(math.e)
RET_HEADS = 4
RET_DK = 128
RET_DV = 128
RET_CHUNK = 128
N_EXPERTS = 64
N_GROUPS = 8
TOPK_GROUPS = 4
TOP_K = 6
EXPERT_FF = 256
SHARED_FF = 256
ROUTED_SCALE = 2.5
MOE_BLOCK = 512
ROPE_BASE = 10000.0
LN_EPS = 1e-5
RMS_EPS = 1e-6
DEPTH = 1
DEEPNORM_ALPHA = (2 * DEPTH) ** 0.25

LANES = 128
ROW_TILE = 8
HALF_ROPE = MLA_ROPE // 2
VMEM_LIMIT = 56 * 1024 * 1024
NEG_INF = float("-inf")


def _cparams(sem):
    return pltpu.CompilerParams(dimension_semantics=sem, vmem_limit_bytes=VMEM_LIMIT)


def _dot(a, b):
    return jnp.dot(a, b, preferred_element_type=F32)


def _dot_nt(a, b):
    return lax.dot_general(a, b, (((1,), (1,)), ((), ())), preferred_element_type=F32)


def _dot_tn(a, b):
    return lax.dot_general(a, b, (((0,), (0,)), ((), ())), preferred_element_type=F32)


N_TAB = 8


def _rope_tables(pos):
    L = pos.shape[0]
    posf = pos.astype(F32)[:, None]
    half_r = RET_DK // 2
    ang_r = posf * (ROPE_BASE ** (-jnp.arange(half_r, dtype=F32) / half_r))[None, :]
    cos_r = jnp.concatenate([jnp.cos(ang_r), jnp.cos(ang_r)], axis=1)
    sin_r = jnp.concatenate([-jnp.sin(ang_r), jnp.sin(ang_r)], axis=1)
    ang_m = posf * (ROPE_BASE ** (-jnp.arange(HALF_ROPE, dtype=F32) / HALF_ROPE))[None, :]
    c, s = jnp.cos(ang_m), jnp.sin(ang_m)
    z16 = jnp.zeros((L, HALF_ROPE), F32)

    def place(parts, offset):
        body = jnp.concatenate(parts, axis=1)
        return jnp.concatenate([jnp.zeros((L, offset), F32), body,
                                jnp.zeros((L, LANES - offset - body.shape[1]), F32)], axis=1)

    cos_k = place([c, c], 0)
    sinp_k = place([z16, s], 0)
    sinm_k = place([-s, z16], 0)
    ones = jnp.ones((L, MLA_NOPE), F32)
    cos_q = jnp.concatenate([ones, c, c, jnp.zeros((L, LANES - MLA_NOPE - MLA_ROPE), F32)], axis=1) * Q_SCALE
    sinp_q = place([z16, s], MLA_NOPE) * Q_SCALE
    sinm_q = place([-s, z16], MLA_NOPE) * Q_SCALE
    return jnp.concatenate([cos_r, sin_r, cos_k, sinp_k, sinm_k, cos_q, sinp_q, sinm_q], axis=1)


def _prep_weights(w_in, w_q_up, w_uk, w_uv, w_mla_o):
    d = w_in.shape[0]
    c_q, c_kv, c_kr = MLA_Q_LORA, MLA_KV_LORA, MLA_ROPE
    o_ret = c_q + c_kv + c_kr
    n_ret = 2 * RET_HEADS * RET_DK + 2 * RET_HEADS * RET_DV
    w_small = jnp.concatenate([w_in[:, :o_ret], jnp.zeros((d, LANES - c_kr), F32)], axis=1).astype(BF16)
    w_ret = w_in[:, o_ret:o_ret + n_ret].astype(BF16)
    w_gate = w_in[:, o_ret + n_ret:].astype(BF16)
    hd = MLA_NOPE + MLA_ROPE
    w_q = jnp.pad(w_q_up.reshape(c_q, MLA_HEADS, hd), ((0, 0), (0, 0), (0, LANES - hd)))
    w_q = w_q.reshape(c_q, MLA_HEADS * LANES).astype(BF16)
    wk_top = jnp.pad(w_uk.reshape(c_kv, MLA_HEADS, MLA_NOPE), ((0, 0), (0, 0), (0, LANES - MLA_NOPE)))
    place = jnp.zeros((LANES, MLA_HEADS, LANES), F32)
    idx = jnp.arange(MLA_ROPE)
    place = place.at[idx, :, MLA_NOPE + idx].set(1.0)
    w_k = jnp.concatenate([wk_top.reshape(c_kv, -1), place.reshape(LANES, -1)], axis=0).astype(BF16)
    w_v = jnp.pad(w_uv.reshape(c_kv, MLA_HEADS, MLA_V), ((0, 0), (0, 0), (0, LANES - MLA_V)))
    w_v = w_v.reshape(c_kv, MLA_HEADS * LANES).astype(BF16)
    w_o = jnp.pad(w_mla_o.reshape(MLA_HEADS, MLA_V, -1), ((0, 0), (0, LANES - MLA_V), (0, 0)))
    w_o = w_o.reshape(MLA_HEADS * LANES, -1).astype(BF16)
    return w_small, w_ret, w_gate, w_q, w_k, w_v, w_o


def _rms_norm(x, g):
    inv = lax.rsqrt(jnp.mean(x * x, axis=-1, keepdims=True) + RMS_EPS)
    return x * inv * g


def _layer_norm(x, g, b):
    mu = jnp.mean(x, axis=-1, keepdims=True)
    xc = x - mu
    var = jnp.mean(xc * xc, axis=-1, keepdims=True)
    return xc * lax.rsqrt(var + LN_EPS) * g + b


def _sigmoid(x):
    return 1.0 / (1.0 + jnp.exp(-x))


def _proj_kernel(x_ref, tab_ref, wsm_ref, wret_ref, wg_ref, qg_ref, wq_ref, kvg_ref, wa_ref, wb_ref,
                 qh_ref, ckv_ref, kr_ref, krp_ref, rq_ref, rk_ref, rv_ref, rg_ref, ga_ref, gb_ref,
                 oa_ref, ob_ref, *, decode):
    xb = x_ref[...].astype(BF16)

    def tab(i):
        return tab_ref[:, i * LANES:(i + 1) * LANES]

    small = _dot(xb, wsm_ref[...])
    cq = small[:, :MLA_Q_LORA]
    ckv = small[:, MLA_Q_LORA:MLA_Q_LORA + MLA_KV_LORA]
    krb = small[:, MLA_Q_LORA + MLA_KV_LORA:]
    q = _dot(_rms_norm(cq, qg_ref[...]).astype(BF16), wq_ref[...])
    cos_q, sinp_q, sinm_q = tab(5), tab(6), tab(7)
    for h in range(MLA_HEADS):
        blk = q[:, h * LANES:(h + 1) * LANES]
        rot = (blk * cos_q + pltpu.roll(blk, HALF_ROPE, 1) * sinp_q
               + pltpu.roll(blk, LANES - HALF_ROPE, 1) * sinm_q)
        rot = rot.astype(BF16)
        qh_ref[:, h * LANES:(h + 1) * LANES] = rot
        if decode:
            oa_ref[:, h * LANES:(h + 1) * LANES] = _dot(rot, wa_ref[h]).astype(BF16)
            ob_ref[:, h * LANES:(h + 1) * LANES] = _dot(rot, wb_ref[...]).astype(BF16)
    ckvn = _rms_norm(ckv, kvg_ref[...])
    ckv_ref[...] = ckvn
    krr = (krb * tab(2) + pltpu.roll(krb, HALF_ROPE, 1) * tab(3)
           + pltpu.roll(krb, LANES - HALF_ROPE, 1) * tab(4))
    kr_ref[...] = krr[:, :MLA_ROPE]
    krp_ref[...] = krr
    if not decode:
        kcat = jnp.concatenate([ckvn, krr], axis=1).astype(BF16)
        oa_ref[...] = _dot(kcat, wa_ref[...]).astype(BF16)
        vv = _dot(kcat[:, :MLA_KV_LORA], wb_ref[...])
        lane = lax.broadcasted_iota(jnp.int32, vv.shape, 1) % LANES
        ob_ref[...] = jnp.where(lane == MLA_V, 1.0, vv).astype(BF16)

    r = _dot(xb, wret_ref[...])
    cos_r, sin_r = tab(0), tab(1)
    nq = RET_HEADS * RET_DK
    for h in range(RET_HEADS):
        sl = slice(h * RET_DK, (h + 1) * RET_DK)
        a = r[:, sl]
        rq_ref[:, sl] = (a * cos_r + pltpu.roll(a, RET_DK // 2, 1) * sin_r).astype(rq_ref.dtype)
        b = r[:, nq + h * RET_DK:nq + (h + 1) * RET_DK]
        rk_ref[:, sl] = ((b * cos_r + pltpu.roll(b, RET_DK // 2, 1) * sin_r)
                         * (RET_DK ** -0.5)).astype(rk_ref.dtype)
    rv_ref[...] = r[:, 2 * nq:2 * nq + RET_HEADS * RET_DV].astype(rv_ref.dtype)
    rg = r[:, 2 * nq + RET_HEADS * RET_DV:]
    rg_ref[...] = (rg * _sigmoid(rg)).astype(rg_ref.dtype)

    g = _dot(xb, wg_ref[...])
    d = ga_ref.shape[1]
    ga_ref[...] = _sigmoid(g[:, :d]).astype(ga_ref.dtype)
    gb_ref[...] = _sigmoid(g[:, d:]).astype(gb_ref.dtype)


def _proj(x2d, tab, weights, q_norm_g, kv_norm_g, *, tm, ret_dtype, decode):
    T, D = x2d.shape
    w_small, w_ret, w_gate, w_q, w_a, w_b = weights
    n_tab = tab.shape[0] // tm
    nr = RET_HEADS * RET_DK
    hp = MLA_HEADS * LANES

    def row(i):
        return (i, 0)

    def full(a):
        return pl.BlockSpec(a.shape, lambda i: (0,) * a.ndim)

    out_shapes = [
        jax.ShapeDtypeStruct((T, hp), BF16),
        jax.ShapeDtypeStruct((T, MLA_KV_LORA), F32),
        jax.ShapeDtypeStruct((T, MLA_ROPE), F32),
        jax.ShapeDtypeStruct((T, LANES), F32),
        jax.ShapeDtypeStruct((T, nr), ret_dtype),
        jax.ShapeDtypeStruct((T, nr), ret_dtype),
        jax.ShapeDtypeStruct((T, nr), ret_dtype),
        jax.ShapeDtypeStruct((T, nr), BF16),
        jax.ShapeDtypeStruct((T, D), BF16),
        jax.ShapeDtypeStruct((T, D), BF16),
        jax.ShapeDtypeStruct((T, hp), BF16),
        jax.ShapeDtypeStruct((T, hp), BF16),
    ]
    out_specs = [pl.BlockSpec((tm, s.shape[1]), row) for s in out_shapes]
    qg = q_norm_g.reshape(1, -1)
    kvg = kv_norm_g.reshape(1, -1)
    in_specs = [pl.BlockSpec((tm, D), row),
                pl.BlockSpec((tm, N_TAB * LANES), lambda i: (i % n_tab, 0)),
                full(w_small), full(w_ret), full(w_gate), full(qg), full(w_q), full(kvg), full(w_a), full(w_b)]
    return pl.pallas_call(
        functools.partial(_proj_kernel, decode=decode),
        grid=(T // tm,),
        in_specs=in_specs,
        out_specs=out_specs,
        out_shape=out_shapes,
        compiler_params=_cparams(("parallel",)),
        name="proj",
    )(x2d, tab, w_small, w_ret, w_gate, qg, w_q, kvg, w_a, w_b)


def _attn_p_kernel(q_ref, k_ref, v_ref, o_ref, m_ref, acc_ref, *, tq, tk, hps):
    i = pl.program_id(2)
    m_ref[...] = jnp.full(m_ref.shape, NEG_INF, F32)
    acc_ref[...] = jnp.zeros(acc_ref.shape, F32)

    def step(j, masked):
        for hh in range(hps):
            cols = slice(hh * LANES, (hh + 1) * LANES)
            k = k_ref[pl.ds(j * tk, tk), cols]
            v = v_ref[pl.ds(j * tk, tk), cols]
            s = _dot_nt(q_ref[:, cols], k)
            if masked:
                row = lax.broadcasted_iota(jnp.int32, (tq, tk), 0) + i * tq
                col = lax.broadcasted_iota(jnp.int32, (tq, tk), 1) + j * tk
                s = jnp.where(col <= row, s, NEG_INF)
            m_prev = m_ref[hh]
            m_new = jnp.maximum(m_prev, jnp.max(s, axis=1, keepdims=True))
            p = jnp.concatenate([jnp.exp2(s[:, c * LANES:(c + 1) * LANES] - m_new)
                                 for c in range(tk // LANES)], axis=1)
            acc_ref[hh] = jnp.exp2(m_prev - m_new) * acc_ref[hh] + _dot(p.astype(BF16), v)
            m_ref[hh] = m_new

    n_full = (i * tq) // tk

    def body(j, c):
        step(j, False)
        return c

    lax.fori_loop(0, n_full, body, 0)
    for jj in range(tq // tk):
        step(n_full + jj, True)
    for hh in range(hps):
        acc = acc_ref[hh]
        o_ref[:, hh * LANES:(hh + 1) * LANES] = (acc / acc[:, MLA_V:MLA_V + 1]).astype(o_ref.dtype)


def _attn_p(qh, kh, vh, B, S, *, tq, tk, hps):
    assert tq % tk == 0 and MLA_HEADS % hps == 0
    nq = S // tq
    hp = MLA_HEADS * LANES
    kh3 = kh.reshape(B, S, hp)
    vh3 = vh.reshape(B, S, hp)
    return pl.pallas_call(
        functools.partial(_attn_p_kernel, tq=tq, tk=tk, hps=hps),
        grid=(B, MLA_HEADS // hps, nq),
        in_specs=[pl.BlockSpec((tq, hps * LANES), lambda b, h, i: (b * nq + i, h)),
                  pl.BlockSpec((None, S, hps * LANES), lambda b, h, i: (b, 0, h)),
                  pl.BlockSpec((None, S, hps * LANES), lambda b, h, i: (b, 0, h))],
        out_specs=pl.BlockSpec((tq, hps * LANES), lambda b, h, i: (b * nq + i, h)),
        out_shape=jax.ShapeDtypeStruct((B * S, hp), BF16),
        scratch_shapes=[pltpu.VMEM((hps, tq, LANES), F32), pltpu.VMEM((hps, tq, LANES), F32)],
        compiler_params=_cparams(("parallel", "parallel", "arbitrary")),
        name="attn_p",
    )(qh, kh3, vh3)


def _ret_consts(C):
    lg = jnp.log(1.0 - 2.0 ** (-5.0 - jnp.arange(RET_HEADS, dtype=F32)))
    idx = jnp.arange(C, dtype=F32)
    diff = idx[:, None] - idx[None, :]
    dmask = jnp.where(diff >= 0, jnp.exp(jnp.maximum(diff, 0.0)[None] * lg[:, None, None]), 0.0)
    q_dec = jnp.exp((idx[None, :] + 1.0) * lg[:, None])[:, :, None]
    k_dec = jnp.exp((C - 1.0 - idx)[None, :] * lg[:, None])[:, :, None]
    s_dec = jnp.exp(C * lg)
    return dmask, q_dec, k_dec, s_dec


def _head_norm_gate(o, gate, gn):
    mu = jnp.mean(o, axis=-1, keepdims=True)
    oc = o - mu
    var = jnp.mean(oc * oc, axis=-1, keepdims=True)
    return gate * (oc * lax.rsqrt(var + LN_EPS) * gn)


def _ret_p_kernel(sdec_ref, q_ref, k_ref, v_ref, g_ref, dm_ref, qd_ref, kd_ref, gn_ref,
                  o_ref, s_ref, *, nb, cb):
    C = RET_CHUNK

    @pl.when(pl.program_id(0) == 0)
    def _():
        s_ref[...] = jnp.zeros(s_ref.shape, F32)

    for c in range(cb):
        rows = slice(c * C, (c + 1) * C)
        for b in range(nb):
            for h in range(RET_HEADS):
                cols = slice(h * RET_DK, (h + 1) * RET_DK)
                q = q_ref[b, rows, cols]
                k = k_ref[b, rows, cols]
                v = v_ref[b, rows, cols]
                state = s_ref[b, h]
                att = _dot_nt(q, k) * dm_ref[h]
                o = _dot(att.astype(BF16), v) + _dot(q, state.astype(BF16)) * qd_ref[h]
                kd = (k.astype(F32) * kd_ref[h]).astype(BF16)
                s_ref[b, h] = state * sdec_ref[h] + _dot_tn(kd, v)
                gate = g_ref[b, rows, cols].astype(F32)
                o_ref[b, rows, cols] = _head_norm_gate(o, gate, gn_ref[:, cols]).astype(o_ref.dtype)


def _ret_p(rq, rk, rv, rg, ret_gn_g, B, S, *, cb):
    C = RET_CHUNK
    nr = RET_HEADS * RET_DK
    dmask, q_dec, k_dec, s_dec = _ret_consts(C)
    blk = pl.BlockSpec((B, cb * C, nr), lambda g: (0, g, 0))

    def full(a):
        return pl.BlockSpec(a.shape, lambda g: (0,) * a.ndim)

    gn = ret_gn_g.reshape(1, nr)
    args = [a.reshape(B, S, nr) for a in (rq, rk, rv, rg)]
    ret, state = pl.pallas_call(
        functools.partial(_ret_p_kernel, nb=B, cb=cb),
        grid=(S // (cb * C),),
        in_specs=[pl.BlockSpec(memory_space=pltpu.SMEM), blk, blk, blk, blk,
                  full(dmask), full(q_dec), full(k_dec), full(gn)],
        out_specs=[blk, pl.BlockSpec((B, RET_HEADS, RET_DK, RET_DV), lambda g: (0, 0, 0, 0))],
        out_shape=[jax.ShapeDtypeStruct((B, S, nr), BF16),
                   jax.ShapeDtypeStruct((B, RET_HEADS, RET_DK, RET_DV), F32)],
        compiler_params=_cparams(("arbitrary",)),
        name="ret_p",
    )(s_dec, *args, dmask, q_dec, k_dec, gn)
    return ret.reshape(B * S, nr), state


def _ret_s_kernel(sdec_ref, q_ref, k_ref, v_ref, g_ref, s0_ref, dm_ref, qd_ref, kd_ref, gn_ref,
                  o_ref, s_ref, *, gb, q_len):
    rows = gb * q_len
    row_b = lax.broadcasted_iota(jnp.int32, (rows, RET_DV), 0) // q_len
    for h in range(RET_HEADS):
        cols = slice(h * RET_DK, (h + 1) * RET_DK)
        q = q_ref[:, cols].astype(BF16)
        k = k_ref[:, cols]
        v = v_ref[:, cols].astype(BF16)
        att = _dot_nt(q, k.astype(BF16)) * dm_ref[h]
        o = _dot(att.astype(BF16), v)
        kd = k * kd_ref[h]
        inter = jnp.zeros((rows, RET_DV), F32)
        for b in range(gb):
            state = s0_ref[b, h]
            inter = jnp.where(row_b == b, _dot(q, state.astype(BF16)), inter)
            kd_b = jnp.where(row_b == b, kd, 0.0).astype(BF16)
            s_ref[b, h] = state * sdec_ref[h] + _dot_tn(kd_b, v)
        o = o + inter * qd_ref[h]
        gate = g_ref[:, cols].astype(F32)
        o_ref[:, cols] = _head_norm_gate(o, gate, gn_ref[:, cols]).astype(o_ref.dtype)


def _ret_s(rq, rk, rv, rg, state, ret_gn_g, *, gb):
    DB = state.shape[0]
    q_len = rq.shape[0] // DB
    nr = RET_HEADS * RET_DK
    rows = gb * q_len
    dmask, q_dec, k_dec, s_dec = _ret_consts(q_len)
    same = (jnp.arange(rows)[:, None] // q_len) == (jnp.arange(rows)[None, :] // q_len)
    dm = jnp.where(same[None], jnp.tile(dmask, (1, gb, gb)), 0.0)
    qd = jnp.tile(q_dec, (1, gb, 1))
    kd = jnp.tile(k_dec, (1, gb, 1))
    gn = ret_gn_g.reshape(1, nr)
    blk = pl.BlockSpec((rows, nr), lambda g: (g, 0))
    sblk = pl.BlockSpec((gb, RET_HEADS, RET_DK, RET_DV), lambda g: (g, 0, 0, 0))

    def full(a):
        return pl.BlockSpec(a.shape, lambda g: (0,) * a.ndim)

    return pl.pallas_call(
        functools.partial(_ret_s_kernel, gb=gb, q_len=q_len),
        grid=(DB // gb,),
        in_specs=[pl.BlockSpec(memory_space=pltpu.SMEM), blk, blk, blk, blk, sblk,
                  full(dm), full(qd), full(kd), full(gn)],
        out_specs=[blk, sblk],
        out_shape=[jax.ShapeDtypeStruct((DB * q_len, nr), BF16),
                   jax.ShapeDtypeStruct(state.shape, F32)],
        compiler_params=_cparams(("parallel",)),
        name="ret_s",
    )(s_dec, rq, rk, rv, rg, state, dm, qd, kd, gn)


SLAB = 16


def _attn_s_kernel(pt_ref, ql_ref, qr_ref, cn_ref, kn_ref, ckv_hbm, krt_hbm, o_ref, ckv_buf, krt_buf, sem,
                   *, n_pages, page, q_len, tk):
    b = pl.program_id(0)
    nb = pl.num_programs(0)
    slot = b % 2
    mine = b % (SLAB // q_len)
    P = n_pages * page
    R = MLA_HEADS * q_len

    def page_copies(bi, s):
        out = []
        for p in range(n_pages):
            pg = pt_ref[bi, p]
            out.append(pltpu.make_async_copy(ckv_hbm.at[pg], ckv_buf.at[s, pl.ds(p * page, page), :], sem.at[0, s]))
            out.append(pltpu.make_async_copy(krt_hbm.at[pg], krt_buf.at[s, pl.ds(p * MLA_ROPE, MLA_ROPE), :],
                                             sem.at[1, s]))
        return out

    @pl.when(b == 0)
    def _():
        for cp in page_copies(0, 0):
            cp.start()

    @pl.when(b + 1 < nb)
    def _():
        for cp in page_copies(b + 1, 1 - slot):
            cp.start()

    ql = ql_ref[...]
    qr = qr_ref[:, :MLA_ROPE]
    for cp in page_copies(b, slot):
        cp.wait()

    ppc = tk // page

    def keys(j):
        return ckv_buf[slot, j * tk:(j + 1) * tk, :].astype(BF16)

    scores = []
    for j in range(P // tk):
        krt = jnp.concatenate([krt_buf[slot, (j * ppc + pp) * MLA_ROPE:(j * ppc + pp + 1) * MLA_ROPE, :]
                               for pp in range(ppc)], axis=1).astype(BF16)
        scores.append(_dot_nt(ql, keys(j)) + _dot(qr, krt))
    kn = cn_ref[...].astype(BF16)
    s_new = _dot_nt(ql, kn) + _dot_nt(qr_ref[...], kn_ref[...].astype(BF16))
    row_t = lax.broadcasted_iota(jnp.int32, (R, SLAB), 0)
    col_t = lax.broadcasted_iota(jnp.int32, (R, SLAB), 1)
    ok = (col_t // q_len == mine) & (col_t % q_len <= row_t % q_len)
    s_new = jnp.where(ok, s_new, NEG_INF)
    m = functools.reduce(jnp.maximum, [jnp.max(s, axis=1, keepdims=True) for s in scores + [s_new]])
    p_new = jnp.exp2(s_new - m)
    l = jnp.sum(p_new, axis=1, keepdims=True)
    acc = _dot(p_new.astype(BF16), kn)
    for j, s in enumerate(scores):
        p = jnp.exp2(s - m)
        l = l + jnp.sum(p, axis=1, keepdims=True)
        acc = acc + _dot(p.astype(BF16), keys(j))
    o_ref[...] = (acc / l).astype(o_ref.dtype)


def _attn_s(page_table, ql, qr, ckv_new, krp_new, cache_ckv, cache_krope, *, tk):
    DB, n_pages = page_table.shape
    page = cache_ckv.shape[1]
    R = ql.shape[1]
    q_len = R // MLA_HEADS
    per_slab = SLAB // q_len
    P = n_pages * page
    assert tk % page == 0 and P % tk == 0 and page == LANES
    krt = jnp.swapaxes(cache_krope, 1, 2)

    def slab(b, pt):
        return (b // per_slab, 0)

    def seq(b, pt):
        return (b, 0, 0)

    grid_spec = pltpu.PrefetchScalarGridSpec(
        num_scalar_prefetch=1,
        grid=(DB,),
        in_specs=[pl.BlockSpec((None, R, LANES), seq),
                  pl.BlockSpec((None, R, LANES), seq),
                  pl.BlockSpec((SLAB, MLA_KV_LORA), slab),
                  pl.BlockSpec((SLAB, LANES), slab),
                  pl.BlockSpec(memory_space=pl.ANY),
                  pl.BlockSpec(memory_space=pl.ANY)],
        out_specs=pl.BlockSpec((None, R, LANES), seq),
        scratch_shapes=[pltpu.VMEM((2, P, MLA_KV_LORA), F32),
                        pltpu.VMEM((2, n_pages * MLA_ROPE, page), F32),
                        pltpu.SemaphoreType.DMA((2, 2))],
    )
    return pl.pallas_call(
        functools.partial(_attn_s_kernel, n_pages=n_pages, page=page, q_len=q_len, tk=tk),
        grid_spec=grid_spec,
        out_shape=jax.ShapeDtypeStruct((DB, R, LANES), BF16),
        compiler_params=_cparams(("arbitrary",)),
        name="attn_s",
    )(page_table, ql, qr, ckv_new, krp_new, cache_ckv, krt)


def _absorb_weights(w_uk, w_uv):
    wabs = jnp.transpose(w_uk.reshape(MLA_KV_LORA, MLA_HEADS, MLA_NOPE), (1, 2, 0))
    wabs = jnp.pad(wabs, ((0, 0), (0, LANES - MLA_NOPE), (0, 0))).astype(BF16)
    idx = jnp.arange(MLA_ROPE)
    wsel = jnp.zeros((LANES, LANES), F32).at[MLA_NOPE + idx, idx].set(1.0).astype(BF16)
    wuv = jnp.pad(jnp.transpose(w_uv.reshape(MLA_KV_LORA, MLA_HEADS, MLA_V), (1, 0, 2)),
                  ((0, 0), (0, 0), (0, LANES - MLA_V))).astype(BF16)
    return wabs, wsel, wuv


def _lat_up_kernel(o_ref, wuv_ref, out_ref):
    for h in range(MLA_HEADS):
        cols = slice(h * LANES, (h + 1) * LANES)
        out_ref[:, cols] = _dot(o_ref[:, cols], wuv_ref[h]).astype(out_ref.dtype)


def _lat_up(o_lat_tok, wuv):
    return pl.pallas_call(
        _lat_up_kernel,
        grid=(1,),
        in_specs=[pl.BlockSpec(o_lat_tok.shape, lambda i: (0, 0)), pl.BlockSpec(wuv.shape, lambda i: (0, 0, 0))],
        out_specs=pl.BlockSpec(o_lat_tok.shape, lambda i: (0, 0)),
        out_shape=jax.ShapeDtypeStruct(o_lat_tok.shape, BF16),
        compiler_params=_cparams(("arbitrary",)),
        name="lat_up",
    )(o_lat_tok, wuv)


ROUTE_ROWS = 8


def _first_index(hit, idx, big):
    return jnp.min(jnp.where(hit, idx, big), axis=0, keepdims=True)


def _route(scores, sel):
    tm = scores.shape[1]
    gsz = N_EXPERTS // N_GROUPS
    sub = lax.broadcasted_iota(jnp.int32, (gsz, tm), 0)
    grp_rows = lax.broadcasted_iota(jnp.int32, (N_GROUPS, tm), 0)
    groups = [sel[g * gsz:(g + 1) * gsz, :] for g in range(N_GROUPS)]
    gscore = jnp.zeros((N_GROUPS, tm), F32)
    for g, x in enumerate(groups):
        m1 = jnp.max(x, axis=0, keepdims=True)
        first = _first_index(x == m1, sub, gsz)
        m2 = jnp.max(jnp.where(sub == first, NEG_INF, x), axis=0, keepdims=True)
        gscore = jnp.where(grp_rows == g, m1 + m2, gscore)
    chosen = jnp.zeros((N_GROUPS, tm), jnp.bool_)
    y = gscore
    for _ in range(TOPK_GROUPS):
        m = jnp.max(y, axis=0, keepdims=True)
        hit = grp_rows == _first_index(y == m, grp_rows, N_GROUPS)
        chosen = chosen | hit
        y = jnp.where(hit, NEG_INF, y)
    cand = [jnp.where(chosen[g:g + 1, :], x, NEG_INF) for g, x in enumerate(groups)]
    eids = [sub + g * gsz for g in range(N_GROUPS)]
    out_rows = lax.broadcasted_iota(jnp.int32, (ROUTE_ROWS, tm), 0)
    eidx = jnp.zeros((ROUTE_ROWS, tm), jnp.int32)
    wsel = jnp.zeros((ROUTE_ROWS, tm), F32)
    hits = []
    for k in range(TOP_K):
        m = functools.reduce(jnp.maximum, [jnp.max(c, axis=0, keepdims=True) for c in cand])
        first = functools.reduce(jnp.minimum, [_first_index(c == m, e, N_EXPERTS) for c, e in zip(cand, eids)])
        wk = jnp.zeros((1, tm), F32)
        hit_k = []
        for g in range(N_GROUPS):
            hit = eids[g] == first
            hit_k.append(jnp.where(hit, 1.0, 0.0))
            wk = wk + jnp.sum(jnp.where(hit, scores[g * gsz:(g + 1) * gsz, :], 0.0), axis=0, keepdims=True)
            cand[g] = jnp.where(hit, NEG_INF, cand[g])
        hits.append(jnp.concatenate(hit_k, axis=0))
        eidx = jnp.where(out_rows == k, first, eidx)
        wsel = jnp.where(out_rows == k, wk, wsel)
    total = jnp.sum(wsel, axis=0, keepdims=True)
    return eidx, wsel / total * ROUTED_SCALE, hits


SEG_CAP = 16


def _sorted_rows(tm):
    return TOP_K * tm + N_EXPERTS * (SEG_CAP - 1) + (-(TOP_K * tm + N_EXPERTS * (SEG_CAP - 1))) % SEG_CAP


def _tile_positions(hits):
    tm = hits[0].shape[1]
    sel = functools.reduce(jnp.add, hits)
    before = (lax.broadcasted_iota(jnp.int32, (tm, tm), 0) < lax.broadcasted_iota(jnp.int32, (tm, tm), 1))
    local = _dot(sel.astype(BF16), jnp.where(before, 1.0, 0.0).astype(BF16))
    cnt = jnp.sum(sel, axis=1, keepdims=True)
    nchunk = jnp.floor((cnt + (SEG_CAP - 1)) * (1.0 / SEG_CAP))
    nchunk_rep = jnp.broadcast_to(nchunk, (N_EXPERTS, LANES))
    lower = (lax.broadcasted_iota(jnp.int32, (N_EXPERTS, N_EXPERTS), 1)
             < lax.broadcasted_iota(jnp.int32, (N_EXPERTS, N_EXPERTS), 0))
    loff_rep = _dot(jnp.where(lower, 1.0, 0.0).astype(BF16), nchunk_rep.astype(BF16)) * SEG_CAP
    where = local + loff_rep[:, :1]
    out_rows = lax.broadcasted_iota(jnp.int32, (ROUTE_ROWS, tm), 0)
    lpos = jnp.zeros((ROUTE_ROWS, tm), F32)
    for k, hit in enumerate(hits):
        lpos = jnp.where(out_rows == k, jnp.sum(hit * where, axis=0, keepdims=True), lpos)
    return lpos.astype(jnp.int32), nchunk_rep, loff_rep, cnt


def _split_hi_lo(a):
    hi = a.astype(BF16)
    lo = (a - hi.astype(F32)).astype(BF16)
    return hi, lo


def _mix_kernel(x_ref, om_ref, ret_ref, ga_ref, gb_ref, wo_ref, wr_ref, wout_ref, g1_ref, b1_ref,
                wrt_hi_ref, wrt_lo_ref, rb_ref, run0_ref,
                h_ref, gw_ref, lpos_ref, nch_ref, loff_ref, runb_ref, run_ref):
    @pl.when(pl.program_id(0) == 0)
    def _():
        run_ref[...] = run0_ref[...]

    y_a = _dot(om_ref[...], wo_ref[...])
    y_b = _dot(ret_ref[...], wr_ref[...])
    mixed_in = ga_ref[...].astype(F32) * y_a + gb_ref[...].astype(F32) * y_b
    mixed = _dot(mixed_in.astype(BF16), wout_ref[...])
    h = _layer_norm(DEEPNORM_ALPHA * x_ref[...] + mixed, g1_ref[...], b1_ref[...])
    h_ref[...] = h
    h_hi, h_lo = _split_hi_lo(h)
    logits = _dot_nt(wrt_hi_ref[...], h_hi) + (_dot_nt(wrt_hi_ref[...], h_lo) + _dot_nt(wrt_lo_ref[...], h_hi))
    scores = _sigmoid(logits)
    _, gw, hits = _route(scores, scores + rb_ref[...])
    lpos, nchunk, loff, cnt = _tile_positions(hits)
    gw_ref[...] = gw
    lpos_ref[...] = lpos
    nch_ref[...] = nchunk
    loff_ref[...] = loff
    run = run_ref[...]
    runb_ref[...] = jnp.broadcast_to(run, (N_EXPERTS, LANES))
    run_ref[...] = run + cnt


def _mix(x2d, o_mla, ret, ga, gb, w_o, w_ret_o, w_out, ln1_g, ln1_b, w_router, router_bias, run0, *, tm):
    T, D = x2d.shape
    assert D == ROW_TILE * LANES
    wrt = w_router.T
    wrt_hi, wrt_lo = _split_hi_lo(wrt)
    rb = router_bias.reshape(N_EXPERTS, 1).astype(F32)
    g1, b1 = ln1_g.reshape(1, D), ln1_b.reshape(1, D)
    wr = w_ret_o.astype(BF16)
    wout = w_out.astype(BF16)

    def row(i):
        return (i, 0)

    def full(a):
        return pl.BlockSpec(a.shape, lambda i: (0,) * a.ndim)

    route_spec = pl.BlockSpec((ROUTE_ROWS, tm), lambda i: (0, i))
    return pl.pallas_call(
        _mix_kernel,
        grid=(T // tm,),
        in_specs=[pl.BlockSpec((tm, D), row), pl.BlockSpec((tm, o_mla.shape[1]), row),
                  pl.BlockSpec((tm, ret.shape[1]), row), pl.BlockSpec((tm, D), row), pl.BlockSpec((tm, D), row),
                  full(w_o), full(wr), full(wout), full(g1), full(b1), full(wrt_hi), full(wrt_lo), full(rb),
                  full(run0)],
        out_specs=[pl.BlockSpec((tm, D), row), route_spec, route_spec,
                   pl.BlockSpec((N_EXPERTS, LANES), row), pl.BlockSpec((N_EXPERTS, LANES), row),
                   pl.BlockSpec((N_EXPERTS, LANES), row), pl.BlockSpec((N_EXPERTS, 1), lambda i: (0, 0))],
        out_shape=[jax.ShapeDtypeStruct((T, D), F32),
                   jax.ShapeDtypeStruct((ROUTE_ROWS, T), F32),
                   jax.ShapeDtypeStruct((ROUTE_ROWS, T), jnp.int32),
                   jax.ShapeDtypeStruct((T // tm * N_EXPERTS, LANES), F32),
                   jax.ShapeDtypeStruct((T // tm * N_EXPERTS, LANES), F32),
                   jax.ShapeDtypeStruct((T // tm * N_EXPERTS, LANES), F32),
                   jax.ShapeDtypeStruct((N_EXPERTS, 1), F32)],
        compiler_params=_cparams(("arbitrary",)),
        name="mix",
    )(x2d, o_mla, ret, ga, gb, w_o, wr, wout, g1, b1, wrt_hi, wrt_lo, rb, run0)


def _n_row_blocks(n_tokens):
    return (n_tokens * TOP_K + N_EXPERTS * (MOE_BLOCK + SEG_CAP - 1) + MOE_BLOCK - 1) // MOE_BLOCK


def _block_plan(counts, n_blocks):
    counts = counts.reshape(N_EXPERTS).astype(jnp.int32)
    pad_len = jnp.where(counts > 0, (counts + SEG_CAP + MOE_BLOCK - 1) // MOE_BLOCK * MOE_BLOCK, 0)
    pad_end = jnp.cumsum(pad_len)
    pad_start = pad_end - pad_len
    first_row = jnp.arange(n_blocks, dtype=jnp.int32) * MOE_BLOCK
    blk_e = jnp.minimum(jnp.sum((pad_end[None, :] <= first_row[:, None]).astype(jnp.int32), axis=1), N_EXPERTS - 1)
    n_used = (pad_end[-1:] // MOE_BLOCK).astype(jnp.int32)
    return pad_start.astype(jnp.int32), pad_len.astype(jnp.int32), blk_e, n_used


def _segment_tables(pad_start, tables, n_tiles):
    nch, loff, runb = (t[:, 0].reshape(n_tiles, 1, N_EXPERTS).astype(jnp.int32) for t in tables)
    return nch, loff, pad_start[None, None, :] + runb


def _chunk_tables(seg, n_chunks):
    nch, _, dst = (t[:, 0, :] for t in seg)
    cum = jnp.cumsum(nch, axis=1)
    first = (cum - nch)[:, None, :]
    c = jnp.arange(n_chunks, dtype=jnp.int32)[None, :, None]
    mine = (first <= c) & (c < cum[:, None, :])
    dstc = jnp.sum(jnp.where(mine, dst[:, None, :] + (c - first) * SEG_CAP, 0), axis=2)
    n_tiles = nch.shape[0]
    return cum[:, -1:].reshape(n_tiles, 1, 1).astype(jnp.int32), dstc.reshape(n_tiles, 1, n_chunks).astype(jnp.int32)


def _chunk_copies(tot_ref, dstc_ref, local_buf, rows_hbm, sem, *, to_hbm, fn):
    rows = SEG_CAP * ROW_TILE

    def per_chunk(c, carry):
        loc = local_buf.at[pl.ds(c * rows, rows), :]
        far = rows_hbm.at[pl.ds(dstc_ref[0, 0, c] * ROW_TILE, rows), :]
        fn(pltpu.make_async_copy(loc, far, sem) if to_hbm else pltpu.make_async_copy(far, loc, sem))
        return carry

    lax.fori_loop(0, tot_ref[0, 0, 0], per_chunk, 0)


def _two_group_specs(tm, width, n_main):
    return (pl.BlockSpec((tm, width), lambda i: (jnp.minimum(i, n_main - 1), 0)),
            pl.BlockSpec((tm, width), lambda i: (jnp.maximum(i - n_main, 0), 0)))


def _dispatch_kernel(ps_ref, pl_ref, cnt_ref, nu_ref, tot_ref, dstc_ref, totp_ref, dstcp_ref,
                     lpos_ref, hp_ref, hs_ref, xs_hbm, sbuf, zbuf, sem, zsem, *, tm, n_main, n_blocks):
    i = pl.program_id(0)
    blk_rows = MOE_BLOCK * ROW_TILE
    rl = sbuf.shape[1] // ROW_TILE

    @pl.when(i == 0)
    def _():
        zbuf[...] = jnp.zeros(zbuf.shape, F32)

        def zcopy(block_row):
            return pltpu.make_async_copy(zbuf, xs_hbm.at[pl.ds(block_row * ROW_TILE, blk_rows), :], zsem)

        def each_expert(fn):
            def body(e, c):
                end = ps_ref[e] + pl_ref[e]

                @pl.when(pl_ref[e] > 0)
                def _():
                    fn(zcopy(end - MOE_BLOCK))

                @pl.when(pl_ref[e] - cnt_ref[e] > MOE_BLOCK)
                def _():
                    fn(zcopy(end - 2 * MOE_BLOCK))
                return c
            lax.fori_loop(0, N_EXPERTS, body, 0)

        def each_tail(fn):
            def body(j, c):
                fn(zcopy(j * MOE_BLOCK))
                return c
            lax.fori_loop(nu_ref[0], n_blocks, body, 0)

        each_expert(lambda cp: cp.start())
        each_tail(lambda cp: cp.start())
        each_expert(lambda cp: cp.wait())
        each_tail(lambda cp: cp.wait())

    slot = i % 2
    h = jnp.where(i < n_main, hp_ref[...], hs_ref[...]).astype(BF16)
    rows = lax.broadcasted_iota(jnp.int32, (rl, tm), 0)
    perm = jnp.zeros((rl, tm), F32)
    for k in range(TOP_K):
        perm = jnp.where(rows == lpos_ref[k:k + 1, :], 1.0, perm)
    xsort = _dot(perm.astype(BF16), h)
    for s in range(ROW_TILE):
        sbuf[slot, pl.ds(s, rl, stride=ROW_TILE), :] = xsort[:, s * LANES:(s + 1) * LANES]

    @pl.when(i > 0)
    def _():
        _chunk_copies(totp_ref, dstcp_ref, sbuf.at[1 - slot], xs_hbm, sem.at[1 - slot], to_hbm=True,
                      fn=lambda cp: cp.wait())

    copies = functools.partial(_chunk_copies, tot_ref, dstc_ref, sbuf.at[slot], xs_hbm, sem.at[slot], to_hbm=True)
    copies(fn=lambda cp: cp.start())

    @pl.when(i == pl.num_programs(0) - 1)
    def _():
        copies(fn=lambda cp: cp.wait())


def _dispatch(plan, counts, chunks, lpos, h_p, h_s, n_blocks, *, tm):
    pad_start, pad_len, _, n_used = plan
    tot, dstc = chunks
    n_tiles = tot.shape[0]
    n_main = h_p.shape[0] // tm
    D = h_p.shape[1]
    rl = _sorted_rows(tm)
    smem = pl.BlockSpec(memory_space=pltpu.SMEM)

    def tile_specs(index):
        return (pl.BlockSpec((1, 1, 1), lambda i: (index(i), 0, 0), memory_space=pltpu.SMEM),
                pl.BlockSpec((1, 1, dstc.shape[2]), lambda i: (index(i), 0, 0), memory_space=pltpu.SMEM))

    return pl.pallas_call(
        functools.partial(_dispatch_kernel, tm=tm, n_main=n_main, n_blocks=n_blocks),
        grid=(n_tiles,),
        in_specs=[smem, smem, smem, smem, *tile_specs(lambda i: i), *tile_specs(lambda i: jnp.maximum(i - 1, 0)),
                  pl.BlockSpec((ROUTE_ROWS, tm), lambda i: (0, i)), *_two_group_specs(tm, D, n_main)],
        out_specs=pl.BlockSpec(memory_space=pl.ANY),
        out_shape=jax.ShapeDtypeStruct((n_blocks * MOE_BLOCK * ROW_TILE, LANES), F32),
        scratch_shapes=[pltpu.VMEM((2, rl * ROW_TILE, LANES), F32),
                        pltpu.VMEM((MOE_BLOCK * ROW_TILE, LANES), F32),
                        pltpu.SemaphoreType.DMA((2,)), pltpu.SemaphoreType.DMA(())],
        compiler_params=_cparams(("arbitrary",)),
        name="dispatch",
    )(pad_start, pad_len, counts, n_used, tot, dstc, tot, dstc, lpos, h_p, h_s)


def _from_row_tiles(ref, n_rows, base=0):
    return jnp.concatenate([ref[pl.ds(base * ROW_TILE + s, n_rows, stride=ROW_TILE), :]
                            for s in range(ROW_TILE)], axis=1)


def _experts_kernel(be_ref, nu_ref, x_ref, wg_ref, wu_ref, wd_ref, y_ref, wgb, wub, wdb):
    j = pl.program_id(0)
    used = j < nu_ref[0]

    @pl.when(used & ((j == 0) | (be_ref[j] != be_ref[jnp.maximum(j - 1, 0)])))
    def _():
        wgb[...] = wg_ref[...].astype(BF16)
        wub[...] = wu_ref[...].astype(BF16)
        wdb[...] = wd_ref[...].astype(BF16)

    @pl.when(used)
    def _():
        x = _from_row_tiles(x_ref, MOE_BLOCK).astype(BF16)
        gate = _dot(x, wgb[...])
        up = _dot(x, wub[...])
        hid = (gate * _sigmoid(gate) * up).astype(BF16)
        y = _dot(hid, wdb[...])
        for s in range(ROW_TILE):
            y_ref[pl.ds(s, MOE_BLOCK, stride=ROW_TILE), :] = y[:, s * LANES:(s + 1) * LANES]

    @pl.when(pl.program_id(0) >= nu_ref[0])
    def _():
        y_ref[...] = jnp.zeros(y_ref.shape, F32)


def _experts(xs, blk_e, n_used, w_gate, w_up, w_down):
    n_blocks = blk_e.shape[0]
    D = w_gate.shape[1]

    def blk(j, be, nu):
        return (jnp.minimum(j, nu[0] - 1), 0)

    def out_blk(j, be, nu):
        return (j, 0)

    def wsel(j, be, nu):
        return (be[jnp.minimum(j, nu[0] - 1)], 0, 0)

    rows = MOE_BLOCK * ROW_TILE
    grid_spec = pltpu.PrefetchScalarGridSpec(
        num_scalar_prefetch=2,
        grid=(n_blocks,),
        in_specs=[pl.BlockSpec((rows, LANES), blk),
                  pl.BlockSpec((None, D, EXPERT_FF), wsel),
                  pl.BlockSpec((None, D, EXPERT_FF), wsel),
                  pl.BlockSpec((None, EXPERT_FF, D), wsel)],
        out_specs=pl.BlockSpec((rows, LANES), out_blk),
        scratch_shapes=[pltpu.VMEM((D, EXPERT_FF), BF16), pltpu.VMEM((D, EXPERT_FF), BF16),
                        pltpu.VMEM((EXPERT_FF, D), BF16)],
    )
    return pl.pallas_call(
        _experts_kernel,
        grid_spec=grid_spec,
        out_shape=jax.ShapeDtypeStruct(xs.shape, F32),
        compiler_params=_cparams(("arbitrary",)),
        name="experts",
    )(blk_e, n_used, xs, w_gate, w_up, w_down)


def _combine_kernel(tot_ref, dstc_ref, lpos_ref, gw_ref, hp_ref, hs_ref, y_hbm,
                    wsg_ref, wsu_ref, wsd_ref, g2_ref, b2_ref, op_ref, os_ref, gbuf, sem, *, tm, n_main):
    i = pl.program_id(0)
    rl = gbuf.shape[0] // ROW_TILE

    @pl.when(i == 0)
    def _():
        gbuf[...] = jnp.zeros(gbuf.shape, F32)

    copies = functools.partial(_chunk_copies, tot_ref, dstc_ref, gbuf, y_hbm, sem, to_hbm=False)
    copies(fn=lambda cp: cp.start())
    h = jnp.where(i < n_main, hp_ref[...], hs_ref[...])
    hb = h.astype(BF16)
    gate = _dot(hb, wsg_ref[...])
    up = _dot(hb, wsu_ref[...])
    ffn = _dot((gate * _sigmoid(gate) * up).astype(BF16), wsd_ref[...])
    cols = lax.broadcasted_iota(jnp.int32, (tm, rl), 1)
    place = jnp.zeros((tm, rl), F32)
    for k in range(TOP_K):
        place = jnp.where(cols == lpos_ref[:, k:k + 1], gw_ref[:, k:k + 1], place)
    copies(fn=lambda cp: cp.wait())
    ffn = ffn + _dot(place.astype(BF16), _from_row_tiles(gbuf, rl).astype(BF16))
    out = _layer_norm(DEEPNORM_ALPHA * h + ffn, g2_ref[...], b2_ref[...])

    @pl.when(i < n_main)
    def _():
        op_ref[...] = out

    @pl.when(i >= n_main)
    def _():
        os_ref[...] = out


def _combine(chunks, lpos_t, gw_t, h_p, h_s, ys, w_sh_gate, w_sh_up, w_sh_down, ln2_g, ln2_b, *, tm):
    tot, dstc = chunks
    n_tiles = tot.shape[0]
    D = h_p.shape[1]
    n_main = h_p.shape[0] // tm
    rl = _sorted_rows(tm)
    wsg, wsu, wsd = w_sh_gate.astype(BF16), w_sh_up.astype(BF16), w_sh_down.astype(BF16)
    g2, b2 = ln2_g.reshape(1, D), ln2_b.reshape(1, D)

    def full(a):
        return pl.BlockSpec(a.shape, lambda i: (0,) * a.ndim)

    tot_spec = pl.BlockSpec((1, 1, 1), lambda i: (i, 0, 0), memory_space=pltpu.SMEM)
    dstc_spec = pl.BlockSpec((1, 1, dstc.shape[2]), lambda i: (i, 0, 0), memory_space=pltpu.SMEM)
    tok = pl.BlockSpec((tm, ROUTE_ROWS), lambda i: (i, 0))
    groups = _two_group_specs(tm, D, n_main)
    return pl.pallas_call(
        functools.partial(_combine_kernel, tm=tm, n_main=n_main),
        grid=(n_tiles,),
        in_specs=[tot_spec, dstc_spec, tok, tok, *groups, pl.BlockSpec(memory_space=pl.ANY),
                  full(wsg), full(wsu), full(wsd), full(g2), full(b2)],
        out_specs=list(groups),
        out_shape=[jax.ShapeDtypeStruct(h_p.shape, F32), jax.ShapeDtypeStruct(h_s.shape, F32)],
        scratch_shapes=[pltpu.VMEM((rl * ROW_TILE, LANES), F32), pltpu.SemaphoreType.DMA(())],
        compiler_params=_cparams(("arbitrary",)),
        name="combine",
    )(tot, dstc, lpos_t, gw_t, h_p, h_s, ys, wsg, wsu, wsd, g2, b2)


def kernel(x_prompt, x_sample, cache_ckv, cache_krope, state_ret, page_table, w_in, q_norm_g, w_q_up, kv_norm_g,
           w_uk, w_uv, ret_gn_g, w_mla_o, w_ret_o, w_out, ln1_g, ln1_b, w_router, router_bias,
           w_exp_gate, w_exp_up, w_exp_down, w_sh_gate, w_sh_up, w_sh_down, ln2_g, ln2_b):
    B, S, D = x_prompt.shape
    DB, Q, _ = x_sample.shape
    Tp, Ts = B * S, DB * Q
    past_len = page_table.shape[1] * cache_ckv.shape[1]
    w_small, w_ret, w_gate, w_q, w_k, w_v, w_o = _prep_weights(w_in, w_q_up, w_uk, w_uv, w_mla_o)
    pw = (w_small, w_ret, w_gate, w_q, w_k, w_v)

    tab_p = _rope_tables(jnp.arange(S))
    (qh, ckv_p, kr_p, _, rq, rk, rv, rg, ga, gb, kh, vh) = _proj(
        x_prompt.reshape(Tp, D), tab_p, pw, q_norm_g, kv_norm_g, tm=512, ret_dtype=BF16, decode=False)
    o_mla = _attn_p(qh, kh, vh, B, S, tq=512, tk=512, hps=4)
    ret, ret_state_p = _ret_p(rq, rk, rv, rg, ret_gn_g, B, S, cb=4)
    mix_w = (w_o, w_ret_o, w_out, ln1_g, ln1_b, w_router, router_bias)
    tmoe = min(256, Ts)
    h_p, gw_p, lpos_p, *seg_p, cnt_p = _mix(x_prompt.reshape(Tp, D), o_mla, ret, ga, gb, *mix_w,
                                            jnp.zeros((N_EXPERTS, 1), F32), tm=tmoe)

    tab_s = _rope_tables(jnp.tile(past_len + jnp.arange(Q), DB))
    wabs, wsel, wuv = _absorb_weights(w_uk, w_uv)
    (_, ckv_s, kr_s, krp_s, rq_s, rk_s, rv_s, rg_s, ga_s, gb_s, ql_s, qr_s) = _proj(
        x_sample.reshape(Ts, D), tab_s, pw[:4] + (wabs, wsel), q_norm_g, kv_norm_g, tm=Ts, ret_dtype=F32,
        decode=True)

    def rows_by_head(a):
        return a.reshape(DB, Q, MLA_HEADS, LANES).transpose(0, 2, 1, 3).reshape(DB, MLA_HEADS * Q, LANES)

    o_lat = _attn_s(page_table, rows_by_head(ql_s), rows_by_head(qr_s), ckv_s, krp_s, cache_ckv, cache_krope,
                    tk=min(1024, past_len))
    o_lat_tok = o_lat.reshape(DB, MLA_HEADS, Q, LANES).transpose(0, 2, 1, 3).reshape(Ts, MLA_HEADS * LANES)
    o_mla_s = _lat_up(o_lat_tok, wuv)
    ret_s, ret_state_s = _ret_s(rq_s, rk_s, rv_s, rg_s, state_ret, ret_gn_g, gb=min(16, DB))
    h_s, gw_s, lpos_s, *seg_s, cnt = _mix(x_sample.reshape(Ts, D), o_mla_s, ret_s, ga_s, gb_s, *mix_w,
                                          cnt_p, tm=tmoe)

    n_blocks = _n_row_blocks(Tp + Ts)
    plan = _block_plan(cnt, n_blocks)
    n_tiles = (Tp + Ts) // tmoe
    seg = _segment_tables(plan[0], [jnp.concatenate(t, axis=0) for t in zip(seg_p, seg_s)], n_tiles)
    lpos = jnp.concatenate([lpos_p, lpos_s], axis=1)
    gw = jnp.concatenate([gw_p, gw_s], axis=1)
    chunks = _chunk_tables(seg, _sorted_rows(tmoe) // SEG_CAP)
    xs = _dispatch(plan, cnt.reshape(N_EXPERTS).astype(jnp.int32), chunks, lpos, h_p, h_s, n_blocks, tm=tmoe)
    ys = _experts(xs, plan[2], plan[3], w_exp_gate, w_exp_up, w_exp_down)
    y_p, y_s = _combine(chunks, lpos.T, gw.T, h_p, h_s, ys, w_sh_gate, w_sh_up, w_sh_down, ln2_g, ln2_b, tm=tmoe)

    return (y_p.reshape(B, S, D), y_s.reshape(DB, Q, D),
            ckv_p.reshape(B, S, -1), kr_p.reshape(B, S, -1), ret_state_p,
            ckv_s.reshape(DB, Q, -1), kr_s.reshape(DB, Q, -1), ret_state_s)
```

```python
import functools
import math

import numpy as np
import jax
import jax.numpy as jnp
from jax import lax
from jax.experimental import pallas as pl
from jax.experimental.pallas import tpu as pltpu

F32 = jnp.float32
BF16 = jnp.bfloat16

MLA_HEADS = 8
MLA_NOPE = 64
MLA_ROPE = 32
MLA_V = 64
MLA_Q_LORA = 256
MLA_KV_LORA = 128
MLA_SCALE = (MLA_NOPE + MLA_ROPE) ** -0.5
Q_SCALE = MLA_SCALE * math.log2(math.e)
RET_HEADS = 4
RET_DK = 128
RET_DV = 128
RET_CHUNK = 128
N_EXPERTS = 64
N_GROUPS = 8
TOPK_GROUPS = 4
TOP_K = 6
EXPERT_FF = 256
SHARED_FF = 256
ROUTED_SCALE = 2.5
MOE_BLOCK = 512
ROPE_BASE = 10000.0
LN_EPS = 1e-5
RMS_EPS = 1e-6
DEPTH = 1
DEEPNORM_ALPHA = (2 * DEPTH) ** 0.25

LANES = 128
ROW_TILE = 8
HALF_ROPE = MLA_ROPE // 2
VMEM_LIMIT = 56 * 1024 * 1024
NEG_INF = float("-inf")


def _cparams(sem):
    return pltpu.CompilerParams(dimension_semantics=sem, vmem_limit_bytes=VMEM_LIMIT)


def _dot(a, b):
    return jnp.dot(a, b, preferred_element_type=F32)


def _dot_nt(a, b):
    return lax.dot_general(a, b, (((1,), (1,)), ((), ())), preferred_element_type=F32)


def _dot_tn(a, b):
    return lax.dot_general(a, b, (((0,), (0,)), ((), ())), preferred_element_type=F32)


N_TAB = 5


def _rope_tables(pos):
    L = pos.shape[0]
    posf = pos.astype(F32)[:, None]
    half_r = RET_DK // 2
    ang_r = posf * (ROPE_BASE ** (-jnp.arange(half_r, dtype=F32) / half_r))[None, :]
    cos_r = jnp.concatenate([jnp.cos(ang_r), jnp.cos(ang_r)], axis=1)
    sin_r = jnp.concatenate([-jnp.sin(ang_r), jnp.sin(ang_r)], axis=1)
    ang_m = posf * (ROPE_BASE ** (-jnp.arange(HALF_ROPE, dtype=F32) / HALF_ROPE))[None, :]
    c, s = jnp.cos(ang_m), jnp.sin(ang_m)
    z16 = jnp.zeros((L, HALF_ROPE), F32)

    def place(parts, offset):
        body = jnp.concatenate(parts, axis=1)
        return jnp.concatenate([jnp.zeros((L, offset), F32), body,
                                jnp.zeros((L, LANES - offset - body.shape[1]), F32)], axis=1)

    cos_k = place([c, c], 0)
    sinp_k = place([z16, s], 0)
    sinm_k = place([-s, z16], 0)
    return jnp.concatenate([cos_r, sin_r, cos_k, sinp_k, sinm_k], axis=1)


def _prep_weights(w_in, w_q_up, w_uk, w_uv, w_mla_o):
    d = w_in.shape[0]
    c_q, c_kv, c_kr = MLA_Q_LORA, MLA_KV_LORA, MLA_ROPE
    o_ret = c_q + c_kv + c_kr
    n_ret = 2 * RET_HEADS * RET_DK + 2 * RET_HEADS * RET_DV
    w_small = jnp.concatenate([w_in[:, :o_ret], jnp.zeros((d, LANES - c_kr), F32)], axis=1).astype(BF16)
    w_ret = w_in[:, o_ret:o_ret + n_ret].astype(BF16)
    w_gate = w_in[:, o_ret + n_ret:].astype(BF16)
    hd = MLA_NOPE + MLA_ROPE
    w_q = jnp.pad(w_q_up.reshape(c_q, MLA_HEADS, hd), ((0, 0), (0, 0), (0, LANES - hd)))
    w_q = w_q.reshape(c_q, MLA_HEADS * LANES).astype(BF16)
    wk_top = jnp.pad(w_uk.reshape(c_kv, MLA_HEADS, MLA_NOPE), ((0, 0), (0, 0), (0, LANES - MLA_NOPE)))
    place = jnp.zeros((LANES, MLA_HEADS, LANES), F32)
    idx = jnp.arange(MLA_ROPE)
    place = place.at[idx, :, MLA_NOPE + idx].set(1.0)
    w_k = jnp.concatenate([wk_top.reshape(c_kv, -1), place.reshape(LANES, -1)], axis=0).astype(BF16)
    w_v = jnp.pad(w_uv.reshape(c_kv, MLA_HEADS, MLA_V), ((0, 0), (0, 0), (0, LANES - MLA_V)))
    w_v = w_v.reshape(c_kv, MLA_HEADS * LANES).astype(BF16)
    w_o = jnp.pad(w_mla_o.reshape(MLA_HEADS, MLA_V, -1), ((0, 0), (0, LANES - MLA_V), (0, 0)))
    w_o = w_o.reshape(MLA_HEADS * LANES, -1).astype(BF16)
    return w_small, w_ret, w_gate, w_q, w_k, w_v, w_o


def _rms_norm(x, g):
    inv = lax.rsqrt(jnp.mean(x * x, axis=-1, keepdims=True) + RMS_EPS)
    return x * inv * g


def _layer_norm(x, g, b):
    mu = jnp.mean(x, axis=-1, keepdims=True)
    xc = x - mu
    var = jnp.mean(xc * xc, axis=-1, keepdims=True)
    return xc * lax.rsqrt(var + LN_EPS) * g + b


def _sigmoid(x):
    return 1.0 / (1.0 + jnp.exp(-x))


def _proj_kernel(x_ref, tab_ref, wsm_ref, wret_ref, wg_ref, qg_ref, wq_ref, kvg_ref, wa_ref, wb_ref,
                 qh_ref, ckv_ref, kr_ref, krp_ref, rq_ref, rk_ref, rv_ref, rg_ref, ga_ref, gb_ref,
                 oa_ref, ob_ref, *, decode):
    xb = x_ref[...].astype(BF16)

    def tab(i):
        return tab_ref[:, i * LANES:(i + 1) * LANES]

    small = _dot(xb, wsm_ref[...])
    cq = small[:, :MLA_Q_LORA]
    ckv = small[:, MLA_Q_LORA:MLA_Q_LORA + MLA_KV_LORA]
    krb = small[:, MLA_Q_LORA + MLA_KV_LORA:]
    q = _dot(_rms_norm(cq, qg_ref[...]).astype(BF16), wq_ref[...])
    nope_lane = lax.broadcasted_iota(jnp.int32, (x_ref.shape[0], LANES), 1) < MLA_NOPE
    cos_q = jnp.where(nope_lane, Q_SCALE, pltpu.roll(tab(2), MLA_NOPE, 1) * Q_SCALE)
    sinp_q = pltpu.roll(tab(3), MLA_NOPE, 1) * Q_SCALE
    sinm_q = pltpu.roll(tab(4), MLA_NOPE, 1) * Q_SCALE
    for h in range(MLA_HEADS):
        blk = q[:, h * LANES:(h + 1) * LANES]
        rot = (blk * cos_q + pltpu.roll(blk, HALF_ROPE, 1) * sinp_q
               + pltpu.roll(blk, LANES - HALF_ROPE, 1) * sinm_q)
        rot = rot.astype(BF16)
        qh_ref[:, h * LANES:(h + 1) * LANES] = rot
        if decode:
            oa_ref[:, h * LANES:(h + 1) * LANES] = _dot(rot, wa_ref[h]).astype(BF16)
            ob_ref[:, h * LANES:(h + 1) * LANES] = _dot(rot, wb_ref[...]).astype(BF16)
    ckvn = _rms_norm(ckv, kvg_ref[...])
    ckv_ref[...] = ckvn
    krr = (krb * tab(2) + pltpu.roll(krb, HALF_ROPE, 1) * tab(3)
           + pltpu.roll(krb, LANES - HALF_ROPE, 1) * tab(4))
    kr_ref[...] = krr[:, :MLA_ROPE]
    krp_ref[...] = krr
    if not decode:
        kcat = jnp.concatenate([ckvn, krr], axis=1).astype(BF16)
        oa_ref[...] = _dot(kcat, wa_ref[...]).astype(BF16)
        vv = _dot(kcat[:, :MLA_KV_LORA], wb_ref[...])
        lane = lax.broadcasted_iota(jnp.int32, vv.shape, 1) % LANES
        ob_ref[...] = jnp.where(lane == MLA_V, 1.0, vv).astype(BF16)

    r = _dot(xb, wret_ref[...])
    cos_r, sin_r = tab(0), tab(1)
    nq = RET_HEADS * RET_DK
    for h in range(RET_HEADS):
        sl = slice(h * RET_DK, (h + 1) * RET_DK)
        a = r[:, sl]
        rq_ref[:, sl] = (a * cos_r + pltpu.roll(a, RET_DK // 2, 1) * sin_r).astype(rq_ref.dtype)
        b = r[:, nq + h * RET_DK:nq + (h + 1) * RET_DK]
        rk_ref[:, sl] = ((b * cos_r + pltpu.roll(b, RET_DK // 2, 1) * sin_r)
                         * (RET_DK ** -0.5)).astype(rk_ref.dtype)
    rv_ref[...] = r[:, 2 * nq:2 * nq + RET_HEADS * RET_DV].astype(rv_ref.dtype)
    rg = r[:, 2 * nq + RET_HEADS * RET_DV:]
    rg_ref[...] = (rg * _sigmoid(rg)).astype(rg_ref.dtype)

    g = _dot(xb, wg_ref[...])
    d = ga_ref.shape[1]
    ga_ref[...] = _sigmoid(g[:, :d]).astype(ga_ref.dtype)
    gb_ref[...] = _sigmoid(g[:, d:]).astype(gb_ref.dtype)


def _proj(x2d, tab, weights, q_norm_g, kv_norm_g, *, tm, ret_dtype, decode):
    T, D = x2d.shape
    w_small, w_ret, w_gate, w_q, w_a, w_b = weights
    n_tab = tab.shape[0] // tm
    nr = RET_HEADS * RET_DK
    hp = MLA_HEADS * LANES

    def row(i):
        return (i, 0)

    def full(a):
        return pl.BlockSpec(a.shape, lambda i: (0,) * a.ndim)

    out_shapes = [
        jax.ShapeDtypeStruct((T, hp), BF16),
        jax.ShapeDtypeStruct((T, MLA_KV_LORA), F32),
        jax.ShapeDtypeStruct((T, MLA_ROPE), F32),
        jax.ShapeDtypeStruct((T, LANES), F32),
        jax.ShapeDtypeStruct((T, nr), ret_dtype),
        jax.ShapeDtypeStruct((T, nr), ret_dtype),
        jax.ShapeDtypeStruct((T, nr), ret_dtype),
        jax.ShapeDtypeStruct((T, nr), BF16),
        jax.ShapeDtypeStruct((T, D), BF16),
        jax.ShapeDtypeStruct((T, D), BF16),
        jax.ShapeDtypeStruct((T, hp), BF16),
        jax.ShapeDtypeStruct((T, hp), BF16),
    ]
    out_specs = [pl.BlockSpec((tm, s.shape[1]), row) for s in out_shapes]
    qg = q_norm_g.reshape(1, -1)
    kvg = kv_norm_g.reshape(1, -1)
    in_specs = [pl.BlockSpec((tm, D), row),
                pl.BlockSpec((tm, N_TAB * LANES), lambda i: (i % n_tab, 0)),
                full(w_small), full(w_ret), full(w_gate), full(qg), full(w_q), full(kvg), full(w_a), full(w_b)]
    return pl.pallas_call(
        functools.partial(_proj_kernel, decode=decode),
        grid=(T // tm,),
        in_specs=in_specs,
        out_specs=out_specs,
        out_shape=out_shapes,
        compiler_params=_cparams(("parallel",)),
        name="proj",
    )(x2d, tab, w_small, w_ret, w_gate, qg, w_q, kvg, w_a, w_b)


def _attn_p_kernel(q_ref, k_ref, v_ref, o_ref, m_ref, acc_ref, *, tq, tk, hps):
    i = pl.program_id(2)
    m_ref[...] = jnp.full(m_ref.shape, NEG_INF, F32)
    acc_ref[...] = jnp.zeros(acc_ref.shape, F32)

    def step(j, masked):
        for hh in range(hps):
            cols = slice(hh * LANES, (hh + 1) * LANES)
            k = k_ref[pl.ds(j * tk, tk), cols]
            v = v_ref[pl.ds(j * tk, tk), cols]
            s = _dot_nt(q_ref[:, cols], k)
            if masked:
                row = lax.broadcasted_iota(jnp.int32, (tq, tk), 0) + i * tq
                col = lax.broadcasted_iota(jnp.int32, (tq, tk), 1) + j * tk
                s = jnp.where(col <= row, s, NEG_INF)
            m_prev = m_ref[hh]
            m_new = jnp.maximum(m_prev, jnp.max(s, axis=1, keepdims=True))
            p = jnp.concatenate([jnp.exp2(s[:, c * LANES:(c + 1) * LANES] - m_new)
                                 for c in range(tk // LANES)], axis=1)
            acc_ref[hh] = jnp.exp2(m_prev - m_new) * acc_ref[hh] + _dot(p.astype(BF16), v)
            m_ref[hh] = m_new

    n_full = (i * tq) // tk

    def body(j, c):
        step(j, False)
        return c

    lax.fori_loop(0, n_full, body, 0)
    for jj in range(tq // tk):
        step(n_full + jj, True)
    for hh in range(hps):
        acc = acc_ref[hh]
        o_ref[:, hh * LANES:(hh + 1) * LANES] = (acc / acc[:, MLA_V:MLA_V + 1]).astype(o_ref.dtype)


def _attn_p(qh, kh, vh, B, S, *, tq, tk, hps):
    assert tq % tk == 0 and MLA_HEADS % hps == 0
    nq = S // tq
    hp = MLA_HEADS * LANES
    kh3 = kh.reshape(B, S, hp)
    vh3 = vh.reshape(B, S, hp)
    return pl.pallas_call(
        functools.partial(_attn_p_kernel, tq=tq, tk=tk, hps=hps),
        grid=(B, MLA_HEADS // hps, nq),
        in_specs=[pl.BlockSpec((tq, hps * LANES), lambda b, h, i: (b * nq + i, h)),
                  pl.BlockSpec((None, S, hps * LANES), lambda b, h, i: (b, 0, h)),
                  pl.BlockSpec((None, S, hps * LANES), lambda b, h, i: (b, 0, h))],
        out_specs=pl.BlockSpec((tq, hps * LANES), lambda b, h, i: (b * nq + i, h)),
        out_shape=jax.ShapeDtypeStruct((B * S, hp), BF16),
        scratch_shapes=[pltpu.VMEM((hps, tq, LANES), F32), pltpu.VMEM((hps, tq, LANES), F32)],
        compiler_params=_cparams(("parallel", "parallel", "arbitrary")),
        name="attn_p",
    )(qh, kh3, vh3)


def _ret_consts(C):
    lg = jnp.log(1.0 - 2.0 ** (-5.0 - jnp.arange(RET_HEADS, dtype=F32)))
    idx = jnp.arange(C, dtype=F32)
    diff = idx[:, None] - idx[None, :]
    dmask = jnp.where(diff >= 0, jnp.exp(jnp.maximum(diff, 0.0)[None] * lg[:, None, None]), 0.0)
    q_dec = jnp.exp((idx[None, :] + 1.0) * lg[:, None])[:, :, None]
    k_dec = jnp.exp((C - 1.0 - idx)[None, :] * lg[:, None])[:, :, None]
    s_dec = jnp.exp(C * lg)
    return dmask, q_dec, k_dec, s_dec


def _head_norm_gate(o, gate, gn):
    mu = jnp.mean(o, axis=-1, keepdims=True)
    oc = o - mu
    var = jnp.mean(oc * oc, axis=-1, keepdims=True)
    return gate * (oc * lax.rsqrt(var + LN_EPS) * gn)


def _ret_p_kernel(sdec_ref, q_ref, k_ref, v_ref, g_ref, dm_ref, qd_ref, kd_ref, gn_ref,
                  o_ref, s_ref, *, nb, cb):
    C = RET_CHUNK

    @pl.when(pl.program_id(0) == 0)
    def _():
        s_ref[...] = jnp.zeros(s_ref.shape, F32)

    for c in range(cb):
        rows = slice(c * C, (c + 1) * C)
        for b in range(nb):
            for h in range(RET_HEADS):
                cols = slice(h * RET_DK, (h + 1) * RET_DK)
                q = q_ref[b, rows, cols]
                k = k_ref[b, rows, cols]
                v = v_ref[b, rows, cols]
                state = s_ref[b, h]
                att = _dot_nt(q, k) * dm_ref[h]
                o = _dot(att.astype(BF16), v) + _dot(q, state.astype(BF16)) * qd_ref[h]
                kd = (k.astype(F32) * kd_ref[h]).astype(BF16)
                s_ref[b, h] = state * sdec_ref[h] + _dot_tn(kd, v)
                gate = g_ref[b, rows, cols].astype(F32)
                o_ref[b, rows, cols] = _head_norm_gate(o, gate, gn_ref[:, cols]).astype(o_ref.dtype)


def _ret_p(rq, rk, rv, rg, ret_gn_g, B, S, *, cb):
    C = RET_CHUNK
    nr = RET_HEADS * RET_DK
    dmask, q_dec, k_dec, s_dec = _ret_consts(C)
    blk = pl.BlockSpec((B, cb * C, nr), lambda g: (0, g, 0))

    def full(a):
        return pl.BlockSpec(a.shape, lambda g: (0,) * a.ndim)

    gn = ret_gn_g.reshape(1, nr)
    args = [a.reshape(B, S, nr) for a in (rq, rk, rv, rg)]
    ret, state = pl.pallas_call(
        functools.partial(_ret_p_kernel, nb=B, cb=cb),
        grid=(S // (cb * C),),
        in_specs=[pl.BlockSpec(memory_space=pltpu.SMEM), blk, blk, blk, blk,
                  full(dmask), full(q_dec), full(k_dec), full(gn)],
        out_specs=[blk, pl.BlockSpec((B, RET_HEADS, RET_DK, RET_DV), lambda g: (0, 0, 0, 0))],
        out_shape=[jax.ShapeDtypeStruct((B, S, nr), BF16),
                   jax.ShapeDtypeStruct((B, RET_HEADS, RET_DK, RET_DV), F32)],
        compiler_params=_cparams(("arbitrary",)),
        name="ret_p",
    )(s_dec, *args, dmask, q_dec, k_dec, gn)
    return ret.reshape(B * S, nr), state


def _ret_s_kernel(sdec_ref, q_ref, k_ref, v_ref, g_ref, s0_ref, dm_ref, qd_ref, kd_ref, gn_ref,
                  o_ref, s_ref, *, gb, q_len):
    rows = gb * q_len
    row_b = lax.broadcasted_iota(jnp.int32, (rows, RET_DV), 0) // q_len
    for h in range(RET_HEADS):
        cols = slice(h * RET_DK, (h + 1) * RET_DK)
        q = q_ref[:, cols].astype(BF16)
        k = k_ref[:, cols]
        v = v_ref[:, cols].astype(BF16)
        att = _dot_nt(q, k.astype(BF16)) * dm_ref[h]
        o = _dot(att.astype(BF16), v)
        kd = k * kd_ref[h]
        inter = jnp.zeros((rows, RET_DV), F32)
        for b in range(gb):
            state = s0_ref[b, h]
            inter = jnp.where(row_b == b, _dot(q, state.astype(BF16)), inter)
            kd_b = jnp.where(row_b == b, kd, 0.0).astype(BF16)
            s_ref[b, h] = state * sdec_ref[h] + _dot_tn(kd_b, v)
        o = o + inter * qd_ref[h]
        gate = g_ref[:, cols].astype(F32)
        o_ref[:, cols] = _head_norm_gate(o, gate, gn_ref[:, cols]).astype(o_ref.dtype)


def _ret_s(rq, rk, rv, rg, state, ret_gn_g, *, gb):
    DB = state.shape[0]
    q_len = rq.shape[0] // DB
    nr = RET_HEADS * RET_DK
    rows = gb * q_len
    dmask, q_dec, k_dec, s_dec = _ret_consts(q_len)
    same = (jnp.arange(rows)[:, None] // q_len) == (jnp.arange(rows)[None, :] // q_len)
    dm = jnp.where(same[None], jnp.tile(dmask, (1, gb, gb)), 0.0)
    qd = jnp.tile(q_dec, (1, gb, 1))
    kd = jnp.tile(k_dec, (1, gb, 1))
    gn = ret_gn_g.reshape(1, nr)
    blk = pl.BlockSpec((rows, nr), lambda g: (g, 0))
    sblk = pl.BlockSpec((gb, RET_HEADS, RET_DK, RET_DV), lambda g: (g, 0, 0, 0))

    def full(a):
        return pl.BlockSpec(a.shape, lambda g: (0,) * a.ndim)

    return pl.pallas_call(
        functools.partial(_ret_s_kernel, gb=gb, q_len=q_len),
        grid=(DB // gb,),
        in_specs=[pl.BlockSpec(memory_space=pltpu.SMEM), blk, blk, blk, blk, sblk,
                  full(dm), full(qd), full(kd), full(gn)],
        out_specs=[blk, sblk],
        out_shape=[jax.ShapeDtypeStruct((DB * q_len, nr), BF16),
                   jax.ShapeDtypeStruct(state.shape, F32)],
        compiler_params=_cparams(("parallel",)),
        name="ret_s",
    )(s_dec, rq, rk, rv, rg, state, dm, qd, kd, gn)


SLAB = 16


def _attn_s_kernel(pt_ref, ql_ref, qr_ref, cn_ref, kn_ref, ckv_hbm, krt_hbm, o_ref, ckv_buf, krt_buf, sem,
                   *, n_pages, page, q_len, tk):
    b = pl.program_id(0)
    nb = pl.num_programs(0)
    slot = b % 2
    mine = b % (SLAB // q_len)
    P = n_pages * page
    R = MLA_HEADS * q_len

    def page_copies(bi, s):
        out = []
        for p in range(n_pages):
            pg = pt_ref[bi, p]
            out.append(pltpu.make_async_copy(ckv_hbm.at[pg], ckv_buf.at[s, pl.ds(p * page, page), :], sem.at[0, s]))
            out.append(pltpu.make_async_copy(krt_hbm.at[pg], krt_buf.at[s, pl.ds(p * MLA_ROPE, MLA_ROPE), :],
                                             sem.at[1, s]))
        return out

    @pl.when(b == 0)
    def _():
        for cp in page_copies(0, 0):
            cp.start()

    @pl.when(b + 1 < nb)
    def _():
        for cp in page_copies(b + 1, 1 - slot):
            cp.start()

    ql = ql_ref[...]
    qr = qr_ref[:, :MLA_ROPE]
    for cp in page_copies(b, slot):
        cp.wait()

    ppc = tk // page

    def keys(j):
        return ckv_buf[slot, j * tk:(j + 1) * tk, :].astype(BF16)

    scores = []
    for j in range(P // tk):
        krt = jnp.concatenate([krt_buf[slot, (j * ppc + pp) * MLA_ROPE:(j * ppc + pp + 1) * MLA_ROPE, :]
                               for pp in range(ppc)], axis=1).astype(BF16)
        scores.append(_dot_nt(ql, keys(j)) + _dot(qr, krt))
    kn = cn_ref[...].astype(BF16)
    s_new = _dot_nt(ql, kn) + _dot_nt(qr_ref[...], kn_ref[...].astype(BF16))
    row_t = lax.broadcasted_iota(jnp.int32, (R, SLAB), 0)
    col_t = lax.broadcasted_iota(jnp.int32, (R, SLAB), 1)
    ok = (col_t // q_len == mine) & (col_t % q_len <= row_t % q_len)
    s_new = jnp.where(ok, s_new, NEG_INF)
    m = functools.reduce(jnp.maximum, [jnp.max(s, axis=1, keepdims=True) for s in scores + [s_new]])
    p_new = jnp.exp2(s_new - m)
    l = jnp.sum(p_new, axis=1, keepdims=True)
    acc = _dot(p_new.astype(BF16), kn)
    for j, s in enumerate(scores):
        p = jnp.exp2(s - m)
        l = l + jnp.sum(p, axis=1, keepdims=True)
        acc = acc + _dot(p.astype(BF16), keys(j))
    o_ref[...] = (acc / l).astype(o_ref.dtype)


def _attn_s(page_table, ql, qr, ckv_new, krp_new, cache_ckv, cache_krope, *, tk):
    DB, n_pages = page_table.shape
    page = cache_ckv.shape[1]
    R = ql.shape[1]
    q_len = R // MLA_HEADS
    per_slab = SLAB // q_len
    P = n_pages * page
    assert tk % page == 0 and P % tk == 0 and page == LANES
    krt = jnp.swapaxes(cache_krope, 1, 2)

    def slab(b, pt):
        return (b // per_slab, 0)

    def seq(b, pt):
        return (b, 0, 0)

    grid_spec = pltpu.PrefetchScalarGridSpec(
        num_scalar_prefetch=1,
        grid=(DB,),
        in_specs=[pl.BlockSpec((None, R, LANES), seq),
                  pl.BlockSpec((None, R, LANES), seq),
                  pl.BlockSpec((SLAB, MLA_KV_LORA), slab),
                  pl.BlockSpec((SLAB, LANES), slab),
                  pl.BlockSpec(memory_space=pl.ANY),
                  pl.BlockSpec(memory_space=pl.ANY)],
        out_specs=pl.BlockSpec((None, R, LANES), seq),
        scratch_shapes=[pltpu.VMEM((2, P, MLA_KV_LORA), F32),
                        pltpu.VMEM((2, n_pages * MLA_ROPE, page), F32),
                        pltpu.SemaphoreType.DMA((2, 2))],
    )
    return pl.pallas_call(
        functools.partial(_attn_s_kernel, n_pages=n_pages, page=page, q_len=q_len, tk=tk),
        grid_spec=grid_spec,
        out_shape=jax.ShapeDtypeStruct((DB, R, LANES), BF16),
        compiler_params=_cparams(("arbitrary",)),
        name="attn_s",
    )(page_table, ql, qr, ckv_new, krp_new, cache_ckv, krt)


def _absorb_weights(w_uk, w_uv):
    wabs = jnp.transpose(w_uk.reshape(MLA_KV_LORA, MLA_HEADS, MLA_NOPE), (1, 2, 0))
    wabs = jnp.pad(wabs, ((0, 0), (0, LANES - MLA_NOPE), (0, 0))).astype(BF16)
    idx = jnp.arange(MLA_ROPE)
    wsel = jnp.zeros((LANES, LANES), F32).at[MLA_NOPE + idx, idx].set(1.0).astype(BF16)
    wuv = jnp.pad(jnp.transpose(w_uv.reshape(MLA_KV_LORA, MLA_HEADS, MLA_V), (1, 0, 2)),
                  ((0, 0), (0, 0), (0, LANES - MLA_V))).astype(BF16)
    return wabs, wsel, wuv


def _lat_up_kernel(o_ref, wuv_ref, out_ref):
    for h in range(MLA_HEADS):
        cols = slice(h * LANES, (h + 1) * LANES)
        out_ref[:, cols] = _dot(o_ref[:, cols], wuv_ref[h]).astype(out_ref.dtype)


def _lat_up(o_lat_tok, wuv):
    return pl.pallas_call(
        _lat_up_kernel,
        grid=(1,),
        in_specs=[pl.BlockSpec(o_lat_tok.shape, lambda i: (0, 0)), pl.BlockSpec(wuv.shape, lambda i: (0, 0, 0))],
        out_specs=pl.BlockSpec(o_lat_tok.shape, lambda i: (0, 0)),
        out_shape=jax.ShapeDtypeStruct(o_lat_tok.shape, BF16),
        compiler_params=_cparams(("arbitrary",)),
        name="lat_up",
    )(o_lat_tok, wuv)


ROUTE_ROWS = 8


def _first_index(hit, idx, big):
    return jnp.min(jnp.where(hit, idx, big), axis=0, keepdims=True)


def _route(scores, sel):
    tm = scores.shape[1]
    gsz = N_EXPERTS // N_GROUPS
    sub = lax.broadcasted_iota(jnp.int32, (gsz, tm), 0)
    grp_rows = lax.broadcasted_iota(jnp.int32, (N_GROUPS, tm), 0)
    groups = [sel[g * gsz:(g + 1) * gsz, :] for g in range(N_GROUPS)]
    gscore = jnp.zeros((N_GROUPS, tm), F32)
    for g, x in enumerate(groups):
        m1 = jnp.max(x, axis=0, keepdims=True)
        first = _first_index(x == m1, sub, gsz)
        m2 = jnp.max(jnp.where(sub == first, NEG_INF, x), axis=0, keepdims=True)
        gscore = jnp.where(grp_rows == g, m1 + m2, gscore)
    chosen = jnp.zeros((N_GROUPS, tm), jnp.bool_)
    y = gscore
    for _ in range(TOPK_GROUPS):
        m = jnp.max(y, axis=0, keepdims=True)
        hit = grp_rows == _first_index(y == m, grp_rows, N_GROUPS)
        chosen = chosen | hit
        y = jnp.where(hit, NEG_INF, y)
    cand = [jnp.where(chosen[g:g + 1, :], x, NEG_INF) for g, x in enumerate(groups)]
    eids = [sub + g * gsz for g in range(N_GROUPS)]
    out_rows = lax.broadcasted_iota(jnp.int32, (ROUTE_ROWS, tm), 0)
    eidx = jnp.zeros((ROUTE_ROWS, tm), jnp.int32)
    wsel = jnp.zeros((ROUTE_ROWS, tm), F32)
    hits = []
    for k in range(TOP_K):
        m = functools.reduce(jnp.maximum, [jnp.max(c, axis=0, keepdims=True) for c in cand])
        first = functools.reduce(jnp.minimum, [_first_index(c == m, e, N_EXPERTS) for c, e in zip(cand, eids)])
        wk = jnp.zeros((1, tm), F32)
        hit_k = []
        for g in range(N_GROUPS):
            hit = eids[g] == first
            hit_k.append(jnp.where(hit, 1.0, 0.0))
            wk = wk + jnp.sum(jnp.where(hit, scores[g * gsz:(g + 1) * gsz, :], 0.0), axis=0, keepdims=True)
            cand[g] = jnp.where(hit, NEG_INF, cand[g])
        hits.append(jnp.concatenate(hit_k, axis=0))
        eidx = jnp.where(out_rows == k, first, eidx)
        wsel = jnp.where(out_rows == k, wk, wsel)
    total = jnp.sum(wsel, axis=0, keepdims=True)
    return eidx, wsel / total * ROUTED_SCALE, hits


SEG_CAP = 8


def _sorted_rows(tm):
    return TOP_K * tm + N_EXPERTS * (SEG_CAP - 1) + (-(TOP_K * tm + N_EXPERTS * (SEG_CAP - 1))) % SEG_CAP


def _tile_positions(hits):
    tm = hits[0].shape[1]
    sel = functools.reduce(jnp.add, hits)
    before = (lax.broadcasted_iota(jnp.int32, (tm, tm), 0) < lax.broadcasted_iota(jnp.int32, (tm, tm), 1))
    local = _dot(sel.astype(BF16), jnp.where(before, 1.0, 0.0).astype(BF16))
    cnt = jnp.sum(sel, axis=1, keepdims=True)
    nchunk = jnp.floor((cnt + (SEG_CAP - 1)) * (1.0 / SEG_CAP))
    nchunk_rep = jnp.broadcast_to(nchunk, (N_EXPERTS, LANES))
    lower = (lax.broadcasted_iota(jnp.int32, (N_EXPERTS, N_EXPERTS), 1)
             < lax.broadcasted_iota(jnp.int32, (N_EXPERTS, N_EXPERTS), 0))
    loff_rep = _dot(jnp.where(lower, 1.0, 0.0).astype(BF16), nchunk_rep.astype(BF16)) * SEG_CAP
    where = local + loff_rep[:, :1]
    out_rows = lax.broadcasted_iota(jnp.int32, (ROUTE_ROWS, tm), 0)
    lpos = jnp.zeros((ROUTE_ROWS, tm), F32)
    for k, hit in enumerate(hits):
        lpos = jnp.where(out_rows == k, jnp.sum(hit * where, axis=0, keepdims=True), lpos)
    return lpos.astype(jnp.int32), nchunk_rep, loff_rep, cnt


def _split_hi_lo(a):
    hi = a.astype(BF16)
    lo = (a - hi.astype(F32)).astype(BF16)
    return hi, lo


def _mix_kernel(x_ref, om_ref, ret_ref, ga_ref, gb_ref, wo_ref, wr_ref, wout_ref, g1_ref, b1_ref,
                wrt_hi_ref, wrt_lo_ref, rb_ref, run0_ref,
                h_ref, gw_ref, lpos_ref, nch_ref, loff_ref, runb_ref, run_ref):
    @pl.when(pl.program_id(0) == 0)
    def _():
        run_ref[...] = run0_ref[...]

    y_a = _dot(om_ref[...], wo_ref[...])
    y_b = _dot(ret_ref[...], wr_ref[...])
    mixed_in = ga_ref[...].astype(F32) * y_a + gb_ref[...].astype(F32) * y_b
    mixed = _dot(mixed_in.astype(BF16), wout_ref[...])
    h = _layer_norm(DEEPNORM_ALPHA * x_ref[...] + mixed, g1_ref[...], b1_ref[...])
    h_ref[...] = h
    h_hi, h_lo = _split_hi_lo(h)
    logits = _dot_nt(wrt_hi_ref[...], h_hi) + (_dot_nt(wrt_hi_ref[...], h_lo) + _dot_nt(wrt_lo_ref[...], h_hi))
    scores = _sigmoid(logits)
    _, gw, hits = _route(scores, scores + rb_ref[...])
    lpos, nchunk, loff, cnt = _tile_positions(hits)
    gw_ref[...] = gw
    lpos_ref[...] = lpos
    nch_ref[...] = nchunk
    loff_ref[...] = loff
    run = run_ref[...]
    runb_ref[...] = jnp.broadcast_to(run, (N_EXPERTS, LANES))
    run_ref[...] = run + cnt


def _mix(x2d, o_mla, ret, ga, gb, w_o, w_ret_o, w_out, ln1_g, ln1_b, w_router, router_bias, run0, *, tm):
    T, D = x2d.shape
    assert D == ROW_TILE * LANES
    wrt = w_router.T
    wrt_hi, wrt_lo = _split_hi_lo(wrt)
    rb = router_bias.reshape(N_EXPERTS, 1).astype(F32)
    g1, b1 = ln1_g.reshape(1, D), ln1_b.reshape(1, D)
    wr = w_ret_o.astype(BF16)
    wout = w_out.astype(BF16)

    def row(i):
        return (i, 0)

    def full(a):
        return pl.BlockSpec(a.shape, lambda i: (0,) * a.ndim)

    route_spec = pl.BlockSpec((ROUTE_ROWS, tm), lambda i: (0, i))
    return pl.pallas_call(
        _mix_kernel,
        grid=(T // tm,),
        in_specs=[pl.BlockSpec((tm, D), row), pl.BlockSpec((tm, o_mla.shape[1]), row),
                  pl.BlockSpec((tm, ret.shape[1]), row), pl.BlockSpec((tm, D), row), pl.BlockSpec((tm, D), row),
                  full(w_o), full(wr), full(wout), full(g1), full(b1), full(wrt_hi), full(wrt_lo), full(rb),
                  full(run0)],
        out_specs=[pl.BlockSpec((tm, D), row), route_spec, route_spec,
                   pl.BlockSpec((N_EXPERTS, LANES), row), pl.BlockSpec((N_EXPERTS, LANES), row),
                   pl.BlockSpec((N_EXPERTS, LANES), row), pl.BlockSpec((N_EXPERTS, 1), lambda i: (0, 0))],
        out_shape=[jax.ShapeDtypeStruct((T, D), F32),
                   jax.ShapeDtypeStruct((ROUTE_ROWS, T), F32),
                   jax.ShapeDtypeStruct((ROUTE_ROWS, T), jnp.int32),
                   jax.ShapeDtypeStruct((T // tm * N_EXPERTS, LANES), F32),
                   jax.ShapeDtypeStruct((T // tm * N_EXPERTS, LANES), F32),
                   jax.ShapeDtypeStruct((T // tm * N_EXPERTS, LANES), F32),
                   jax.ShapeDtypeStruct((N_EXPERTS, 1), F32)],
        compiler_params=_cparams(("arbitrary",)),
        name="mix",
    )(x2d, o_mla, ret, ga, gb, w_o, wr, wout, g1, b1, wrt_hi, wrt_lo, rb, run0)


def _n_row_blocks(n_tokens):
    return (n_tokens * TOP_K + N_EXPERTS * (MOE_BLOCK + SEG_CAP - 1) + MOE_BLOCK - 1) // MOE_BLOCK


def _block_plan(counts, n_blocks):
    counts = counts.reshape(N_EXPERTS).astype(jnp.int32)
    pad_len = jnp.where(counts > 0, (counts + SEG_CAP + MOE_BLOCK - 1) // MOE_BLOCK * MOE_BLOCK, 0)
    pad_end = jnp.cumsum(pad_len)
    pad_start = pad_end - pad_len
    first_row = jnp.arange(n_blocks, dtype=jnp.int32) * MOE_BLOCK
    blk_e = jnp.minimum(jnp.sum((pad_end[None, :] <= first_row[:, None]).astype(jnp.int32), axis=1), N_EXPERTS - 1)
    n_used = (pad_end[-1:] // MOE_BLOCK).astype(jnp.int32)
    return pad_start.astype(jnp.int32), pad_len.astype(jnp.int32), blk_e, n_used


def _segment_tables(pad_start, tables, n_tiles):
    nch, loff, runb = (t[:, 0].reshape(n_tiles, 1, N_EXPERTS).astype(jnp.int32) for t in tables)
    return nch, loff, pad_start[None, None, :] + runb


def _chunk_tables(seg, n_chunks):
    nch, _, dst = (t[:, 0, :] for t in seg)
    cum = jnp.cumsum(nch, axis=1)
    first = (cum - nch)[:, None, :]
    c = jnp.arange(n_chunks, dtype=jnp.int32)[None, :, None]
    mine = (first <= c) & (c < cum[:, None, :])
    dstc = jnp.sum(jnp.where(mine, dst[:, None, :] + (c - first) * SEG_CAP, 0), axis=2)
    n_tiles = nch.shape[0]
    return cum[:, -1:].reshape(n_tiles, 1, 1).astype(jnp.int32), dstc.reshape(n_tiles, 1, n_chunks).astype(jnp.int32)


def _chunk_copies(tot_ref, dstc_ref, local_buf, rows_hbm, sem, *, to_hbm, fn):
    rows = SEG_CAP * ROW_TILE

    def per_chunk(c, carry):
        loc = local_buf.at[pl.ds(c * rows, rows), :]
        far = rows_hbm.at[pl.ds(dstc_ref[0, 0, c] * ROW_TILE, rows), :]
        fn(pltpu.make_async_copy(loc, far, sem) if to_hbm else pltpu.make_async_copy(far, loc, sem))
        return carry

    lax.fori_loop(0, tot_ref[0, 0, 0], per_chunk, 0)


def _two_group_specs(tm, width, n_main):
    return (pl.BlockSpec((tm, width), lambda i: (jnp.minimum(i, n_main - 1), 0)),
            pl.BlockSpec((tm, width), lambda i: (jnp.maximum(i - n_main, 0), 0)))


def _dispatch_kernel(ps_ref, pl_ref, cnt_ref, nu_ref, tot_ref, dstc_ref, totp_ref, dstcp_ref,
                     lpos_ref, hp_ref, hs_ref, xs_hbm, sbuf, zbuf, sem, zsem, *, tm, n_main, n_blocks):
    i = pl.program_id(0)
    blk_rows = MOE_BLOCK * ROW_TILE
    rl = sbuf.shape[1] // ROW_TILE

    @pl.when(i == 0)
    def _():
        zbuf[...] = jnp.zeros(zbuf.shape, F32)

        def zcopy(block_row):
            return pltpu.make_async_copy(zbuf, xs_hbm.at[pl.ds(block_row * ROW_TILE, blk_rows), :], zsem)

        def each_expert(fn):
            def body(e, c):
                end = ps_ref[e] + pl_ref[e]

                @pl.when(pl_ref[e] > 0)
                def _():
                    fn(zcopy(end - MOE_BLOCK))

                @pl.when(pl_ref[e] - cnt_ref[e] > MOE_BLOCK)
                def _():
                    fn(zcopy(end - 2 * MOE_BLOCK))
                return c
            lax.fori_loop(0, N_EXPERTS, body, 0)

        def each_tail(fn):
            def body(j, c):
                fn(zcopy(j * MOE_BLOCK))
                return c
            lax.fori_loop(nu_ref[0], n_blocks, body, 0)

        each_expert(lambda cp: cp.start())
        each_tail(lambda cp: cp.start())
        each_expert(lambda cp: cp.wait())
        each_tail(lambda cp: cp.wait())

    slot = i % 2
    h = jnp.where(i < n_main, hp_ref[...], hs_ref[...]).astype(BF16)
    rows = lax.broadcasted_iota(jnp.int32, (rl, tm), 0)
    perm = jnp.zeros((rl, tm), F32)
    for k in range(TOP_K):
        perm = jnp.where(rows == lpos_ref[k:k + 1, :], 1.0, perm)
    xsort = _dot(perm.astype(BF16), h)
    for s in range(ROW_TILE):
        sbuf[slot, pl.ds(s, rl, stride=ROW_TILE), :] = xsort[:, s * LANES:(s + 1) * LANES]

    @pl.when(i > 0)
    def _():
        _chunk_copies(totp_ref, dstcp_ref, sbuf.at[1 - slot], xs_hbm, sem.at[1 - slot], to_hbm=True,
                      fn=lambda cp: cp.wait())

    copies = functools.partial(_chunk_copies, tot_ref, dstc_ref, sbuf.at[slot], xs_hbm, sem.at[slot], to_hbm=True)
    copies(fn=lambda cp: cp.start())

    @pl.when(i == pl.num_programs(0) - 1)
    def _():
        copies(fn=lambda cp: cp.wait())


def _dispatch(plan, counts, chunks, lpos, h_p, h_s, n_blocks, *, tm):
    pad_start, pad_len, _, n_used = plan
    tot, dstc = chunks
    n_tiles = tot.shape[0]
    n_main = h_p.shape[0] // tm
    D = h_p.shape[1]
    rl = _sorted_rows(tm)
    smem = pl.BlockSpec(memory_space=pltpu.SMEM)

    def tile_specs(index):
        return (pl.BlockSpec((1, 1, 1), lambda i: (index(i), 0, 0), memory_space=pltpu.SMEM),
                pl.BlockSpec((1, 1, dstc.shape[2]), lambda i: (index(i), 0, 0), memory_space=pltpu.SMEM))

    return pl.pallas_call(
        functools.partial(_dispatch_kernel, tm=tm, n_main=n_main, n_blocks=n_blocks),
        grid=(n_tiles,),
        in_specs=[smem, smem, smem, smem, *tile_specs(lambda i: i), *tile_specs(lambda i: jnp.maximum(i - 1, 0)),
                  pl.BlockSpec((ROUTE_ROWS, tm), lambda i: (0, i)), *_two_group_specs(tm, D, n_main)],
        out_specs=pl.BlockSpec(memory_space=pl.ANY),
        out_shape=jax.ShapeDtypeStruct((n_blocks * MOE_BLOCK * ROW_TILE, LANES), F32),
        scratch_shapes=[pltpu.VMEM((2, rl * ROW_TILE, LANES), F32),
                        pltpu.VMEM((MOE_BLOCK * ROW_TILE, LANES), F32),
                        pltpu.SemaphoreType.DMA((2,)), pltpu.SemaphoreType.DMA(())],
        compiler_params=_cparams(("arbitrary",)),
        name="dispatch",
    )(pad_start, pad_len, counts, n_used, tot, dstc, tot, dstc, lpos, h_p, h_s)


def _from_row_tiles(ref, n_rows, base=0):
    return jnp.concatenate([ref[pl.ds(base * ROW_TILE + s, n_rows, stride=ROW_TILE), :]
                            for s in range(ROW_TILE)], axis=1)


def _experts_kernel(be_ref, nu_ref, x_ref, wg_ref, wu_ref, wd_ref, y_ref, wgb, wub, wdb):
    j = pl.program_id(0)
    used = j < nu_ref[0]

    @pl.when(used & ((j == 0) | (be_ref[j] != be_ref[jnp.maximum(j - 1, 0)])))
    def _():
        wgb[...] = wg_ref[...].astype(BF16)
        wub[...] = wu_ref[...].astype(BF16)
        wdb[...] = wd_ref[...].astype(BF16)

    @pl.when(used)
    def _():
        x = _from_row_tiles(x_ref, MOE_BLOCK).astype(BF16)
        gate = _dot(x, wgb[...])
        up = _dot(x, wub[...])
        hid = (gate * _sigmoid(gate) * up).astype(BF16)
        y = _dot(hid, wdb[...])
        for s in range(ROW_TILE):
            y_ref[pl.ds(s, MOE_BLOCK, stride=ROW_TILE), :] = y[:, s * LANES:(s + 1) * LANES]

    @pl.when(pl.program_id(0) >= nu_ref[0])
    def _():
        y_ref[...] = jnp.zeros(y_ref.shape, F32)


def _experts(xs, blk_e, n_used, w_gate, w_up, w_down):
    n_blocks = blk_e.shape[0]
    D = w_gate.shape[1]

    def blk(j, be, nu):
        return (jnp.minimum(j, nu[0] - 1), 0)

    def out_blk(j, be, nu):
        return (j, 0)

    def wsel(j, be, nu):
        return (be[jnp.minimum(j, nu[0] - 1)], 0, 0)

    rows = MOE_BLOCK * ROW_TILE
    grid_spec = pltpu.PrefetchScalarGridSpec(
        num_scalar_prefetch=2,
        grid=(n_blocks,),
        in_specs=[pl.BlockSpec((rows, LANES), blk),
                  pl.BlockSpec((None, D, EXPERT_FF), wsel),
                  pl.BlockSpec((None, D, EXPERT_FF), wsel),
                  pl.BlockSpec((None, EXPERT_FF, D), wsel)],
        out_specs=pl.BlockSpec((rows, LANES), out_blk),
        scratch_shapes=[pltpu.VMEM((D, EXPERT_FF), BF16), pltpu.VMEM((D, EXPERT_FF), BF16),
                        pltpu.VMEM((EXPERT_FF, D), BF16)],
    )
    return pl.pallas_call(
        _experts_kernel,
        grid_spec=grid_spec,
        out_shape=jax.ShapeDtypeStruct(xs.shape, F32),
        compiler_params=_cparams(("arbitrary",)),
        name="experts",
    )(blk_e, n_used, xs, w_gate, w_up, w_down)


def _combine_kernel(tot_ref, dstc_ref, lpos_ref, gw_ref, hp_ref, hs_ref, y_hbm,
                    wsg_ref, wsu_ref, wsd_ref, g2_ref, b2_ref, op_ref, os_ref, gbuf, sem, *, tm, n_main):
    i = pl.program_id(0)
    rl = gbuf.shape[0] // ROW_TILE

    @pl.when(i == 0)
    def _():
        gbuf[...] = jnp.zeros(gbuf.shape, F32)

    copies = functools.partial(_chunk_copies, tot_ref, dstc_ref, gbuf, y_hbm, sem, to_hbm=False)
    copies(fn=lambda cp: cp.start())
    h = jnp.where(i < n_main, hp_ref[...], hs_ref[...])
    hb = h.astype(BF16)
    gate = _dot(hb, wsg_ref[...])
    up = _dot(hb, wsu_ref[...])
    ffn = _dot((gate * _sigmoid(gate) * up).astype(BF16), wsd_ref[...])
    cols = lax.broadcasted_iota(jnp.int32, (tm, rl), 1)
    place = jnp.zeros((tm, rl), F32)
    for k in range(TOP_K):
        place = jnp.where(cols == lpos_ref[:, k:k + 1], gw_ref[:, k:k + 1], place)
    copies(fn=lambda cp: cp.wait())
    ffn = ffn + _dot(place.astype(BF16), _from_row_tiles(gbuf, rl).astype(BF16))
    out = _layer_norm(DEEPNORM_ALPHA * h + ffn, g2_ref[...], b2_ref[...])

    @pl.when(i < n_main)
    def _():
        op_ref[...] = out

    @pl.when(i >= n_main)
    def _():
        os_ref[...] = out


def _combine(chunks, lpos_t, gw_t, h_p, h_s, ys, w_sh_gate, w_sh_up, w_sh_down, ln2_g, ln2_b, *, tm):
    tot, dstc = chunks
    n_tiles = tot.shape[0]
    D = h_p.shape[1]
    n_main = h_p.shape[0] // tm
    rl = _sorted_rows(tm)
    wsg, wsu, wsd = w_sh_gate.astype(BF16), w_sh_up.astype(BF16), w_sh_down.astype(BF16)
    g2, b2 = ln2_g.reshape(1, D), ln2_b.reshape(1, D)

    def full(a):
        return pl.BlockSpec(a.shape, lambda i: (0,) * a.ndim)

    tot_spec = pl.BlockSpec((1, 1, 1), lambda i: (i, 0, 0), memory_space=pltpu.SMEM)
    dstc_spec = pl.BlockSpec((1, 1, dstc.shape[2]), lambda i: (i, 0, 0), memory_space=pltpu.SMEM)
    tok = pl.BlockSpec((tm, ROUTE_ROWS), lambda i: (i, 0))
    groups = _two_group_specs(tm, D, n_main)
    return pl.pallas_call(
        functools.partial(_combine_kernel, tm=tm, n_main=n_main),
        grid=(n_tiles,),
        in_specs=[tot_spec, dstc_spec, tok, tok, *groups, pl.BlockSpec(memory_space=pl.ANY),
                  full(wsg), full(wsu), full(wsd), full(g2), full(b2)],
        out_specs=list(groups),
        out_shape=[jax.ShapeDtypeStruct(h_p.shape, F32), jax.ShapeDtypeStruct(h_s.shape, F32)],
        scratch_shapes=[pltpu.VMEM((rl * ROW_TILE, LANES), F32), pltpu.SemaphoreType.DMA(())],
        compiler_params=_cparams(("arbitrary",)),
        name="combine",
    )(tot, dstc, lpos_t, gw_t, h_p, h_s, ys, wsg, wsu, wsd, g2, b2)


def kernel(x_prompt, x_sample, cache_ckv, cache_krope, state_ret, page_table, w_in, q_norm_g, w_q_up, kv_norm_g,
           w_uk, w_uv, ret_gn_g, w_mla_o, w_ret_o, w_out, ln1_g, ln1_b, w_router, router_bias,
           w_exp_gate, w_exp_up, w_exp_down, w_sh_gate, w_sh_up, w_sh_down, ln2_g, ln2_b):
    B, S, D = x_prompt.shape
    DB, Q, _ = x_sample.shape
    Tp, Ts = B * S, DB * Q
    past_len = page_table.shape[1] * cache_ckv.shape[1]
    w_small, w_ret, w_gate, w_q, w_k, w_v, w_o = _prep_weights(w_in, w_q_up, w_uk, w_uv, w_mla_o)
    pw = (w_small, w_ret, w_gate, w_q, w_k, w_v)

    tab_p = _rope_tables(jnp.arange(S))
    (qh, ckv_p, kr_p, _, rq, rk, rv, rg, ga, gb, kh, vh) = _proj(
        x_prompt.reshape(Tp, D), tab_p, pw, q_norm_g, kv_norm_g, tm=512, ret_dtype=BF16, decode=False)
    o_mla = _attn_p(qh, kh, vh, B, S, tq=512, tk=512, hps=4)
    ret, ret_state_p = _ret_p(rq, rk, rv, rg, ret_gn_g, B, S, cb=4)
    mix_w = (w_o, w_ret_o, w_out, ln1_g, ln1_b, w_router, router_bias)
    tmoe = min(256, Ts)
    h_p, gw_p, lpos_p, *seg_p, cnt_p = _mix(x_prompt.reshape(Tp, D), o_mla, ret, ga, gb, *mix_w,
                                            jnp.zeros((N_EXPERTS, 1), F32), tm=tmoe)

    tab_s = _rope_tables(jnp.tile(past_len + jnp.arange(Q), DB))
    wabs, wsel, wuv = _absorb_weights(w_uk, w_uv)
    (_, ckv_s, kr_s, krp_s, rq_s, rk_s, rv_s, rg_s, ga_s, gb_s, ql_s, qr_s) = _proj(
        x_sample.reshape(Ts, D), tab_s, pw[:4] + (wabs, wsel), q_norm_g, kv_norm_g, tm=Ts, ret_dtype=F32,
        decode=True)

    def rows_by_head(a):
        return a.reshape(DB, Q, MLA_HEADS, LANES).transpose(0, 2, 1, 3).reshape(DB, MLA_HEADS * Q, LANES)

    o_lat = _attn_s(page_table, rows_by_head(ql_s), rows_by_head(qr_s), ckv_s, krp_s, cache_ckv, cache_krope,
                    tk=min(1024, past_len))
    o_lat_tok = o_lat.reshape(DB, MLA_HEADS, Q, LANES).transpose(0, 2, 1, 3).reshape(Ts, MLA_HEADS * LANES)
    o_mla_s = _lat_up(o_lat_tok, wuv)
    ret_s, ret_state_s = _ret_s(rq_s, rk_s, rv_s, rg_s, state_ret, ret_gn_g, gb=min(16, DB))
    h_s, gw_s, lpos_s, *seg_s, cnt = _mix(x_sample.reshape(Ts, D), o_mla_s, ret_s, ga_s, gb_s, *mix_w,
                                          cnt_p, tm=tmoe)

    n_blocks = _n_row_blocks(Tp + Ts)
    plan = _block_plan(cnt, n_blocks)
    n_tiles = (Tp + Ts) // tmoe
    seg = _segment_tables(plan[0], [jnp.concatenate(t, axis=0) for t in zip(seg_p, seg_s)], n_tiles)
    lpos = jnp.concatenate([lpos_p, lpos_s], axis=1)
    gw = jnp.concatenate([gw_p, gw_s], axis=1)
    chunks = _chunk_tables(seg, _sorted_rows(tmoe) // SEG_CAP)
    xs = _dispatch(plan, cnt.reshape(N_EXPERTS).astype(jnp.int32), chunks, lpos, h_p, h_s, n_blocks, tm=tmoe)
    ys = _experts(xs, plan[2], plan[3], w_exp_gate, w_exp_up, w_exp_down)
    y_p, y_s = _combine(chunks, lpos.T, gw.T, h_p, h_s, ys, w_sh_gate, w_sh_up, w_sh_down, ln2_g, ln2_b, tm=tmoe)

    return (y_p.reshape(B, S, D), y_s.reshape(DB, Q, D),
            ckv_p.reshape(B, S, -1), kr_p.reshape(B, S, -1), ret_state_p,
            ckv_s.reshape(DB, Q, -1), kr_s.reshape(DB, Q, -1), ret_state_s)
```
